```python
import jax, jax.numpy as jnp
from jax import lax
import numpy as np

D_MODEL = 1024
BATCH = 2
SEQ = 8192
DEPTH = 2

GRID_W = 64
CTX_LEN = 256

HEAD = 64
D_RWKV = D_MODEL // 2
H_RWKV = D_RWKV // HEAD
D_CONV = D_MODEL // 4
D_FOUR = D_MODEL - D_RWKV - D_CONV
FOUR_GROUP = 64
G_FOUR = D_FOUR // FOUR_GROUP
LORA_W = 64
LORA_A = 64
LORA_G = 128
D_RWKV_IN = 3 * D_RWKV + 2 * LORA_W + 2 * LORA_A + LORA_G
RWKV_SPLITS = (D_RWKV, 2 * D_RWKV, 3 * D_RWKV, 3 * D_RWKV + LORA_W, 3 * D_RWKV + 2 * LORA_W,
               3 * D_RWKV + 2 * LORA_W + LORA_A, 3 * D_RWKV + 2 * LORA_W + 2 * LORA_A)
D_IN = D_RWKV_IN + 3 * D_CONV + D_FOUR
CONV_W = 3
N_GROUPS = 4
EXPERTS_PER_GROUP = 8
N_EXPERTS = N_GROUPS * EXPERTS_PER_GROUP
TOP_K = 2
D_EXPERT = D_MODEL // 2
MOE_BLOCK = 128
RMS_EPS = 1e-6
GN_EPS = 64e-5

kernel_name = "hybrid_rwkv7_shortconv_fourier_hmoe_dit"


def rmsnorm(x, gain):
    xf = x.astype(jnp.float32)
    y = xf * lax.rsqrt(jnp.mean(xf * xf, axis=-1, keepdims=True) + RMS_EPS)
    return (y * gain).astype(x.dtype)


def to_heads(t):
    return t.reshape(t.shape[0], t.shape[1], H_RWKV, HEAD)


def grid_shift(z):
    b, length, ch = z.shape
    rows = length // GRID_W
    zg = z.reshape(b, rows, GRID_W, ch // 4, 4)
    left = jnp.pad(zg[:, :, :-1, :, 0], ((0, 0), (0, 0), (1, 0), (0, 0)))
    right = jnp.pad(zg[:, :, 1:, :, 1], ((0, 0), (0, 0), (0, 1), (0, 0)))
    up = jnp.pad(zg[:, :-1, :, :, 2], ((0, 0), (1, 0), (0, 0), (0, 0)))
    down = jnp.pad(zg[:, 1:, :, :, 3], ((0, 0), (0, 1), (0, 0), (0, 0)))
    return jnp.stack([left, right, up, down], axis=-1).reshape(b, length, ch)


def seq_shift(z):
    b, length, ch = z.shape
    zs = z.reshape(b, length, ch // 2, 2)
    prev = jnp.pad(zs[:, :-1, :, 0], ((0, 0), (1, 0), (0, 0)))
    nxt = jnp.pad(zs[:, 1:, :, 1], ((0, 0), (0, 1), (0, 0)))
    return jnp.stack([prev, nxt], axis=-1).reshape(b, length, ch)


def rwkv_streams(z, decay_w0, decay_w2, iclr_a0, iclr_a2, gate_g2, k_k, k_a):
    zf = z.astype(jnp.float32)
    r, k, v, lw_f, lw_b, la_f, la_b, lg = jnp.split(zf, RWKV_SPLITS, axis=-1)
    g = jax.nn.sigmoid(lg) @ gate_g2
    kk = to_heads(k * k_k)
    kk = kk / jnp.maximum(jnp.sqrt(jnp.sum(kk * kk, axis=-1, keepdims=True)), 1e-12)
    per_dir = []
    for dn, (lw, la) in enumerate(((lw_f, la_f), (lw_b, la_b))):
        w_log = -jax.nn.softplus(-(decay_w0[dn] + jnp.tanh(lw) @ decay_w2[dn])) - 0.5
        decay = jnp.exp(-jnp.exp(w_log))
        a = jax.nn.sigmoid(iclr_a0[dn] + la @ iclr_a2[dn])
        k_dir = k * (1.0 + (a - 1.0) * k_a)
        per_dir.append((to_heads(decay), to_heads(k_dir), to_heads(a)))
    return to_heads(r), to_heads(v), kk, g, per_dir


def wkv7_scan(s0, r, decay, k, v, kk, a, reverse, readout):
    def step(s, inp):
        r_t, d_t, k_t, v_t, kk_t, a_t = inp
        s_kk = jnp.einsum('bhvk,bhk->bhv', s, kk_t)
        s = (s * d_t[:, :, None, :]
             - s_kk[..., None] * (kk_t * a_t)[:, :, None, :]
             + v_t[..., None] * k_t[:, :, None, :])
        y_t = jnp.einsum('bhvk,bhk->bhv', s, r_t) if readout else None
        return s, y_t
    xs = tuple(jnp.swapaxes(t, 0, 1) for t in (r, decay, k, v, kk, a))
    s, y = lax.scan(step, s0, xs, reverse=reverse)
    return s, (jnp.swapaxes(y, 0, 1) if readout else None)


def rwkv_readout(ys, r, k_dirs, v, g, r_k, gn_w, gn_b):
    b, length = r.shape[0], r.shape[1]
    y = ys[0] + ys[1]
    mu = jnp.mean(y, axis=-1, keepdims=True)
    var = jnp.mean(jnp.square(y - mu), axis=-1, keepdims=True)
    yn = ((y - mu) * lax.rsqrt(var + GN_EPS)).reshape(b, length, D_RWKV) * gn_w + gn_b
    bonus = (jnp.sum(r * k_dirs[0] * r_k, axis=-1, keepdims=True) * v
             + jnp.sum(r * k_dirs[1] * r_k, axis=-1, keepdims=True) * v)
    return (yn + bonus.reshape(b, length, D_RWKV)) * g


def gated_short_conv(p, conv_w, conv_gain):
    bg, cg, h = jnp.split(p, 3, axis=-1)
    z = cg * h
    prev = jnp.pad(z[:, :-1], ((0, 0), (1, 0), (0, 0)))
    nxt = jnp.pad(z[:, 1:], ((0, 0), (0, 1), (0, 0)))
    y = bg * (conv_w[0] * prev + conv_w[1] * z + conv_w[2] * nxt)
    return rmsnorm(y, conv_gain)


def fourier_mix(p, four_gain):
    b, length, _ = p.shape
    f = p.astype(jnp.float32).reshape(b, length, G_FOUR, FOUR_GROUP)
    y = jnp.fft.fft2(f, axes=(1, 3), norm='ortho').real.reshape(b, length, D_FOUR)
    return rmsnorm(y.astype(p.dtype), four_gain)


def assemble(p, o_rwkv, conv_w, conv_gain, four_gain, w_out):
    cut1, cut2 = D_RWKV_IN, D_RWKV_IN + 3 * D_CONV
    conv_o = gated_short_conv(p[..., cut1:cut2], conv_w, conv_gain)
    four_o = fourier_mix(p[..., cut2:], four_gain)
    return jnp.concatenate([o_rwkv.astype(p.dtype), conv_o, four_o], axis=-1) @ w_out


def token_mixers(hx, hc, w_in, mu_shift, decay_w0, decay_w2, iclr_a0, iclr_a2, gate_g2, k_k, k_a, r_k,
                 gn_w, gn_b, conv_w, conv_gain, four_gain, w_out, need_ctx):
    px = hx @ w_in
    pc = hc @ w_in
    zx = px[..., :D_RWKV_IN]
    zx = zx + (grid_shift(zx) - zx) * mu_shift
    zc = pc[..., :D_RWKV_IN]
    zc = zc + (seq_shift(zc) - zc) * mu_shift
    rx, vx, kkx, gx, dx = rwkv_streams(zx, decay_w0, decay_w2, iclr_a0, iclr_a2, gate_g2, k_k, k_a)
    rc, vc, kkc, gc, dc = rwkv_streams(zc, decay_w0, decay_w2, iclr_a0, iclr_a2, gate_g2, k_k, k_a)
    s0 = jnp.zeros((hx.shape[0], H_RWKV, HEAD, HEAD), jnp.float32)
    ys_x, ys_c = [], []
    for dn in range(2):
        rev = dn == 1
        dec_c, k_c, a_c = dc[dn]
        s_ctx, y_c = wkv7_scan(s0, rc, dec_c, k_c, vc, kkc, a_c, rev, need_ctx)
        dec_x, k_x, a_x = dx[dn]
        _, y_x = wkv7_scan(s_ctx, rx, dec_x, k_x, vx, kkx, a_x, rev, True)
        ys_x.append(y_x)
        ys_c.append(y_c)
    ox = rwkv_readout(ys_x, rx, [d[1] for d in dx], vx, gx, r_k, gn_w, gn_b)
    out_x = assemble(px, ox, conv_w, conv_gain, four_gain, w_out)
    if not need_ctx:
        return out_x, None
    oc = rwkv_readout(ys_c, rc, [d[1] for d in dc], vc, gc, r_k, gn_w, gn_b)
    out_c = assemble(pc, oc, conv_w, conv_gain, four_gain, w_out)
    return out_x, out_c


def hier_moe(h, router_g_w, router_g_b, router_e_w, router_e_b, exp_gate, exp_up, exp_down):
    n, d = h.shape
    hf = h.astype(jnp.float32)
    pg = jax.nn.softmax(hf @ router_g_w.astype(jnp.float32) + router_g_b, axis=-1)
    pg_top, grp = lax.top_k(pg, 1)
    le = (hf @ router_e_w.astype(jnp.float32) + router_e_b).reshape(n, N_GROUPS, EXPERTS_PER_GROUP)
    le_g = jnp.take_along_axis(le, grp[:, :, None], axis=1)[:, 0]
    top_l, top_i = lax.top_k(le_g, TOP_K)
    gates = pg_top * jax.nn.softmax(top_l, axis=-1)
    eid = grp * EXPERTS_PER_GROUP + top_i
    m = n * TOP_K
    flat_e = eid.reshape(m)
    flat_t = jnp.repeat(jnp.arange(n, dtype=jnp.int32), TOP_K)
    flat_w = gates.reshape(m)
    order = jnp.argsort(flat_e)
    se = flat_e[order]
    counts = jnp.bincount(flat_e, length=N_EXPERTS)
    start = jnp.cumsum(counts) - counts
    pcounts = (counts + MOE_BLOCK - 1) // MOE_BLOCK * MOE_BLOCK
    pend = jnp.cumsum(pcounts)
    pstart = pend - pcounts
    dest = pstart[se] + jnp.arange(m, dtype=jnp.int32) - start[se]
    n_blocks = -(-m // MOE_BLOCK) + N_EXPERTS
    cap = n_blocks * MOE_BLOCK
    buf_t = jnp.full((cap,), n, jnp.int32).at[dest].set(flat_t[order])
    buf_w = jnp.zeros((cap,), jnp.float32).at[dest].set(flat_w[order])
    block_e = jnp.minimum(jnp.searchsorted(pend, jnp.arange(n_blocks) * MOE_BLOCK, side='right'),
                          N_EXPERTS - 1)
    h_pad = jnp.concatenate([h, jnp.zeros((1, d), h.dtype)], axis=0)
    xb = h_pad[buf_t].reshape(n_blocks, MOE_BLOCK, d)

    def expert_block(args):
        xblk, e = args
        return (jax.nn.silu(xblk @ exp_gate[e]) * (xblk @ exp_up[e])) @ exp_down[e]

    yb = lax.map(expert_block, (xb, block_e)).reshape(cap, d)
    out = jnp.zeros((n + 1, d), jnp.float32).at[buf_t].add(yb.astype(jnp.float32) * buf_w[:, None])
    return out[:n].astype(h.dtype)


def setup_inputs(seed: int = 0) -> dict:
    key = jax.random.key(seed)
    k = jax.random.split(key, 32)
    f32 = jnp.float32
    L, D = DEPTH, D_MODEL

    def nrm(i, shape, s):
        return jax.random.normal(k[i], shape, f32) * s

    return {
        "x": nrm(0, (BATCH, SEQ, D), 1.0),
        "c": nrm(1, (BATCH, D), 1.0),
        "ctx": nrm(2, (BATCH, CTX_LEN, D), 1.0),
        "c_ctx": nrm(3, (D,), 1.0),
        "ada_w": nrm(4, (L, D, 6 * D), 0.5 * D ** -0.5),
        "ada_b": nrm(5, (L, 6 * D), 0.02),
        "norm1_g": 1.0 + nrm(6, (L, D), 0.05),
        "norm2_g": 1.0 + nrm(7, (L, D), 0.05),
        "w_in": nrm(8, (L, D, D_IN), D ** -0.5),
        "mu_shift": jax.random.uniform(k[9], (L, D_RWKV_IN), f32),
        "decay_w0": jax.random.uniform(k[10], (L, 2, D_RWKV), f32, -6.0, -1.0),
        "decay_w2": nrm(11, (L, 2, LORA_W, D_RWKV), 0.5 * LORA_W ** -0.5),
        "iclr_a0": nrm(12, (L, 2, D_RWKV), 0.1),
        "iclr_a2": nrm(13, (L, 2, LORA_A, D_RWKV), 0.5 * LORA_A ** -0.5),
        "gate_g2": nrm(14, (L, LORA_G, D_RWKV), LORA_G ** -0.5),
        "k_k": 0.85 + nrm(15, (L, D_RWKV), 0.05),
        "k_a": 1.0 + nrm(16, (L, D_RWKV), 0.05),
        "r_k": nrm(17, (L, H_RWKV, HEAD), 0.1),
        "gn_w": 1.0 + nrm(18, (L, D_RWKV), 0.05),
        "gn_b": nrm(19, (L, D_RWKV), 0.02),
        "conv_w": nrm(20, (L, CONV_W, D_CONV), 0.5),
        "conv_gain": 1.0 + nrm(21, (L, D_CONV), 0.05),
        "four_gain": 1.0 + nrm(22, (L, D_FOUR), 0.05),
        "w_out": nrm(23, (L, D, D), D ** -0.5),
        "router_g_w": nrm(24, (L, D, N_GROUPS), D ** -0.5),
        "router_g_b": nrm(25, (L, N_GROUPS), 0.01),
        "router_e_w": nrm(26, (L, D, N_EXPERTS), D ** -0.5),
        "router_e_b": nrm(27, (L, N_EXPERTS), 0.01),
        "exp_gate": nrm(28, (L, N_EXPERTS, D, D_EXPERT), D ** -0.5),
        "exp_up": nrm(29, (L, N_EXPERTS, D, D_EXPERT), D ** -0.5),
        "exp_down": nrm(30, (L, N_EXPERTS, D_EXPERT, D), D_EXPERT ** -0.5),
        "final_g": 1.0 + nrm(31, (D,), 0.05),
    }


def reference(x, c, ctx, c_ctx, ada_w, ada_b, norm1_g, norm2_g, w_in, mu_shift, decay_w0, decay_w2,
              iclr_a0, iclr_a2, gate_g2, k_k, k_a, r_k, gn_w, gn_b, conv_w, conv_gain, four_gain, w_out,
              router_g_w, router_g_b, router_e_w, router_e_b, exp_gate, exp_up, exp_down, final_g):
    bsz, seq, d = x.shape
    n_lat = bsz * seq
    for l in range(DEPTH):
        last = l == DEPTH - 1
        mod_x = jax.nn.silu(c) @ ada_w[l] + ada_b[l]
        mod_c = jax.nn.silu(c_ctx) @ ada_w[l] + ada_b[l]
        sh1x, sc1x, g1x, sh2x, sc2x, g2x = jnp.split(mod_x[:, None, :], 6, axis=-1)
        sh1c, sc1c, g1c, sh2c, sc2c, g2c = jnp.split(mod_c, 6)
        hx = rmsnorm(x, norm1_g[l]) * (1.0 + sc1x) + sh1x
        hc = rmsnorm(ctx, norm1_g[l]) * (1.0 + sc1c) + sh1c
        mx, mc = token_mixers(hx, hc, w_in[l], mu_shift[l], decay_w0[l], decay_w2[l], iclr_a0[l], iclr_a2[l],
                              gate_g2[l], k_k[l], k_a[l], r_k[l], gn_w[l], gn_b[l], conv_w[l], conv_gain[l],
                              four_gain[l], w_out[l], not last)
        x = x + g1x * mx
        hx = rmsnorm(x, norm2_g[l]) * (1.0 + sc2x) + sh2x
        moe_args = (router_g_w[l], router_g_b[l], router_e_w[l], router_e_b[l], exp_gate[l], exp_up[l], exp_down[l])
        if last:
            x = x + g2x * hier_moe(hx.reshape(n_lat, d), *moe_args).reshape(x.shape)
        else:
            ctx = ctx + g1c * mc
            hc = rmsnorm(ctx, norm2_g[l]) * (1.0 + sc2c) + sh2c
            y = hier_moe(jnp.concatenate([hx.reshape(n_lat, d), hc.reshape(-1, d)], axis=0), *moe_args)
            x = x + g2x * y[:n_lat].reshape(x.shape)
            ctx = ctx + g2c * y[n_lat:].reshape(ctx.shape)
    return rmsnorm(x, final_g)
```

```python
import functools

import numpy as np
import jax
import jax.numpy as jnp
from jax import lax
from jax.experimental import pallas as pl
from jax.experimental.pallas import tpu as pltpu

F32 = jnp.float32
BF16 = jnp.bfloat16

D_MODEL = 1024
HEAD = 64
D_RWKV = 512
H_RWKV = D_RWKV // HEAD
D_CONV = 256
D_FOUR = 256
FOUR_GROUP = 64
D_Z = 3 * D_RWKV + 2 * 64 + 2 * 64 + 128
D_IN = D_Z + 3 * D_CONV + D_FOUR
GRID_W = 64
N_GROUPS = 4
EXPERTS_PER_GROUP = 8
N_EXPERTS = N_GROUPS * EXPERTS_PER_GROUP
D_EXPERT = D_MODEL // 2
RMS_EPS = 1e-6
GN_EPS = 64e-5

CHUNK = 64
ROW_BLOCK = 256
MOE_ROWS = 256
FOUR_INNER = 128
LANES = 128
VMEM_LIMIT = 48 * 1024 * 1024

NN = (((1,), (0,)), ((), ()))
NT = (((1,), (1,)), ((), ()))
TN = (((0,), (0,)), ((), ()))


def _params(*sem):
    return pltpu.CompilerParams(dimension_semantics=sem, vmem_limit_bytes=VMEM_LIMIT)


def _split2(x):
    hi = x.astype(BF16)
    lo = (x - hi.astype(F32)).astype(BF16)
    return hi, lo


def _dot(a, b, dims=NN):
    return lax.dot_general(a, b, dims, preferred_element_type=F32)


def _dot1(a, b, dims=NN):
    return _dot(a.astype(BF16), b.astype(BF16), dims)


def _dot3(a, b, dims=NN):
    ah, al = _split2(a)
    bh, bl = _split2(b)
    return _dot(ah, bh, dims) + (_dot(ah, bl, dims) + _dot(al, bh, dims))


def _dot3c(ch, cl, x, dims=NN):
    xh, xl = _split2(x)
    return _dot(ch, xh, dims) + (_dot(cl, xh, dims) + _dot(ch, xl, dims))


def _dot3r(x, ch, cl, dims=NN):
    xh, xl = _split2(x)
    return _dot(xh, ch, dims) + (_dot(xl, ch, dims) + _dot(xh, cl, dims))


def _headsum(x, ones_bd):
    hi, lo = _split2(x)
    return _dot(hi, ones_bd) + _dot(lo, ones_bd)


def _rms(x, eps=RMS_EPS):
    return x * lax.rsqrt(jnp.mean(x * x, axis=-1, keepdims=True) + eps)


def _sigmoid(x):
    return 1.0 / (1.0 + jnp.exp(-x))


def _softplus(x):
    return jnp.maximum(x, 0.0) + jnp.log(1.0 + jnp.exp(-jnp.abs(x)))


def _inproj_body(x_ref, g_ref, sc_ref, sh_ref, w_ref, z_ref, cv_ref, fo_ref):
    h = _rms(x_ref[...]) * g_ref[...] * (1.0 + sc_ref[0]) + sh_ref[0]
    p = _dot(h.astype(BF16), w_ref[...])
    z_ref[...] = p[:, :D_Z]
    cv_ref[...] = p[:, D_Z:D_Z + 3 * D_CONV]
    fo_ref[...] = p[:, D_Z + 3 * D_CONV:]


def _inproj(xa, gain, sc, sh, w_bf, ent):
    n = xa.shape[0]
    row = lambda w: pl.BlockSpec((ROW_BLOCK, w), lambda i: (i, 0))
    full = lambda a: pl.BlockSpec(a.shape, lambda i: (0,) * a.ndim)
    mod = pl.BlockSpec((1, 1, D_MODEL), lambda i: (ent(i), 0, 0))
    return pl.pallas_call(
        _inproj_body,
        grid=(n // ROW_BLOCK,),
        in_specs=[row(D_MODEL), full(gain), mod, mod, full(w_bf)],
        out_specs=[row(D_Z), row(3 * D_CONV), row(D_FOUR)],
        out_shape=[jax.ShapeDtypeStruct((n, D_Z), F32), jax.ShapeDtypeStruct((n, 3 * D_CONV), F32),
                   jax.ShapeDtypeStruct((n, D_FOUR), F32)],
        compiler_params=_params("parallel"),
        name="inproj",
    )(xa, gain, sc, sh, w_bf)


def _streams_body(n_lat_blocks, blocks_per_seq, zm_ref, zp_ref, zn_ref, mu_ref, w2_ref, a2_ref, g2_ref,
                  w0_ref, a0_ref, kkw_ref, ka_ref, rk_ref, ones_ref,
                  r_out, v_out, kk_out, g_out, bon_out, lw_out, kd_out, b_out):
    i = pl.program_id(0)
    ctx_i = (i >= n_lat_blocks).astype(jnp.int32)
    z = zm_ref[...]
    tb = z.shape[0]
    t = lax.broadcasted_iota(jnp.int32, (tb, 1), 0)
    c = lax.broadcasted_iota(jnp.int32, (1, D_Z), 1)
    ctx_v = jnp.zeros((tb, 1), jnp.int32) + ctx_i
    col = t & (GRID_W - 1)
    lmask = (col != 0) | ((ctx_v != 0) & (t != 0))
    rmask = (col != GRID_W - 1) | ((ctx_v != 0) & (t != tb - 1))
    seq_pos = i % blocks_per_seq
    top_v = jnp.zeros((tb, 1), jnp.int32) + (seq_pos == 0).astype(jnp.int32)
    bot_v = jnp.zeros((tb, 1), jnp.int32) + (seq_pos == blocks_per_seq - 1).astype(jnp.int32)
    umask = jnp.logical_not((top_v != 0) & (t < GRID_W))
    dmask = jnp.logical_not((bot_v != 0) & (t >= tb - GRID_W))
    left = jnp.where(lmask, pltpu.roll(z, 1, axis=0), 0.0)
    right = jnp.where(rmask, pltpu.roll(z, tb - 1, axis=0), 0.0)
    up = jnp.where(umask, jnp.concatenate([zp_ref[...], z[:tb - GRID_W]], axis=0), 0.0)
    down = jnp.where(dmask, jnp.concatenate([z[GRID_W:], zn_ref[...]], axis=0), 0.0)
    q = c & jnp.where(ctx_i != 0, 1, 3)
    shifted = jnp.where(q == 0, left, jnp.where(q == 1, right, jnp.where(q == 2, up, down)))
    z = z + (shifted - z) * mu_ref[...]

    r = z[:, 0:D_RWKV]
    k = z[:, D_RWKV:2 * D_RWKV]
    v = z[:, 2 * D_RWKV:3 * D_RWKV]
    lw_in = z[:, 3 * D_RWKV:3 * D_RWKV + 128]
    la_in = z[:, 3 * D_RWKV + 128:3 * D_RWKV + 256]
    lg = z[:, 3 * D_RWKV + 256:]
    ones_bd = ones_ref[...]

    g_out[...] = _dot3(_sigmoid(lg), g2_ref[...])
    kq = k * kkw_ref[...]
    kk = kq / jnp.maximum(jnp.sqrt(_headsum(kq * kq, ones_bd)), 1e-12)
    w_log = -_softplus(-(w0_ref[...] + _dot3(jnp.tanh(lw_in), w2_ref[...]))) - 0.5
    lw = -jnp.exp(w_log)
    a = _sigmoid(a0_ref[...] + _dot3(la_in, a2_ref[...]))
    ka = ka_ref[...]
    ksum = jnp.zeros_like(k)
    for d in range(2):
        a_d = a[:, d * D_RWKV:(d + 1) * D_RWKV]
        k_d = k * (1.0 + (a_d - 1.0) * ka)
        ksum = ksum + k_d
        lw_out[d] = lw[:, d * D_RWKV:(d + 1) * D_RWKV]
        kd_out[d] = k_d
        b_out[d] = kk * a_d
    r_out[...] = r
    v_out[...] = v
    kk_out[...] = kk
    bon_out[...] = _headsum(r * ksum * rk_ref[...], ones_bd) * v


def _streams(pz, seq, ctx_len, bsz, mu, w2bd, a2bd, g2, w0, a0, kkw, ka, rk, ones_bd):
    n = pz.shape[0]
    assert ctx_len == ROW_BLOCK and seq % ROW_BLOCK == 0
    n_lat_blocks = bsz * seq // ROW_BLOCK
    sub = ROW_BLOCK // GRID_W
    last = n // GRID_W - 1
    full = lambda a: pl.BlockSpec(a.shape, lambda i: (0,) * a.ndim)
    row = pl.BlockSpec((ROW_BLOCK, D_RWKV), lambda i: (i, 0))
    row2 = pl.BlockSpec((2, ROW_BLOCK, D_RWKV), lambda i: (0, i, 0))
    s1 = jax.ShapeDtypeStruct((n, D_RWKV), F32)
    s2 = jax.ShapeDtypeStruct((2, n, D_RWKV), F32)
    consts = (mu, w2bd, a2bd, g2, w0, a0, kkw, ka, rk, ones_bd)
    return pl.pallas_call(
        functools.partial(_streams_body, n_lat_blocks, seq // ROW_BLOCK),
        grid=(n // ROW_BLOCK,),
        in_specs=[pl.BlockSpec((ROW_BLOCK, D_Z), lambda i: (i, 0)),
                  pl.BlockSpec((GRID_W, D_Z), lambda i: (jnp.maximum(i * sub - 1, 0), 0)),
                  pl.BlockSpec((GRID_W, D_Z), lambda i: (jnp.minimum((i + 1) * sub, last), 0))]
                 + [full(a) for a in consts],
        out_specs=[row] * 5 + [row2] * 3,
        out_shape=[s1] * 5 + [s2] * 3,
        compiler_params=_params("parallel"),
        name="streams",
    )(pz, pz, pz, *consts)


def _unit_tri_inverse(a_ab, strict, ti, tj, mm):
    eye = (ti == tj).astype(F32)
    base = CHUNK // 8
    a0 = jnp.where(strict & ((ti >> 3) == (tj >> 3)), a_ab, 0.0)
    a2 = mm(a0, a0)
    x = eye + a0
    x = x + mm(x, a2)
    x = x + mm(x, mm(a2, a2))
    s = 3
    while (1 << s) < CHUNK:
        off = strict & ((ti >> (s + 1)) == (tj >> (s + 1))) & ((ti >> s) != (tj >> s))
        x = x + mm(x, mm(jnp.where(off, a_ab, 0.0), x))
        s += 1
    del base
    return x


def _chunk_body(r_ref, v_ref, kk_ref, lw_ref, kd_ref, b_ref, p_out, q_out, rp_out, y0_out):
    d = pl.program_id(0)
    cs = CHUNK
    ti = lax.broadcasted_iota(jnp.int32, (cs, cs), 0)
    tj = lax.broadcasted_iota(jnp.int32, (cs, cs), 1)
    sgn = jnp.where(d == 0, 1, -1)
    order = (ti - tj) * sgn
    strict = order > 0
    incl = order >= 0
    eye = ti == tj

    lw = lw_ref[0]
    t_incl = incl.astype(F32).astype(BF16)
    l1 = lw.astype(BF16)
    rem = lw - l1.astype(F32)
    l2 = rem.astype(BF16)
    l3 = (rem - l2.astype(F32)).astype(BF16)
    gcum = _dot(t_incl, l1) + (_dot(t_incl, l2) + _dot(t_incl, l3))
    gtot = jnp.sum(lw, axis=0, keepdims=True)
    e_pos = jnp.exp(gcum)
    e_neg = jnp.exp(-gcum)
    e_rem = jnp.exp(gtot - gcum)
    e_prev = jnp.exp(gcum - lw)
    e_tot = jnp.exp(gtot)

    r = r_ref[...]
    v = v_ref[...]
    kk = kk_ref[...]
    kd = kd_ref[0]
    b = b_ref[0]
    at = -(kk * e_prev)
    bt = b * e_neg
    kt = kd * e_neg
    rt = r * e_pos
    bh = b * e_rem
    kh = kd * e_rem

    mm = _dot1
    zeros = jnp.zeros((cs, HEAD), F32)
    p_parts, q_parts, rp_parts, y0_parts = [], [], [], []
    for h in range(H_RWKV):
        sl = slice(h * HEAD, (h + 1) * HEAD)
        s = mm(jnp.concatenate([at[:, sl], rt[:, sl]], axis=0),
               jnp.concatenate([bt[:, sl], kt[:, sl]], axis=0), NT)
        a_ab = jnp.where(strict, s[:cs, :cs], 0.0)
        a_ak = jnp.where(strict, s[:cs, cs:], 0.0)
        a_rb = jnp.where(incl, s[cs:, :cs], 0.0)
        a_rk = jnp.where(incl, s[cs:, cs:], 0.0)
        minv = _unit_tri_inverse(a_ab, strict, ti, tj, mm)
        vh = v[:, sl]
        wu = mm(minv, jnp.concatenate([at[:, sl], mm(a_ak, vh)], axis=1))
        rhs = jnp.concatenate([wu, jnp.concatenate([zeros, vh], axis=1)], axis=0)
        top = mm(jnp.concatenate([bh[:, sl], kh[:, sl]], axis=0), rhs, TN)
        bot = mm(jnp.concatenate([a_rb, a_rk], axis=1), rhs)
        p_parts.append(top[:, :HEAD] + jnp.where(eye, e_tot[:, sl], 0.0))
        q_parts.append(top[:, HEAD:])
        rp_parts.append(rt[:, sl] + bot[:, :HEAD])
        y0_parts.append(bot[:, HEAD:])
    p_out[0, 0] = jnp.concatenate(p_parts, axis=1)
    q_out[0, 0] = jnp.concatenate(q_parts, axis=1)
    rp_out[0] = jnp.concatenate(rp_parts, axis=1)
    y0_out[0] = jnp.concatenate(y0_parts, axis=1)


def _chunk_local(r, v, kk, lw, kd, b):
    n = r.shape[0]
    nc = n // CHUNK
    shared = pl.BlockSpec((CHUNK, D_RWKV), lambda d, j: (j, 0))
    per_dir = pl.BlockSpec((1, CHUNK, D_RWKV), lambda d, j: (d, j, 0))
    mat = pl.BlockSpec((1, 1, HEAD, D_RWKV), lambda d, j: (d, j, 0, 0))
    sm = jax.ShapeDtypeStruct((2, nc, HEAD, D_RWKV), F32)
    st = jax.ShapeDtypeStruct((2, n, D_RWKV), F32)
    return pl.pallas_call(
        _chunk_body,
        grid=(2, nc),
        in_specs=[shared, shared, shared, per_dir, per_dir, per_dir],
        out_specs=[mat, mat, per_dir, per_dir],
        out_shape=[sm, sm, st, st],
        compiler_params=_params("parallel", "parallel"),
        name="chunk_local",
    )(r, v, kk, lw, kd, b)


def _scan_body(p_ref, q_ref, rp_ref, y0_ref, y_out, h_scr):
    @pl.when(pl.program_id(2) == 0)
    def _():
        h_scr[...] = jnp.zeros_like(h_scr)

    hst = h_scr[...]
    p = p_ref[0, 0]
    q = q_ref[0, 0]
    rp = rp_ref[0]
    y0 = y0_ref[0]
    h_parts, y_parts = [], []
    for h in range(H_RWKV):
        sl = slice(h * HEAD, (h + 1) * HEAD)
        o = _dot3(jnp.concatenate([p[:, sl], rp[:, sl]], axis=0), hst[:, sl])
        h_parts.append(o[:HEAD] + q[:, sl])
        y_parts.append(y0[:, sl] + o[HEAD:])
    h_scr[...] = jnp.concatenate(h_parts, axis=1)
    y_out[0] = jnp.concatenate(y_parts, axis=1)


def _chunk_scan(p, q, rp, y0, seq, ctx_len, bsz):
    n = rp.shape[1]
    xpc, cpc = seq // CHUNK, ctx_len // CHUNK

    def chunk(d, b, i):
        in_ctx = i < cpc
        fwd = jnp.where(in_ctx, bsz * xpc + b * cpc + i, b * xpc + i - cpc)
        bwd = jnp.where(in_ctx, bsz * xpc + b * cpc + cpc - 1 - i, b * xpc + xpc - 1 - (i - cpc))
        return jnp.where(d == 0, fwd, bwd)

    mat = pl.BlockSpec((1, 1, HEAD, D_RWKV), lambda d, b, i: (d, chunk(d, b, i), 0, 0))
    tok = pl.BlockSpec((1, CHUNK, D_RWKV), lambda d, b, i: (d, chunk(d, b, i), 0))
    return pl.pallas_call(
        _scan_body,
        grid=(2, bsz, xpc + cpc),
        in_specs=[mat, mat, tok, tok],
        out_specs=tok,
        out_shape=jax.ShapeDtypeStruct((2, n, D_RWKV), F32),
        scratch_shapes=[pltpu.VMEM((HEAD, D_RWKV), F32)],
        compiler_params=_params("parallel", "parallel", "arbitrary"),
        name="chunk_scan",
    )(p, q, rp, y0)


def _hilo(a):
    if isinstance(a, np.ndarray):
        a = a.astype(np.float32)
        hi = a.astype(BF16)
        return jnp.asarray(hi), jnp.asarray((a - hi.astype(np.float32)).astype(BF16))
    hi = a.astype(BF16)
    return hi, (a - hi.astype(F32)).astype(BF16)


def _channel_tables(length):
    j = np.arange(FOUR_GROUP)
    ang = 2.0 * np.pi * np.outer(j, j) / FOUR_GROUP
    scale = 1.0 / np.sqrt(float(length) * FOUR_GROUP)
    groups = D_FOUR // FOUR_GROUP
    c4 = np.kron(np.eye(groups), np.cos(ang)) * scale
    s4 = np.kron(np.eye(groups), np.sin(ang)) * scale
    return _hilo(c4) + _hilo(s4)


def _four_finish(fr, fi, c4h, c4l, s4h, s4l, gain):
    y = _dot3r(fr, c4h, c4l) + _dot3r(fi, s4h, s4l)
    return _rms(y) * gain


def _four_direct_body(f_ref, mh_ref, ml_ref, c4h, c4l, s4h, s4l, gain_ref, o_ref):
    length = f_ref.shape[1]
    fc = _dot3c(mh_ref[...], ml_ref[...], f_ref[0])
    o_ref[0] = _four_finish(fc[:length], fc[length:], c4h[...], c4l[...], s4h[...], s4l[...], gain_ref[...])


def _fourier_direct(f, gain):
    bsz, length, _ = f.shape
    t = np.arange(length)
    ang = 2.0 * np.pi * (np.outer(t, t) % length) / length
    mh, ml = _hilo(np.concatenate([np.cos(ang), -np.sin(ang)], axis=0))
    consts = (mh, ml) + _channel_tables(length) + (gain,)
    full = lambda a: pl.BlockSpec(a.shape, lambda b: (0,) * a.ndim)
    blk = pl.BlockSpec((1, length, D_FOUR), lambda b: (b, 0, 0))
    return pl.pallas_call(
        _four_direct_body,
        grid=(bsz,),
        in_specs=[blk] + [full(a) for a in consts],
        out_specs=blk,
        out_shape=jax.ShapeDtypeStruct(f.shape, F32),
        compiler_params=_params("parallel"),
        name="fourier_direct",
    )(f, *consts)


def _four_stage1_body(f_ref, mh_ref, ml_ref, o_ref):
    o_ref[0] = _dot3c(mh_ref[...], ml_ref[...], f_ref[0])


def _four_stage2_body(zr_ref, zi_ref, mh_ref, ml_ref, c4h, c4l, s4h, s4l, gain_ref, o_ref):
    inner = zr_ref.shape[2]
    zz = jnp.concatenate([zr_ref[0, 0], zi_ref[0, 0]], axis=0)
    fc = _dot3c(mh_ref[0], ml_ref[0], zz)
    o_ref[0] = _four_finish(fc[:inner], fc[inner:], c4h[...], c4l[...], s4h[...], s4l[...], gain_ref[...])


def _fourier_long(f, gain):
    bsz, length, _ = f.shape
    l2 = FOUR_INNER
    l1 = length // l2
    assert l1 * l2 == length and l1 % 16 == 0
    cols = l2 * D_FOUR
    col_tile = 4096
    th = np.arange(l1)
    ang1 = 2.0 * np.pi * (np.outer(th, th) % l1) / l1
    m1h, m1l = _hilo(np.concatenate([np.cos(ang1), -np.sin(ang1)], axis=0))
    z = pl.pallas_call(
        _four_stage1_body,
        grid=(bsz, cols // col_tile),
        in_specs=[pl.BlockSpec((1, l1, col_tile), lambda b, c: (b, 0, c)),
                  pl.BlockSpec(m1h.shape, lambda b, c: (0, 0)),
                  pl.BlockSpec(m1l.shape, lambda b, c: (0, 0))],
        out_specs=pl.BlockSpec((1, 2 * l1, col_tile), lambda b, c: (b, 0, c)),
        out_shape=jax.ShapeDtypeStruct((bsz, 2 * l1, cols), F32),
        compiler_params=_params("parallel", "parallel"),
        name="fourier_stage1",
    )(f.reshape(bsz, l1, cols), m1h, m1l)
    z = z.reshape(bsz, 2 * l1, l2, D_FOUR)

    ma = np.arange(l1)[:, None, None]
    mb = np.arange(l2)[None, :, None]
    tl = np.arange(l2)[None, None, :]
    ang2 = 2.0 * np.pi * (((ma + l1 * mb) * tl) % length) / length
    cos2, sin2 = np.cos(ang2), np.sin(ang2)
    m2 = np.concatenate([np.concatenate([cos2, sin2], axis=2), np.concatenate([-sin2, cos2], axis=2)], axis=1)
    m2h, m2l = _hilo(m2)
    consts = _channel_tables(length) + (gain,)
    full = lambda a: pl.BlockSpec(a.shape, lambda b, m: (0,) * a.ndim)
    tab = pl.BlockSpec((1, 2 * l2, 2 * l2), lambda b, m: (m, 0, 0))
    out = pl.pallas_call(
        _four_stage2_body,
        grid=(bsz, l1),
        in_specs=[pl.BlockSpec((1, 1, l2, D_FOUR), lambda b, m: (b, m, 0, 0)),
                  pl.BlockSpec((1, 1, l2, D_FOUR), lambda b, m: (b, l1 + m, 0, 0)),
                  tab, tab] + [full(a) for a in consts],
        out_specs=pl.BlockSpec((1, l2, D_FOUR), lambda b, m: (b, 0, m)),
        out_shape=jax.ShapeDtypeStruct((bsz, l2, l1 * D_FOUR), F32),
        compiler_params=_params("parallel", "parallel"),
        name="fourier_stage2",
    )(z, z, m2h, m2l, *consts)
    return out.reshape(bsz, length, D_FOUR)


def _route(logits):
    lane = lax.broadcasted_iota(jnp.int32, (1, LANES), 1)
    lane_f = lane.astype(F32)
    neg = jnp.float32(-1e30)
    big = jnp.float32(1e9)
    gl = jnp.where(lane < N_GROUPS, logits, neg)
    gmax = jnp.max(gl, axis=-1, keepdims=True)
    pg_top = 1.0 / jnp.sum(jnp.exp(gl - gmax), axis=-1, keepdims=True)
    grp = jnp.min(jnp.where(gl == gmax, lane_f, big), axis=-1, keepdims=True)
    e_lane = lane - N_GROUPS
    in_grp = (e_lane >= 0) & (e_lane < N_EXPERTS) & ((e_lane >> 3).astype(F32) == grp)
    el = jnp.where(in_grp, logits, neg)
    m1 = jnp.max(el, axis=-1, keepdims=True)
    i1 = jnp.min(jnp.where(el == m1, lane_f, big), axis=-1, keepdims=True)
    el2 = jnp.where(lane_f == i1, neg, el)
    m2 = jnp.max(el2, axis=-1, keepdims=True)
    i2 = jnp.min(jnp.where(el2 == m2, lane_f, big), axis=-1, keepdims=True)
    e2 = jnp.exp(m2 - m1)
    den = 1.0 + e2
    return jnp.where(lane == 0, pg_top / den,
                     jnp.where(lane == 1, pg_top * e2 / den,
                               jnp.where(lane == 2, i1 - N_GROUPS, jnp.where(lane == 3, i2 - N_GROUPS, 0.0))))


def _post_body(n_lat_blocks, blocks_per_seq, y_ref, bon_ref, g_ref, cv_ref, cvp_ref, cvn_ref, fo_ref, x_ref,
               wo_ref, gnw_ref, gnb_ref, cw_ref, cg_ref, g1_ref, sc2_ref, sh2_ref, n2_ref, wrh_ref, wrl_ref,
               rb_ref, ones_ref, x_out, h_out, route_out):
    i = pl.program_id(0)
    ones_bd = ones_ref[...]
    y = y_ref[0] + y_ref[1]
    mu = _headsum(y, ones_bd) * (1.0 / HEAD)
    yc = y - mu
    var = _headsum(yc * yc, ones_bd) * (1.0 / HEAD)
    yn = yc * lax.rsqrt(var + GN_EPS) * gnw_ref[...] + gnb_ref[...]
    o_rwkv = (yn + bon_ref[...]) * g_ref[...]

    cv = cv_ref[...]
    tb = cv.shape[0]
    t = lax.broadcasted_iota(jnp.int32, (tb, 1), 0)
    is_ctx = i >= n_lat_blocks
    seq_pos = i % blocks_per_seq
    first = jnp.logical_or(is_ctx, seq_pos == 0).astype(F32)
    final = jnp.logical_or(is_ctx, seq_pos == blocks_per_seq - 1).astype(F32)
    zc = cv[:, D_CONV:2 * D_CONV] * cv[:, 2 * D_CONV:]
    zp_row = cvp_ref[7:8, D_CONV:2 * D_CONV] * cvp_ref[7:8, 2 * D_CONV:] * (1.0 - first)
    zn_row = cvn_ref[0:1, D_CONV:2 * D_CONV] * cvn_ref[0:1, 2 * D_CONV:] * (1.0 - final)
    prev = jnp.where(t == 0, zp_row, pltpu.roll(zc, 1, axis=0))
    nxt = jnp.where(t == tb - 1, zn_row, pltpu.roll(zc, tb - 1, axis=0))
    cw = cw_ref[...]
    conv = cv[:, :D_CONV] * (cw[0:1] * prev + cw[1:2] * zc + cw[2:3] * nxt)
    conv_o = _rms(conv) * cg_ref[...]

    mix = (_dot(o_rwkv.astype(BF16), wo_ref[0:D_RWKV, :])
           + _dot(conv_o.astype(BF16), wo_ref[D_RWKV:D_RWKV + D_CONV, :])
           + _dot(fo_ref[...].astype(BF16), wo_ref[D_RWKV + D_CONV:, :]))
    x = x_ref[...] + g1_ref[0] * mix
    x_out[...] = x
    h2 = _rms(x) * n2_ref[...] * (1.0 + sc2_ref[0]) + sh2_ref[0]
    h_out[...] = h2.astype(BF16)
    route_out[...] = _route(_dot3r(h2, wrh_ref[...], wrl_ref[...]) + rb_ref[...])


def _post(y, bonus, g, pcv, four_o, xa, seq, ctx_len, bsz, ent, wo_bf, gnw, gnb, cw, cg, g1, sc2, sh2, n2,
          wrh, wrl, rb, ones_bd):
    n = xa.shape[0]
    assert ctx_len == ROW_BLOCK
    n_lat_blocks = bsz * seq // ROW_BLOCK
    sub = ROW_BLOCK // 8
    last = n // 8 - 1
    full = lambda a: pl.BlockSpec(a.shape, lambda i: (0,) * a.ndim)
    row = lambda w: pl.BlockSpec((ROW_BLOCK, w), lambda i: (i, 0))
    mod = pl.BlockSpec((1, 1, D_MODEL), lambda i: (ent(i), 0, 0))
    return pl.pallas_call(
        functools.partial(_post_body, n_lat_blocks, seq // ROW_BLOCK),
        grid=(n // ROW_BLOCK,),
        in_specs=[pl.BlockSpec((2, ROW_BLOCK, D_RWKV), lambda i: (0, i, 0)), row(D_RWKV), row(D_RWKV),
                  row(3 * D_CONV),
                  pl.BlockSpec((8, 3 * D_CONV), lambda i: (jnp.maximum(i * sub - 1, 0), 0)),
                  pl.BlockSpec((8, 3 * D_CONV), lambda i: (jnp.minimum((i + 1) * sub, last), 0)),
                  row(D_FOUR), row(D_MODEL), full(wo_bf), full(gnw), full(gnb), full(cw), full(cg),
                  mod, mod, mod, full(n2), full(wrh), full(wrl), full(rb), full(ones_bd)],
        out_specs=[row(D_MODEL), row(D_MODEL), row(LANES)],
        out_shape=[jax.ShapeDtypeStruct((n, D_MODEL), F32), jax.ShapeDtypeStruct((n, D_MODEL), BF16),
                   jax.ShapeDtypeStruct((n, LANES), F32)],
        compiler_params=_params("parallel"),
        name="post_mix",
    )(y, bonus, g, pcv, pcv, pcv, four_o, xa, wo_bf, gnw, gnb, cw, cg, g1, sc2, sh2, n2, wrh, wrl, rb, ones_bd)


def _expert_body(be_ref, nu_ref, xb_ref, w_ref, wg_ref, wu_ref, wd_ref, o_ref, wg_s, wu_s, wd_s):
    i = pl.program_id(0)
    prev = be_ref[jnp.maximum(i - 1, 0)]
    fresh = jnp.logical_or(i == 0, be_ref[i] != prev)

    @pl.when(jnp.logical_and(fresh, i < nu_ref[0]))
    def _():
        wg_s[...] = wg_ref[0].astype(BF16)
        wu_s[...] = wu_ref[0].astype(BF16)
        wd_s[...] = wd_ref[0].astype(BF16)

    @pl.when(i < nu_ref[0])
    def _():
        xb = xb_ref[...]
        gate = _dot(xb, wg_s[...])
        up = _dot(xb, wu_s[...])
        act = gate * _sigmoid(gate) * up
        o_ref[...] = _dot(act.astype(BF16), wd_s[...]) * w_ref[...]

    @pl.when(i >= nu_ref[0])
    def _():
        o_ref[...] = jnp.zeros_like(o_ref)


def _experts(xb, buf_w, block_e, n_used, exp_gate, exp_up, exp_down):
    cap = xb.shape[0]
    nb = cap // MOE_ROWS
    grid_spec = pltpu.PrefetchScalarGridSpec(
        num_scalar_prefetch=2,
        grid=(nb,),
        in_specs=[pl.BlockSpec((MOE_ROWS, D_MODEL), lambda i, be, nu: (i, 0)),
                  pl.BlockSpec((MOE_ROWS, 1), lambda i, be, nu: (i, 0)),
                  pl.BlockSpec((1, D_MODEL, D_EXPERT), lambda i, be, nu: (be[i], 0, 0)),
                  pl.BlockSpec((1, D_MODEL, D_EXPERT), lambda i, be, nu: (be[i], 0, 0)),
                  pl.BlockSpec((1, D_EXPERT, D_MODEL), lambda i, be, nu: (be[i], 0, 0))],
        out_specs=pl.BlockSpec((MOE_ROWS, D_MODEL), lambda i, be, nu: (i, 0)),
        scratch_shapes=[pltpu.VMEM((D_MODEL, D_EXPERT), BF16), pltpu.VMEM((D_MODEL, D_EXPERT), BF16),
                        pltpu.VMEM((D_EXPERT, D_MODEL), BF16)],
    )
    return pl.pallas_call(
        _expert_body,
        grid_spec=grid_spec,
        out_shape=jax.ShapeDtypeStruct((cap, D_MODEL), F32),
        compiler_params=_params("arbitrary"),
        name="experts",
    )(block_e, n_used, xb, buf_w, exp_gate, exp_up, exp_down)


def _dispatch(route, n_tok):
    gates = route[:n_tok, 0:2]
    eid = route[:n_tok, 2:4].astype(jnp.int32)
    m = 2 * n_tok
    flat_e = eid.reshape(m)
    onehot = (flat_e[:, None] == jnp.arange(N_EXPERTS, dtype=jnp.int32)[None, :]).astype(jnp.int32)
    csum = jnp.cumsum(onehot, axis=0)
    rank = jnp.take_along_axis(csum, flat_e[:, None], axis=1)[:, 0] - 1
    counts = csum[-1]
    pcounts = (counts + MOE_ROWS - 1) // MOE_ROWS * MOE_ROWS
    pend = jnp.cumsum(pcounts)
    pstart = pend - pcounts
    dest = pstart[flat_e] + rank
    nb = -(-m // MOE_ROWS) + N_EXPERTS
    cap = nb * MOE_ROWS
    flat_t = jnp.arange(m, dtype=jnp.int32) // 2
    buf_t = jnp.full((cap,), n_tok, jnp.int32).at[dest].set(flat_t)
    buf_w = jnp.zeros((cap,), F32).at[dest].set(gates.reshape(m))
    block_e = jnp.minimum(jnp.searchsorted(pend, jnp.arange(nb, dtype=jnp.int32) * MOE_ROWS, side='right'),
                          N_EXPERTS - 1).astype(jnp.int32)
    n_used = (pend[-1] // MOE_ROWS).astype(jnp.int32).reshape(1)
    return buf_t, buf_w.reshape(cap, 1), block_e, n_used, dest.reshape(n_tok, 2)


def _combine_body(final_norm, x_ref, y_ref, g2_ref, fg_ref, o_ref):
    x = x_ref[...] + g2_ref[0] * (y_ref[:, :D_MODEL] + y_ref[:, D_MODEL:])
    if final_norm:
        x = _rms(x) * fg_ref[...]
    o_ref[...] = x


def _combine(xa, y2, g2, ent, final_g, final_norm):
    n = y2.shape[0]
    return pl.pallas_call(
        functools.partial(_combine_body, final_norm),
        grid=(n // ROW_BLOCK,),
        in_specs=[pl.BlockSpec((ROW_BLOCK, D_MODEL), lambda i: (i, 0)),
                  pl.BlockSpec((ROW_BLOCK, 2 * D_MODEL), lambda i: (i, 0)),
                  pl.BlockSpec((1, 1, D_MODEL), lambda i: (ent(i), 0, 0)),
                  pl.BlockSpec(final_g.shape, lambda i: (0, 0))],
        out_specs=pl.BlockSpec((ROW_BLOCK, D_MODEL), lambda i: (i, 0)),
        out_shape=jax.ShapeDtypeStruct((n, D_MODEL), F32),
        compiler_params=_params("parallel"),
        name="moe_combine",
    )(xa, y2, g2, final_g)


def _block_diag2(w):
    k, n = w.shape[1], w.shape[2]
    z = jnp.zeros((k, n), w.dtype)
    return jnp.concatenate([jnp.concatenate([w[0], z], axis=1), jnp.concatenate([z, w[1]], axis=1)], axis=0)


def kernel(x, c, ctx, c_ctx, ada_w, ada_b, norm1_g, norm2_g, w_in, mu_shift, decay_w0, decay_w2, iclr_a0, iclr_a2, gate_g2, k_k, k_a, r_k, gn_w, gn_b, conv_w, conv_gain, four_gain, w_out, router_g_w, router_g_b, router_e_w, router_e_b, exp_gate, exp_up, exp_down, final_g):
    bsz, seq, d = x.shape
    ctx_len = ctx.shape[1]
    depth = ada_w.shape[0]
    n_lat = bsz * seq
    n_ctx = bsz * ctx_len
    lat_blocks_per_batch = seq // ROW_BLOCK
    ent = lambda i: jnp.minimum(i // lat_blocks_per_batch, bsz)

    xa = jnp.concatenate([x.reshape(n_lat, d), ctx.reshape(n_ctx, d)], axis=0)
    cvec = jnp.concatenate([c, c_ctx[None, :]], axis=0)
    cvec = cvec * jax.nn.sigmoid(cvec)
    head_id = np.arange(D_RWKV) // HEAD
    ones_bd = jnp.asarray(head_id[:, None] == head_id[None, :], BF16)
    row2 = lambda a: a.reshape(1, -1)

    for l in range(depth):
        last = l == depth - 1
        mod = jnp.dot(cvec, ada_w[l], precision=lax.Precision.HIGHEST) + ada_b[l]
        sh1, sc1, g1, sh2, sc2, g2 = [mod[:, j * d:(j + 1) * d].reshape(bsz + 1, 1, d) for j in range(6)]

        pz, pcv, pfo = _inproj(xa, row2(norm1_g[l]), sc1, sh1, w_in[l].astype(BF16), ent)
        r, v, kk, g, bonus, lw, kd, b = _streams(
            pz, seq, ctx_len, bsz, row2(mu_shift[l]), _block_diag2(decay_w2[l]), _block_diag2(iclr_a2[l]),
            gate_g2[l], row2(decay_w0[l]), row2(iclr_a0[l]), row2(k_k[l]), row2(k_a[l]), row2(r_k[l]), ones_bd)
        pm, qm, rp, y0 = _chunk_local(r, v, kk, lw, kd, b)
        y = _chunk_scan(pm, qm, rp, y0, seq, ctx_len, bsz)

        fgain = row2(four_gain[l])
        four_x = _fourier_long(pfo[:n_lat].reshape(bsz, seq, D_FOUR), fgain).reshape(n_lat, D_FOUR)
        four_c = _fourier_direct(pfo[n_lat:].reshape(bsz, ctx_len, D_FOUR), fgain).reshape(n_ctx, D_FOUR)
        four_o = jnp.concatenate([four_x, four_c], axis=0)

        wr = jnp.zeros((d, LANES), F32).at[:, :N_GROUPS].set(router_g_w[l])
        wr = wr.at[:, N_GROUPS:N_GROUPS + N_EXPERTS].set(router_e_w[l])
        rb = jnp.zeros((1, LANES), F32).at[0, :N_GROUPS].set(router_g_b[l])
        rb = rb.at[0, N_GROUPS:N_GROUPS + N_EXPERTS].set(router_e_b[l])
        wrh, wrl = _hilo(wr)
        xa, h2, route = _post(y, bonus, g, pcv, four_o, xa, seq, ctx_len, bsz, ent, w_out[l].astype(BF16),
                              row2(gn_w[l]), row2(gn_b[l]), conv_w[l], row2(conv_gain[l]), g1, sc2, sh2,
                              row2(norm2_g[l]), wrh, wrl, rb, ones_bd)

        n_tok = n_lat if last else n_lat + n_ctx
        buf_t, buf_w, block_e, n_used, dest = _dispatch(route, n_tok)
        h_pad = jnp.concatenate([h2[:n_tok], jnp.zeros((1, d), BF16)], axis=0)
        yb = _experts(h_pad[buf_t], buf_w, block_e, n_used, exp_gate[l], exp_up[l], exp_down[l])
        xa = _combine(xa, yb[dest].reshape(n_tok, 2 * d), g2, ent, row2(final_g), last)

    return xa[:n_lat].reshape(bsz, seq, d)
```

```python
import functools

import numpy as np
import jax
import jax.numpy as jnp
from jax import lax
from jax.experimental import pallas as pl
from jax.experimental.pallas import tpu as pltpu

F32 = jnp.float32
BF16 = jnp.bfloat16

D_MODEL = 1024
HEAD = 64
D_RWKV = 512
H_RWKV = D_RWKV // HEAD
D_CONV = 256
D_FOUR = 256
FOUR_GROUP = 64
D_Z = 3 * D_RWKV + 2 * 64 + 2 * 64 + 128
D_IN = D_Z + 3 * D_CONV + D_FOUR
GRID_W = 64
N_GROUPS = 4
EXPERTS_PER_GROUP = 8
N_EXPERTS = N_GROUPS * EXPERTS_PER_GROUP
D_EXPERT = D_MODEL // 2
RMS_EPS = 1e-6
GN_EPS = 64e-5

CHUNK = 64
ROW_BLOCK = 256
MOE_ROWS = 256
FOUR_INNER = 128
LANES = 128
VMEM_LIMIT = 48 * 1024 * 1024

NN = (((1,), (0,)), ((), ()))
NT = (((1,), (1,)), ((), ()))
TN = (((0,), (0,)), ((), ()))


def _params(*sem):
    return pltpu.CompilerParams(dimension_semantics=sem, vmem_limit_bytes=VMEM_LIMIT)


def _split2(x):
    hi = x.astype(BF16)
    lo = (x - hi.astype(F32)).astype(BF16)
    return hi, lo


def _dot(a, b, dims=NN):
    return lax.dot_general(a, b, dims, preferred_element_type=F32)


def _dot1(a, b, dims=NN):
    return _dot(a.astype(BF16), b.astype(BF16), dims)


def _dot3(a, b, dims=NN):
    ah, al = _split2(a)
    bh, bl = _split2(b)
    return _dot(ah, bh, dims) + (_dot(ah, bl, dims) + _dot(al, bh, dims))


def _dot3c(ch, cl, x, dims=NN):
    xh, xl = _split2(x)
    return _dot(ch, xh, dims) + (_dot(cl, xh, dims) + _dot(ch, xl, dims))


def _dot3r(x, ch, cl, dims=NN):
    xh, xl = _split2(x)
    return _dot(xh, ch, dims) + (_dot(xl, ch, dims) + _dot(xh, cl, dims))


def _headsum(x, ones_bd):
    hi, lo = _split2(x)
    return _dot(hi, ones_bd) + _dot(lo, ones_bd)


def _rms(x, eps=RMS_EPS):
    return x * lax.rsqrt(jnp.mean(x * x, axis=-1, keepdims=True) + eps)


def _sigmoid(x):
    return 1.0 / (1.0 + jnp.exp(-x))


def _softplus(x):
    return jnp.maximum(x, 0.0) + jnp.log(1.0 + jnp.exp(-jnp.abs(x)))


def _inproj_body(x_ref, g_ref, sc_ref, sh_ref, w_ref, z_ref, cv_ref, fo_ref):
    h = _rms(x_ref[...]) * g_ref[...] * (1.0 + sc_ref[0]) + sh_ref[0]
    p = _dot(h.astype(BF16), w_ref[...])
    z_ref[...] = p[:, :D_Z]
    cv_ref[...] = p[:, D_Z:D_Z + 3 * D_CONV]
    fo_ref[...] = p[:, D_Z + 3 * D_CONV:]


def _inproj(xa, gain, sc, sh, w_bf, ent):
    n = xa.shape[0]
    row = lambda w: pl.BlockSpec((ROW_BLOCK, w), lambda i: (i, 0))
    full = lambda a: pl.BlockSpec(a.shape, lambda i: (0,) * a.ndim)
    mod = pl.BlockSpec((1, 1, D_MODEL), lambda i: (ent(i), 0, 0))
    return pl.pallas_call(
        _inproj_body,
        grid=(n // ROW_BLOCK,),
        in_specs=[row(D_MODEL), full(gain), mod, mod, full(w_bf)],
        out_specs=[row(D_Z), row(3 * D_CONV), row(D_FOUR)],
        out_shape=[jax.ShapeDtypeStruct((n, D_Z), F32), jax.ShapeDtypeStruct((n, 3 * D_CONV), F32),
                   jax.ShapeDtypeStruct((n, D_FOUR), F32)],
        compiler_params=_params("parallel"),
        name="inproj",
    )(xa, gain, sc, sh, w_bf)


def _streams_body(n_lat_blocks, blocks_per_seq, zm_ref, zp_ref, zn_ref, mu_ref, w2_ref, a2_ref, g2_ref,
                  w0_ref, a0_ref, kkw_ref, ka_ref, rk_ref, ones_ref,
                  r_out, v_out, kk_out, g_out, bon_out, lw_out, kd_out, b_out):
    i = pl.program_id(0)
    ctx_i = (i >= n_lat_blocks).astype(jnp.int32)
    z = zm_ref[...]
    tb = z.shape[0]
    t = lax.broadcasted_iota(jnp.int32, (tb, 1), 0)
    c = lax.broadcasted_iota(jnp.int32, (1, D_Z), 1)
    ctx_v = jnp.zeros((tb, 1), jnp.int32) + ctx_i
    col = t & (GRID_W - 1)
    lmask = (col != 0) | ((ctx_v != 0) & (t != 0))
    rmask = (col != GRID_W - 1) | ((ctx_v != 0) & (t != tb - 1))
    seq_pos = i % blocks_per_seq
    top_v = jnp.zeros((tb, 1), jnp.int32) + (seq_pos == 0).astype(jnp.int32)
    bot_v = jnp.zeros((tb, 1), jnp.int32) + (seq_pos == blocks_per_seq - 1).astype(jnp.int32)
    umask = jnp.logical_not((top_v != 0) & (t < GRID_W))
    dmask = jnp.logical_not((bot_v != 0) & (t >= tb - GRID_W))
    left = jnp.where(lmask, pltpu.roll(z, 1, axis=0), 0.0)
    right = jnp.where(rmask, pltpu.roll(z, tb - 1, axis=0), 0.0)
    up = jnp.where(umask, jnp.concatenate([zp_ref[...], z[:tb - GRID_W]], axis=0), 0.0)
    down = jnp.where(dmask, jnp.concatenate([z[GRID_W:], zn_ref[...]], axis=0), 0.0)
    q = c & jnp.where(ctx_i != 0, 1, 3)
    shifted = jnp.where(q == 0, left, jnp.where(q == 1, right, jnp.where(q == 2, up, down)))
    z = z + (shifted - z) * mu_ref[...]

    r = z[:, 0:D_RWKV]
    k = z[:, D_RWKV:2 * D_RWKV]
    v = z[:, 2 * D_RWKV:3 * D_RWKV]
    lw_in = z[:, 3 * D_RWKV:3 * D_RWKV + 128]
    la_in = z[:, 3 * D_RWKV + 128:3 * D_RWKV + 256]
    lg = z[:, 3 * D_RWKV + 256:]
    ones_bd = ones_ref[...]

    g_out[...] = _dot3(_sigmoid(lg), g2_ref[...])
    kq = k * kkw_ref[...]
    kk = kq / jnp.maximum(jnp.sqrt(_headsum(kq * kq, ones_bd)), 1e-12)
    w_log = -_softplus(-(w0_ref[...] + _dot3(jnp.tanh(lw_in), w2_ref[...]))) - 0.5
    lw = -jnp.exp(w_log)
    a = _sigmoid(a0_ref[...] + _dot3(la_in, a2_ref[...]))
    ka = ka_ref[...]
    ksum = jnp.zeros_like(k)
    for d in range(2):
        a_d = a[:, d * D_RWKV:(d + 1) * D_RWKV]
        k_d = k * (1.0 + (a_d - 1.0) * ka)
        ksum = ksum + k_d
        lw_out[d] = lw[:, d * D_RWKV:(d + 1) * D_RWKV]
        kd_out[d] = k_d
        b_out[d] = kk * a_d
    r_out[...] = r
    v_out[...] = v
    kk_out[...] = kk
    bon_out[...] = _headsum(r * ksum * rk_ref[...], ones_bd) * v


def _streams(pz, seq, ctx_len, bsz, mu, w2bd, a2bd, g2, w0, a0, kkw, ka, rk, ones_bd):
    n = pz.shape[0]
    assert ctx_len == ROW_BLOCK and seq % ROW_BLOCK == 0
    n_lat_blocks = bsz * seq // ROW_BLOCK
    sub = ROW_BLOCK // GRID_W
    last = n // GRID_W - 1
    full = lambda a: pl.BlockSpec(a.shape, lambda i: (0,) * a.ndim)
    row = pl.BlockSpec((ROW_BLOCK, D_RWKV), lambda i: (i, 0))
    row2 = pl.BlockSpec((2, ROW_BLOCK, D_RWKV), lambda i: (0, i, 0))
    s1 = jax.ShapeDtypeStruct((n, D_RWKV), F32)
    s2 = jax.ShapeDtypeStruct((2, n, D_RWKV), F32)
    consts = (mu, w2bd, a2bd, g2, w0, a0, kkw, ka, rk, ones_bd)
    return pl.pallas_call(
        functools.partial(_streams_body, n_lat_blocks, seq // ROW_BLOCK),
        grid=(n // ROW_BLOCK,),
        in_specs=[pl.BlockSpec((ROW_BLOCK, D_Z), lambda i: (i, 0)),
                  pl.BlockSpec((GRID_W, D_Z), lambda i: (jnp.maximum(i * sub - 1, 0), 0)),
                  pl.BlockSpec((GRID_W, D_Z), lambda i: (jnp.minimum((i + 1) * sub, last), 0))]
                 + [full(a) for a in consts],
        out_specs=[row] * 5 + [row2] * 3,
        out_shape=[s1] * 5 + [s2] * 3,
        compiler_params=_params("parallel"),
        name="streams",
    )(pz, pz, pz, *consts)


def _chunk_body(r_ref, v_ref, kk_ref, lw_ref, kd_ref, b_ref, p_out, q_out, rp_out, y0_out):
    d = pl.program_id(0)
    cs = CHUNK
    rows = r_ref.shape[0]
    n_chunks = rows // cs
    n_pairs = D_RWKV // LANES
    sgn = jnp.where(d == 0, 1, -1)

    ri = lax.broadcasted_iota(jnp.int32, (rows, rows), 0)
    rj = lax.broadcasted_iota(jnp.int32, (rows, rows), 1)
    same_chunk = (ri >> 6) == (rj >> 6)
    t_incl = (same_chunk & ((ri - rj) * sgn >= 0)).astype(F32).astype(BF16)
    t_all = same_chunk.astype(F32).astype(BF16)
    lw = lw_ref[0]
    l1 = lw.astype(BF16)
    rem = lw - l1.astype(F32)
    l2 = rem.astype(BF16)
    l3 = (rem - l2.astype(F32)).astype(BF16)
    gcum = _dot(t_incl, l1) + (_dot(t_incl, l2) + _dot(t_incl, l3))
    gtot = _dot(t_all, l1) + (_dot(t_all, l2) + _dot(t_all, l3))
    e_pos = jnp.exp(gcum)
    e_neg = jnp.exp(-gcum)
    e_rem = jnp.exp(gtot - gcum)
    e_prev = jnp.exp(gcum - lw)
    e_tot = jnp.exp(gtot)

    r = r_ref[...]
    v = v_ref[...]
    kk = kk_ref[...]
    kd = kd_ref[0]
    b = b_ref[0]
    at = -(kk * e_prev)
    bt = b * e_neg
    kt = kd * e_neg
    rt = r * e_pos
    bh = b * e_rem
    kh = kd * e_rem

    ti = lax.broadcasted_iota(jnp.int32, (cs, LANES), 0)
    lane = lax.broadcasted_iota(jnp.int32, (cs, LANES), 1)
    tj = lane & (HEAD - 1)
    lo_half = lane < HEAD
    order = (ti - tj) * sgn
    strict = order > 0
    incl = order >= 0
    eye = ti == tj
    eye_f = eye.astype(F32)
    zero_bf = jnp.zeros((cs, LANES), BF16)

    def bd(y):
        y = y.astype(BF16)
        return jnp.concatenate([jnp.where(lo_half, y, zero_bf), jnp.where(lo_half, zero_bf, y)], axis=0)

    def mm(x, y_bd, dims=NN):
        return _dot(x.astype(BF16), y_bd, dims)

    chains = [(c, p) for c in range(n_chunks) for p in range(n_pairs)]
    tile = lambda a, c, p: a[c * cs:(c + 1) * cs, p * LANES:(p + 1) * LANES]

    s = [mm(jnp.concatenate([tile(at, c, p), tile(rt, c, p)], axis=0),
            jnp.concatenate([bd(tile(bt, c, p)), bd(tile(kt, c, p))], axis=0), NT) for c, p in chains]
    a_ab = [jnp.where(strict, x[:cs, :LANES], 0.0) for x in s]
    a_ak = [jnp.where(strict, x[:cs, LANES:], 0.0) for x in s]
    a_rbk = [jnp.concatenate([jnp.where(incl, x[cs:, :LANES], 0.0), jnp.where(incl, x[cs:, LANES:], 0.0)], axis=1)
             for x in s]
    a0 = [jnp.where((ti >> 3) == (tj >> 3), x, 0.0) for x in a_ab]
    a2 = [mm(x, bd(x)) for x in a0]
    a2_bd = [bd(x) for x in a2]
    x0 = [eye_f + x for x in a0]
    x1 = [x + mm(x, y) for x, y in zip(x0, a2_bd)]
    a4 = [mm(x, y) for x, y in zip(a2, a2_bd)]
    minv = [x + mm(x, bd(y)) for x, y in zip(x1, a4)]
    lvl = 3
    while (1 << lvl) < cs:
        off = ((ti >> (lvl + 1)) == (tj >> (lvl + 1))) & ((ti >> lvl) != (tj >> lvl))
        t = [mm(jnp.where(off, x, 0.0), bd(y)) for x, y in zip(a_ab, minv)]
        minv = [x + mm(x, bd(y)) for x, y in zip(minv, t)]
        lvl += 1
    v_bd = [bd(tile(v, c, p)) for c, p in chains]
    av = [mm(x, y) for x, y in zip(a_ak, v_bd)]
    wu = [mm(m, jnp.concatenate([bd(tile(at, c, p)), bd(y)], axis=1)) for m, y, (c, p) in zip(minv, av, chains)]
    top = [mm(jnp.concatenate([tile(bh, c, p), tile(kh, c, p)], axis=0),
              jnp.concatenate([x.astype(BF16), jnp.concatenate([zero_bf, tile(v, c, p).astype(BF16)], axis=1)],
                              axis=0), TN) for x, (c, p) in zip(wu, chains)]
    bot = [mm(x, jnp.concatenate([jnp.concatenate([bd(y[:, :LANES]), bd(y[:, LANES:])], axis=1),
                                  jnp.concatenate([jnp.zeros((2 * cs, LANES), BF16), vb], axis=1)], axis=0))
           for x, y, vb in zip(a_rbk, wu, v_bd)]
    for c in range(n_chunks):
        sel = lambda x, off: jnp.where(lo_half, x[:cs, off:off + LANES], x[cs:, off:off + LANES])
        idx = [c * n_pairs + p for p in range(n_pairs)]
        p_out[0, c] = jnp.concatenate(
            [sel(top[i], 0) + jnp.where(eye, tile(e_tot, c, p), 0.0) for p, i in enumerate(idx)], axis=1)
        q_out[0, c] = jnp.concatenate([sel(top[i], LANES) for i in idx], axis=1)
        rp_out[0, c * cs:(c + 1) * cs, :] = jnp.concatenate(
            [tile(rt, c, p) + bot[i][:, :LANES] for p, i in enumerate(idx)], axis=1)
        y0_out[0, c * cs:(c + 1) * cs, :] = jnp.concatenate([bot[i][:, LANES:] for i in idx], axis=1)


CHUNKS_PER_STEP = 2


def _chunk_local(r, v, kk, lw, kd, b):
    n = r.shape[0]
    nc = n // CHUNK
    rows = CHUNKS_PER_STEP * CHUNK
    shared = pl.BlockSpec((rows, D_RWKV), lambda d, j: (j, 0))
    per_dir = pl.BlockSpec((1, rows, D_RWKV), lambda d, j: (d, j, 0))
    mat = pl.BlockSpec((1, CHUNKS_PER_STEP, HEAD, D_RWKV), lambda d, j: (d, j, 0, 0))
    sm = jax.ShapeDtypeStruct((2, nc, HEAD, D_RWKV), F32)
    st = jax.ShapeDtypeStruct((2, n, D_RWKV), F32)
    return pl.pallas_call(
        _chunk_body,
        grid=(2, nc // CHUNKS_PER_STEP),
        in_specs=[shared, shared, shared, per_dir, per_dir, per_dir],
        out_specs=[mat, mat, per_dir, per_dir],
        out_shape=[sm, sm, st, st],
        compiler_params=_params("parallel", "parallel"),
        name="chunk_local",
    )(r, v, kk, lw, kd, b)


def _scan_body(p_ref, q_ref, rp_ref, y0_ref, y_out, h_scr):
    @pl.when(pl.program_id(2) == 0)
    def _():
        h_scr[...] = jnp.zeros_like(h_scr)

    hst = h_scr[...]
    p = p_ref[0, 0]
    q = q_ref[0, 0]
    rp = rp_ref[0]
    y0 = y0_ref[0]
    h_parts, y_parts = [], []
    for h in range(H_RWKV):
        sl = slice(h * HEAD, (h + 1) * HEAD)
        o = _dot3(jnp.concatenate([p[:, sl], rp[:, sl]], axis=0), hst[:, sl])
        h_parts.append(o[:HEAD] + q[:, sl])
        y_parts.append(y0[:, sl] + o[HEAD:])
    h_scr[...] = jnp.concatenate(h_parts, axis=1)
    y_out[0] = jnp.concatenate(y_parts, axis=1)


def _chunk_scan(p, q, rp, y0, seq, ctx_len, bsz):
    n = rp.shape[1]
    xpc, cpc = seq // CHUNK, ctx_len // CHUNK

    def chunk(d, b, i):
        in_ctx = i < cpc
        fwd = jnp.where(in_ctx, bsz * xpc + b * cpc + i, b * xpc + i - cpc)
        bwd = jnp.where(in_ctx, bsz * xpc + b * cpc + cpc - 1 - i, b * xpc + xpc - 1 - (i - cpc))
        return jnp.where(d == 0, fwd, bwd)

    mat = pl.BlockSpec((1, 1, HEAD, D_RWKV), lambda d, b, i: (d, chunk(d, b, i), 0, 0))
    tok = pl.BlockSpec((1, CHUNK, D_RWKV), lambda d, b, i: (d, chunk(d, b, i), 0))
    return pl.pallas_call(
        _scan_body,
        grid=(2, bsz, xpc + cpc),
        in_specs=[mat, mat, tok, tok],
        out_specs=tok,
        out_shape=jax.ShapeDtypeStruct((2, n, D_RWKV), F32),
        scratch_shapes=[pltpu.VMEM((HEAD, D_RWKV), F32)],
        compiler_params=_params("parallel", "parallel", "arbitrary"),
        name="chunk_scan",
    )(p, q, rp, y0)


def _hilo(a):
    if isinstance(a, np.ndarray):
        a = a.astype(np.float32)
        hi = a.astype(BF16)
        return jnp.asarray(hi), jnp.asarray((a - hi.astype(np.float32)).astype(BF16))
    hi = a.astype(BF16)
    return hi, (a - hi.astype(F32)).astype(BF16)


def _channel_tables(length):
    j = np.arange(FOUR_GROUP)
    ang = 2.0 * np.pi * np.outer(j, j) / FOUR_GROUP
    scale = 1.0 / np.sqrt(float(length) * FOUR_GROUP)
    groups = D_FOUR // FOUR_GROUP
    c4 = np.kron(np.eye(groups), np.cos(ang)) * scale
    s4 = np.kron(np.eye(groups), np.sin(ang)) * scale
    return _hilo(c4) + _hilo(s4)


def _four_finish(fr, fi, c4h, c4l, s4h, s4l, gain):
    y = _dot3r(fr, c4h, c4l) + _dot3r(fi, s4h, s4l)
    return _rms(y) * gain


def _four_direct_body(f_ref, mh_ref, ml_ref, c4h, c4l, s4h, s4l, gain_ref, o_ref):
    length = f_ref.shape[1]
    fc = _dot3c(mh_ref[...], ml_ref[...], f_ref[0])
    o_ref[0] = _four_finish(fc[:length], fc[length:], c4h[...], c4l[...], s4h[...], s4l[...], gain_ref[...])


def _fourier_direct(f, gain):
    bsz, length, _ = f.shape
    t = np.arange(length)
    ang = 2.0 * np.pi * (np.outer(t, t) % length) / length
    mh, ml = _hilo(np.concatenate([np.cos(ang), -np.sin(ang)], axis=0))
    consts = (mh, ml) + _channel_tables(length) + (gain,)
    full = lambda a: pl.BlockSpec(a.shape, lambda b: (0,) * a.ndim)
    blk = pl.BlockSpec((1, length, D_FOUR), lambda b: (b, 0, 0))
    return pl.pallas_call(
        _four_direct_body,
        grid=(bsz,),
        in_specs=[blk] + [full(a) for a in consts],
        out_specs=blk,
        out_shape=jax.ShapeDtypeStruct(f.shape, F32),
        compiler_params=_params("parallel"),
        name="fourier_direct",
    )(f, *consts)


def _four_stage1_body(f_ref, mh_ref, ml_ref, o_ref):
    o_ref[0] = _dot3c(mh_ref[...], ml_ref[...], f_ref[0])


def _four_stage2_body(zr_ref, zi_ref, mh_ref, ml_ref, c4h, c4l, s4h, s4l, gain_ref, o_ref):
    inner = zr_ref.shape[2]
    zz = jnp.concatenate([zr_ref[0, 0], zi_ref[0, 0]], axis=0)
    fc = _dot3c(mh_ref[0], ml_ref[0], zz)
    o_ref[0] = _four_finish(fc[:inner], fc[inner:], c4h[...], c4l[...], s4h[...], s4l[...], gain_ref[...])


def _fourier_long(f, gain):
    bsz, length, _ = f.shape
    l2 = FOUR_INNER
    l1 = length // l2
    assert l1 * l2 == length and l1 % 16 == 0
    cols = l2 * D_FOUR
    col_tile = 4096
    th = np.arange(l1)
    ang1 = 2.0 * np.pi * (np.outer(th, th) % l1) / l1
    m1h, m1l = _hilo(np.concatenate([np.cos(ang1), -np.sin(ang1)], axis=0))
    z = pl.pallas_call(
        _four_stage1_body,
        grid=(bsz, cols // col_tile),
        in_specs=[pl.BlockSpec((1, l1, col_tile), lambda b, c: (b, 0, c)),
                  pl.BlockSpec(m1h.shape, lambda b, c: (0, 0)),
                  pl.BlockSpec(m1l.shape, lambda b, c: (0, 0))],
        out_specs=pl.BlockSpec((1, 2 * l1, col_tile), lambda b, c: (b, 0, c)),
        out_shape=jax.ShapeDtypeStruct((bsz, 2 * l1, cols), F32),
        compiler_params=_params("parallel", "parallel"),
        name="fourier_stage1",
    )(f.reshape(bsz, l1, cols), m1h, m1l)
    z = z.reshape(bsz, 2 * l1, l2, D_FOUR)

    ma = np.arange(l1)[:, None, None]
    mb = np.arange(l2)[None, :, None]
    tl = np.arange(l2)[None, None, :]
    ang2 = 2.0 * np.pi * (((ma + l1 * mb) * tl) % length) / length
    cos2, sin2 = np.cos(ang2), np.sin(ang2)
    m2 = np.concatenate([np.concatenate([cos2, sin2], axis=2), np.concatenate([-sin2, cos2], axis=2)], axis=1)
    m2h, m2l = _hilo(m2)
    consts = _channel_tables(length) + (gain,)
    full = lambda a: pl.BlockSpec(a.shape, lambda b, m: (0,) * a.ndim)
    tab = pl.BlockSpec((1, 2 * l2, 2 * l2), lambda b, m: (m, 0, 0))
    out = pl.pallas_call(
        _four_stage2_body,
        grid=(bsz, l1),
        in_specs=[pl.BlockSpec((1, 1, l2, D_FOUR), lambda b, m: (b, m, 0, 0)),
                  pl.BlockSpec((1, 1, l2, D_FOUR), lambda b, m: (b, l1 + m, 0, 0)),
                  tab, tab] + [full(a) for a in consts],
        out_specs=pl.BlockSpec((1, l2, D_FOUR), lambda b, m: (b, 0, m)),
        out_shape=jax.ShapeDtypeStruct((bsz, l2, l1 * D_FOUR), F32),
        compiler_params=_params("parallel", "parallel"),
        name="fourier_stage2",
    )(z, z, m2h, m2l, *consts)
    return out.reshape(bsz, length, D_FOUR)


def _route(logits):
    lane = lax.broadcasted_iota(jnp.int32, (1, LANES), 1)
    lane_f = lane.astype(F32)
    neg = jnp.float32(-1e30)
    big = jnp.float32(1e9)
    gl = jnp.where(lane < N_GROUPS, logits, neg)
    gmax = jnp.max(gl, axis=-1, keepdims=True)
    pg_top = 1.0 / jnp.sum(jnp.exp(gl - gmax), axis=-1, keepdims=True)
    grp = jnp.min(jnp.where(gl == gmax, lane_f, big), axis=-1, keepdims=True)
    e_lane = lane - N_GROUPS
    in_grp = (e_lane >= 0) & (e_lane < N_EXPERTS) & ((e_lane >> 3).astype(F32) == grp)
    el = jnp.where(in_grp, logits, neg)
    m1 = jnp.max(el, axis=-1, keepdims=True)
    i1 = jnp.min(jnp.where(el == m1, lane_f, big), axis=-1, keepdims=True)
    el2 = jnp.where(lane_f == i1, neg, el)
    m2 = jnp.max(el2, axis=-1, keepdims=True)
    i2 = jnp.min(jnp.where(el2 == m2, lane_f, big), axis=-1, keepdims=True)
    e2 = jnp.exp(m2 - m1)
    den = 1.0 + e2
    return jnp.where(lane == 0, pg_top / den,
                     jnp.where(lane == 1, pg_top * e2 / den,
                               jnp.where(lane == 2, i1 - N_GROUPS, jnp.where(lane == 3, i2 - N_GROUPS, 0.0))))


def _post_body(n_lat_blocks, blocks_per_seq, y_ref, bon_ref, g_ref, cv_ref, cvp_ref, cvn_ref, fo_ref, x_ref,
               wo_ref, gnw_ref, gnb_ref, cw_ref, cg_ref, g1_ref, sc2_ref, sh2_ref, n2_ref, wrh_ref, wrl_ref,
               rb_ref, ones_ref, x_out, h_out, route_out):
    i = pl.program_id(0)
    ones_bd = ones_ref[...]
    y = y_ref[0] + y_ref[1]
    mu = _headsum(y, ones_bd) * (1.0 / HEAD)
    yc = y - mu
    var = _headsum(yc * yc, ones_bd) * (1.0 / HEAD)
    yn = yc * lax.rsqrt(var + GN_EPS) * gnw_ref[...] + gnb_ref[...]
    o_rwkv = (yn + bon_ref[...]) * g_ref[...]

    cv = cv_ref[...]
    tb = cv.shape[0]
    t = lax.broadcasted_iota(jnp.int32, (tb, 1), 0)
    is_ctx = i >= n_lat_blocks
    seq_pos = i % blocks_per_seq
    first = jnp.logical_or(is_ctx, seq_pos == 0).astype(F32)
    final = jnp.logical_or(is_ctx, seq_pos == blocks_per_seq - 1).astype(F32)
    zc = cv[:, D_CONV:2 * D_CONV] * cv[:, 2 * D_CONV:]
    zp_row = cvp_ref[7:8, D_CONV:2 * D_CONV] * cvp_ref[7:8, 2 * D_CONV:] * (1.0 - first)
    zn_row = cvn_ref[0:1, D_CONV:2 * D_CONV] * cvn_ref[0:1, 2 * D_CONV:] * (1.0 - final)
    prev = jnp.where(t == 0, zp_row, pltpu.roll(zc, 1, axis=0))
    nxt = jnp.where(t == tb - 1, zn_row, pltpu.roll(zc, tb - 1, axis=0))
    cw = cw_ref[...]
    conv = cv[:, :D_CONV] * (cw[0:1] * prev + cw[1:2] * zc + cw[2:3] * nxt)
    conv_o = _rms(conv) * cg_ref[...]

    mix = (_dot(o_rwkv.astype(BF16), wo_ref[0:D_RWKV, :])
           + _dot(conv_o.astype(BF16), wo_ref[D_RWKV:D_RWKV + D_CONV, :])
           + _dot(fo_ref[...].astype(BF16), wo_ref[D_RWKV + D_CONV:, :]))
    x = x_ref[...] + g1_ref[0] * mix
    x_out[...] = x
    h2 = _rms(x) * n2_ref[...] * (1.0 + sc2_ref[0]) + sh2_ref[0]
    h_out[...] = h2.astype(BF16)
    route_out[...] = _route(_dot3r(h2, wrh_ref[...], wrl_ref[...]) + rb_ref[...])


def _post(y, bonus, g, pcv, four_o, xa, seq, ctx_len, bsz, ent, wo_bf, gnw, gnb, cw, cg, g1, sc2, sh2, n2,
          wrh, wrl, rb, ones_bd):
    n = xa.shape[0]
    assert ctx_len == ROW_BLOCK
    n_lat_blocks = bsz * seq // ROW_BLOCK
    sub = ROW_BLOCK // 8
    last = n // 8 - 1
    full = lambda a: pl.BlockSpec(a.shape, lambda i: (0,) * a.ndim)
    row = lambda w: pl.BlockSpec((ROW_BLOCK, w), lambda i: (i, 0))
    mod = pl.BlockSpec((1, 1, D_MODEL), lambda i: (ent(i), 0, 0))
    return pl.pallas_call(
        functools.partial(_post_body, n_lat_blocks, seq // ROW_BLOCK),
        grid=(n // ROW_BLOCK,),
        in_specs=[pl.BlockSpec((2, ROW_BLOCK, D_RWKV), lambda i: (0, i, 0)), row(D_RWKV), row(D_RWKV),
                  row(3 * D_CONV),
                  pl.BlockSpec((8, 3 * D_CONV), lambda i: (jnp.maximum(i * sub - 1, 0), 0)),
                  pl.BlockSpec((8, 3 * D_CONV), lambda i: (jnp.minimum((i + 1) * sub, last), 0)),
                  row(D_FOUR), row(D_MODEL), full(wo_bf), full(gnw), full(gnb), full(cw), full(cg),
                  mod, mod, mod, full(n2), full(wrh), full(wrl), full(rb), full(ones_bd)],
        out_specs=[row(D_MODEL), row(D_MODEL), row(LANES)],
        out_shape=[jax.ShapeDtypeStruct((n, D_MODEL), F32), jax.ShapeDtypeStruct((n, D_MODEL), BF16),
                   jax.ShapeDtypeStruct((n, LANES), F32)],
        compiler_params=_params("parallel"),
        name="post_mix",
    )(y, bonus, g, pcv, pcv, pcv, four_o, xa, wo_bf, gnw, gnb, cw, cg, g1, sc2, sh2, n2, wrh, wrl, rb, ones_bd)


def _expert_body(be_ref, nu_ref, xb_ref, w_ref, wg_ref, wu_ref, wd_ref, o_ref, wg_s, wu_s, wd_s):
    i = pl.program_id(0)
    prev = be_ref[jnp.maximum(i - 1, 0)]
    fresh = jnp.logical_or(i == 0, be_ref[i] != prev)

    @pl.when(jnp.logical_and(fresh, i < nu_ref[0]))
    def _():
        wg_s[...] = wg_ref[0].astype(BF16)
        wu_s[...] = wu_ref[0].astype(BF16)
        wd_s[...] = wd_ref[0].astype(BF16)

    @pl.when(i < nu_ref[0])
    def _():
        xb = xb_ref[...]
        gate = _dot(xb, wg_s[...])
        up = _dot(xb, wu_s[...])
        act = gate * _sigmoid(gate) * up
        o_ref[...] = _dot(act.astype(BF16), wd_s[...]) * w_ref[...]

    @pl.when(i >= nu_ref[0])
    def _():
        o_ref[...] = jnp.zeros_like(o_ref)


def _experts(xb, buf_w, block_e, n_used, exp_gate, exp_up, exp_down):
    cap = xb.shape[0]
    nb = cap // MOE_ROWS
    grid_spec = pltpu.PrefetchScalarGridSpec(
        num_scalar_prefetch=2,
        grid=(nb,),
        in_specs=[pl.BlockSpec((MOE_ROWS, D_MODEL), lambda i, be, nu: (i, 0)),
                  pl.BlockSpec((MOE_ROWS, 1), lambda i, be, nu: (i, 0)),
                  pl.BlockSpec((1, D_MODEL, D_EXPERT), lambda i, be, nu: (be[i], 0, 0)),
                  pl.BlockSpec((1, D_MODEL, D_EXPERT), lambda i, be, nu: (be[i], 0, 0)),
                  pl.BlockSpec((1, D_EXPERT, D_MODEL), lambda i, be, nu: (be[i], 0, 0))],
        out_specs=pl.BlockSpec((MOE_ROWS, D_MODEL), lambda i, be, nu: (i, 0)),
        scratch_shapes=[pltpu.VMEM((D_MODEL, D_EXPERT), BF16), pltpu.VMEM((D_MODEL, D_EXPERT), BF16),
                        pltpu.VMEM((D_EXPERT, D_MODEL), BF16)],
    )
    return pl.pallas_call(
        _expert_body,
        grid_spec=grid_spec,
        out_shape=jax.ShapeDtypeStruct((cap, D_MODEL), F32),
        compiler_params=_params("arbitrary"),
        name="experts",
    )(block_e, n_used, xb, buf_w, exp_gate, exp_up, exp_down)


def _dispatch(route, n_tok):
    gates = route[:n_tok, 0:2]
    eid = route[:n_tok, 2:4].astype(jnp.int32)
    m = 2 * n_tok
    flat_e = eid.reshape(m)
    onehot = (flat_e[:, None] == jnp.arange(N_EXPERTS, dtype=jnp.int32)[None, :]).astype(jnp.int32)
    csum = jnp.cumsum(onehot, axis=0)
    rank = jnp.take_along_axis(csum, flat_e[:, None], axis=1)[:, 0] - 1
    counts = csum[-1]
    pcounts = (counts + MOE_ROWS - 1) // MOE_ROWS * MOE_ROWS
    pend = jnp.cumsum(pcounts)
    pstart = pend - pcounts
    dest = pstart[flat_e] + rank
    nb = -(-m // MOE_ROWS) + N_EXPERTS
    cap = nb * MOE_ROWS
    flat_t = jnp.arange(m, dtype=jnp.int32) // 2
    buf_t = jnp.full((cap,), n_tok, jnp.int32).at[dest].set(flat_t)
    buf_w = jnp.zeros((cap,), F32).at[dest].set(gates.reshape(m))
    block_start = jnp.arange(nb, dtype=jnp.int32) * MOE_ROWS
    block_e = jnp.minimum(jnp.sum((pend[None, :] <= block_start[:, None]).astype(jnp.int32), axis=1),
                          N_EXPERTS - 1)
    n_used = (pend[-1] // MOE_ROWS).astype(jnp.int32).reshape(1)
    return buf_t, buf_w.reshape(cap, 1), block_e, n_used, dest.reshape(n_tok, 2)


def _combine_body(final_norm, x_ref, y_ref, g2_ref, fg_ref, o_ref):
    x = x_ref[...] + g2_ref[0] * (y_ref[:, :D_MODEL] + y_ref[:, D_MODEL:])
    if final_norm:
        x = _rms(x) * fg_ref[...]
    o_ref[...] = x


def _combine(xa, y2, g2, ent, final_g, final_norm):
    n = y2.shape[0]
    return pl.pallas_call(
        functools.partial(_combine_body, final_norm),
        grid=(n // ROW_BLOCK,),
        in_specs=[pl.BlockSpec((ROW_BLOCK, D_MODEL), lambda i: (i, 0)),
                  pl.BlockSpec((ROW_BLOCK, 2 * D_MODEL), lambda i: (i, 0)),
                  pl.BlockSpec((1, 1, D_MODEL), lambda i: (ent(i), 0, 0)),
                  pl.BlockSpec(final_g.shape, lambda i: (0, 0))],
        out_specs=pl.BlockSpec((ROW_BLOCK, D_MODEL), lambda i: (i, 0)),
        out_shape=jax.ShapeDtypeStruct((n, D_MODEL), F32),
        compiler_params=_params("parallel"),
        name="moe_combine",
    )(xa, y2, g2, final_g)


def _block_diag2(w):
    k, n = w.shape[1], w.shape[2]
    z = jnp.zeros((k, n), w.dtype)
    return jnp.concatenate([jnp.concatenate([w[0], z], axis=1), jnp.concatenate([z, w[1]], axis=1)], axis=0)


def kernel(x, c, ctx, c_ctx, ada_w, ada_b, norm1_g, norm2_g, w_in, mu_shift, decay_w0, decay_w2, iclr_a0, iclr_a2, gate_g2, k_k, k_a, r_k, gn_w, gn_b, conv_w, conv_gain, four_gain, w_out, router_g_w, router_g_b, router_e_w, router_e_b, exp_gate, exp_up, exp_down, final_g):
    bsz, seq, d = x.shape
    ctx_len = ctx.shape[1]
    depth = ada_w.shape[0]
    n_lat = bsz * seq
    n_ctx = bsz * ctx_len
    lat_blocks_per_batch = seq // ROW_BLOCK
    ent = lambda i: jnp.minimum(i // lat_blocks_per_batch, bsz)

    xa = jnp.concatenate([x.reshape(n_lat, d), ctx.reshape(n_ctx, d)], axis=0)
    cvec = jnp.concatenate([c, c_ctx[None, :]], axis=0)
    cvec = cvec * jax.nn.sigmoid(cvec)
    head_id = np.arange(D_RWKV) // HEAD
    ones_bd = jnp.asarray(head_id[:, None] == head_id[None, :], BF16)
    row2 = lambda a: a.reshape(1, -1)

    for l in range(depth):
        last = l == depth - 1
        mod = jnp.dot(cvec, ada_w[l], precision=lax.Precision.HIGHEST) + ada_b[l]
        sh1, sc1, g1, sh2, sc2, g2 = [mod[:, j * d:(j + 1) * d].reshape(bsz + 1, 1, d) for j in range(6)]

        pz, pcv, pfo = _inproj(xa, row2(norm1_g[l]), sc1, sh1, w_in[l].astype(BF16), ent)
        r, v, kk, g, bonus, lw, kd, b = _streams(
            pz, seq, ctx_len, bsz, row2(mu_shift[l]), _block_diag2(decay_w2[l]), _block_diag2(iclr_a2[l]),
            gate_g2[l], row2(decay_w0[l]), row2(iclr_a0[l]), row2(k_k[l]), row2(k_a[l]), row2(r_k[l]), ones_bd)
        pm, qm, rp, y0 = _chunk_local(r, v, kk, lw, kd, b)
        y = _chunk_scan(pm, qm, rp, y0, seq, ctx_len, bsz)

        fgain = row2(four_gain[l])
        four_x = _fourier_long(pfo[:n_lat].reshape(bsz, seq, D_FOUR), fgain).reshape(n_lat, D_FOUR)
        four_c = _fourier_direct(pfo[n_lat:].reshape(bsz, ctx_len, D_FOUR), fgain).reshape(n_ctx, D_FOUR)
        four_o = jnp.concatenate([four_x, four_c], axis=0)

        wr = jnp.zeros((d, LANES), F32).at[:, :N_GROUPS].set(router_g_w[l])
        wr = wr.at[:, N_GROUPS:N_GROUPS + N_EXPERTS].set(router_e_w[l])
        rb = jnp.zeros((1, LANES), F32).at[0, :N_GROUPS].set(router_g_b[l])
        rb = rb.at[0, N_GROUPS:N_GROUPS + N_EXPERTS].set(router_e_b[l])
        wrh, wrl = _hilo(wr)
        xa, h2, route = _post(y, bonus, g, pcv, four_o, xa, seq, ctx_len, bsz, ent, w_out[l].astype(BF16),
                              row2(gn_w[l]), row2(gn_b[l]), conv_w[l], row2(conv_gain[l]), g1, sc2, sh2,
                              row2(norm2_g[l]), wrh, wrl, rb, ones_bd)

        n_tok = n_lat if last else n_lat + n_ctx
        buf_t, buf_w, block_e, n_used, dest = _dispatch(route, n_tok)
        h_pad = jnp.concatenate([h2[:n_tok], jnp.zeros((1, d), BF16)], axis=0)
        yb = _experts(h_pad[buf_t], buf_w, block_e, n_used, exp_gate[l], exp_up[l], exp_down[l])
        xa = _combine(xa, yb[dest].reshape(n_tok, 2 * d), g2, ent, row2(final_g), last)

    return xa[:n_lat].reshape(bsz, seq, d)
```

```python
import functools

import numpy as np
import jax
import jax.numpy as jnp
from jax import lax
from jax.experimental import pallas as pl
from jax.experimental.pallas import tpu as pltpu

F32 = jnp.float32
BF16 = jnp.bfloat16

D_MODEL = 1024
HEAD = 64
D_RWKV = 512
H_RWKV = D_RWKV // HEAD
D_CONV = 256
D_FOUR = 256
FOUR_GROUP = 64
D_Z = 3 * D_RWKV + 2 * 64 + 2 * 64 + 128
D_IN = D_Z + 3 * D_CONV + D_FOUR
GRID_W = 64
N_GROUPS = 4
EXPERTS_PER_GROUP = 8
N_EXPERTS = N_GROUPS * EXPERTS_PER_GROUP
D_EXPERT = D_MODEL // 2
RMS_EPS = 1e-6
GN_EPS = 64e-5

CHUNK = 64
ROW_BLOCK = 256
MOE_ROWS = 256
FOUR_INNER = 128
LANES = 128
VMEM_LIMIT = 48 * 1024 * 1024

NN = (((1,), (0,)), ((), ()))
NT = (((1,), (1,)), ((), ()))
TN = (((0,), (0,)), ((), ()))


def _params(*sem):
    return pltpu.CompilerParams(dimension_semantics=sem, vmem_limit_bytes=VMEM_LIMIT)


def _split2(x):
    hi = x.astype(BF16)
    lo = (x - hi.astype(F32)).astype(BF16)
    return hi, lo


def _dot(a, b, dims=NN):
    return lax.dot_general(a, b, dims, preferred_element_type=F32)


def _dot1(a, b, dims=NN):
    return _dot(a.astype(BF16), b.astype(BF16), dims)


def _dot3(a, b, dims=NN):
    ah, al = _split2(a)
    bh, bl = _split2(b)
    return _dot(ah, bh, dims) + (_dot(ah, bl, dims) + _dot(al, bh, dims))


def _dot3c(ch, cl, x, dims=NN):
    xh, xl = _split2(x)
    return _dot(ch, xh, dims) + (_dot(cl, xh, dims) + _dot(ch, xl, dims))


def _dot3r(x, ch, cl, dims=NN):
    xh, xl = _split2(x)
    return _dot(xh, ch, dims) + (_dot(xl, ch, dims) + _dot(xh, cl, dims))


def _headsum(x, ones_bd):
    hi, lo = _split2(x)
    return _dot(hi, ones_bd) + _dot(lo, ones_bd)


def _rms(x, eps=RMS_EPS):
    return x * lax.rsqrt(jnp.mean(x * x, axis=-1, keepdims=True) + eps)


def _sigmoid(x):
    return 1.0 / (1.0 + jnp.exp(-x))


def _softplus(x):
    return jnp.maximum(x, 0.0) + jnp.log(1.0 + jnp.exp(-jnp.abs(x)))


def _inproj_body(x_ref, g_ref, sc_ref, sh_ref, w_ref, z_ref, cv_ref, fo_ref):
    h = _rms(x_ref[...]) * g_ref[...] * (1.0 + sc_ref[0]) + sh_ref[0]
    p = _dot(h.astype(BF16), w_ref[...])
    z_ref[...] = p[:, :D_Z]
    cv_ref[...] = p[:, D_Z:D_Z + 3 * D_CONV]
    fo_ref[...] = p[:, D_Z + 3 * D_CONV:]


def _inproj(xa, gain, sc, sh, w_bf, ent):
    n = xa.shape[0]
    row = lambda w: pl.BlockSpec((ROW_BLOCK, w), lambda i: (i, 0))
    full = lambda a: pl.BlockSpec(a.shape, lambda i: (0,) * a.ndim)
    mod = pl.BlockSpec((1, 1, D_MODEL), lambda i: (ent(i), 0, 0))
    return pl.pallas_call(
        _inproj_body,
        grid=(n // ROW_BLOCK,),
        in_specs=[row(D_MODEL), full(gain), mod, mod, full(w_bf)],
        out_specs=[row(D_Z), row(3 * D_CONV), row(D_FOUR)],
        out_shape=[jax.ShapeDtypeStruct((n, D_Z), F32), jax.ShapeDtypeStruct((n, 3 * D_CONV), F32),
                   jax.ShapeDtypeStruct((n, D_FOUR), F32)],
        compiler_params=_params("parallel"),
        name="inproj",
    )(xa, gain, sc, sh, w_bf)


def _streams_body(blocks_per_batch, zm_ref, zp_ref, zn_ref, mu_ref, w2_ref, a2_ref, g2_ref,
                  w0_ref, a0_ref, kkw_ref, ka_ref, rk_ref, ones_ref,
                  r_out, v_out, kk_out, g_out, bon_out, lw_out, kd_out, b_out):
    i = pl.program_id(0)
    seq_pos = i % blocks_per_batch
    ctx_i = (seq_pos == 0).astype(jnp.int32)
    z = zm_ref[...]
    tb = z.shape[0]
    t = lax.broadcasted_iota(jnp.int32, (tb, 1), 0)
    c = lax.broadcasted_iota(jnp.int32, (1, D_Z), 1)
    ctx_v = jnp.zeros((tb, 1), jnp.int32) + ctx_i
    col = t & (GRID_W - 1)
    lmask = (col != 0) | ((ctx_v != 0) & (t != 0))
    rmask = (col != GRID_W - 1) | ((ctx_v != 0) & (t != tb - 1))
    top_v = jnp.zeros((tb, 1), jnp.int32) + (seq_pos == 1).astype(jnp.int32)
    bot_v = jnp.zeros((tb, 1), jnp.int32) + (seq_pos == blocks_per_batch - 1).astype(jnp.int32)
    umask = jnp.logical_not((top_v != 0) & (t < GRID_W))
    dmask = jnp.logical_not((bot_v != 0) & (t >= tb - GRID_W))
    left = jnp.where(lmask, pltpu.roll(z, 1, axis=0), 0.0)
    right = jnp.where(rmask, pltpu.roll(z, tb - 1, axis=0), 0.0)
    up = jnp.where(umask, jnp.concatenate([zp_ref[...], z[:tb - GRID_W]], axis=0), 0.0)
    down = jnp.where(dmask, jnp.concatenate([z[GRID_W:], zn_ref[...]], axis=0), 0.0)
    q = c & jnp.where(ctx_i != 0, 1, 3)
    shifted = jnp.where(q == 0, left, jnp.where(q == 1, right, jnp.where(q == 2, up, down)))
    z = z + (shifted - z) * mu_ref[...]

    r = z[:, 0:D_RWKV]
    k = z[:, D_RWKV:2 * D_RWKV]
    v = z[:, 2 * D_RWKV:3 * D_RWKV]
    lw_in = z[:, 3 * D_RWKV:3 * D_RWKV + 128]
    la_in = z[:, 3 * D_RWKV + 128:3 * D_RWKV + 256]
    lg = z[:, 3 * D_RWKV + 256:]
    ones_bd = ones_ref[...]

    g_out[...] = _dot3(_sigmoid(lg), g2_ref[...])
    kq = k * kkw_ref[...]
    kk = kq / jnp.maximum(jnp.sqrt(_headsum(kq * kq, ones_bd)), 1e-12)
    w_log = -_softplus(-(w0_ref[...] + _dot3(jnp.tanh(lw_in), w2_ref[...]))) - 0.5
    lw = -jnp.exp(w_log)
    a = _sigmoid(a0_ref[...] + _dot3(la_in, a2_ref[...]))
    ka = ka_ref[...]
    ksum = jnp.zeros_like(k)
    for d in range(2):
        a_d = a[:, d * D_RWKV:(d + 1) * D_RWKV]
        k_d = k * (1.0 + (a_d - 1.0) * ka)
        ksum = ksum + k_d
        lw_out[d] = lw[:, d * D_RWKV:(d + 1) * D_RWKV]
        kd_out[d] = k_d
        b_out[d] = kk * a_d
    r_out[...] = r
    v_out[...] = v
    kk_out[...] = kk
    bon_out[...] = _headsum(r * ksum * rk_ref[...], ones_bd) * v


def _streams(pz, seq, ctx_len, bsz, mu, w2bd, a2bd, g2, w0, a0, kkw, ka, rk, ones_bd):
    n = pz.shape[0]
    assert ctx_len == ROW_BLOCK and seq % ROW_BLOCK == 0 and n == bsz * (ctx_len + seq)
    sub = ROW_BLOCK // GRID_W
    last = n // GRID_W - 1
    full = lambda a: pl.BlockSpec(a.shape, lambda i: (0,) * a.ndim)
    row = pl.BlockSpec((ROW_BLOCK, D_RWKV), lambda i: (i, 0))
    row2 = pl.BlockSpec((2, ROW_BLOCK, D_RWKV), lambda i: (0, i, 0))
    s1 = jax.ShapeDtypeStruct((n, D_RWKV), F32)
    s2 = jax.ShapeDtypeStruct((2, n, D_RWKV), F32)
    consts = (mu, w2bd, a2bd, g2, w0, a0, kkw, ka, rk, ones_bd)
    return pl.pallas_call(
        functools.partial(_streams_body, (ctx_len + seq) // ROW_BLOCK),
        grid=(n // ROW_BLOCK,),
        in_specs=[pl.BlockSpec((ROW_BLOCK, D_Z), lambda i: (i, 0)),
                  pl.BlockSpec((GRID_W, D_Z), lambda i: (jnp.maximum(i * sub - 1, 0), 0)),
                  pl.BlockSpec((GRID_W, D_Z), lambda i: (jnp.minimum((i + 1) * sub, last), 0))]
                 + [full(a) for a in consts],
        out_specs=[row] * 5 + [row2] * 3,
        out_shape=[s1] * 5 + [s2] * 3,
        compiler_params=_params("parallel"),
        name="streams",
    )(pz, pz, pz, *consts)


def _chunk_body(r_ref, v_ref, kk_ref, lw_ref, kd_ref, b_ref, p_out, q_out, rp_out, y0_out):
    d = pl.program_id(0)
    cs = CHUNK
    rows = r_ref.shape[0]
    n_chunks = rows // cs
    n_pairs = D_RWKV // LANES
    sgn = jnp.where(d == 0, 1, -1)

    ri = lax.broadcasted_iota(jnp.int32, (rows, rows), 0)
    rj = lax.broadcasted_iota(jnp.int32, (rows, rows), 1)
    same_chunk = (ri >> 6) == (rj >> 6)
    t_incl = (same_chunk & ((ri - rj) * sgn >= 0)).astype(F32).astype(BF16)
    t_all = same_chunk.astype(F32).astype(BF16)
    lw = lw_ref[0]
    l1 = lw.astype(BF16)
    rem = lw - l1.astype(F32)
    l2 = rem.astype(BF16)
    l3 = (rem - l2.astype(F32)).astype(BF16)
    gcum = _dot(t_incl, l1) + (_dot(t_incl, l2) + _dot(t_incl, l3))
    gtot = _dot(t_all, l1) + (_dot(t_all, l2) + _dot(t_all, l3))
    e_pos = jnp.exp(gcum)
    e_neg = jnp.exp(-gcum)
    e_rem = jnp.exp(gtot - gcum)
    e_prev = jnp.exp(gcum - lw)
    e_tot = jnp.exp(gtot)

    r = r_ref[...]
    v = v_ref[...]
    kk = kk_ref[...]
    kd = kd_ref[0]
    b = b_ref[0]
    at = -(kk * e_prev)
    bt = b * e_neg
    kt = kd * e_neg
    rt = r * e_pos
    bh = b * e_rem
    kh = kd * e_rem

    ti = lax.broadcasted_iota(jnp.int32, (cs, LANES), 0)
    lane = lax.broadcasted_iota(jnp.int32, (cs, LANES), 1)
    tj = lane & (HEAD - 1)
    lo_half = lane < HEAD
    order = (ti - tj) * sgn
    strict = order > 0
    incl = order >= 0
    eye = ti == tj
    eye_f = eye.astype(F32)
    zero_bf = jnp.zeros((cs, LANES), BF16)

    def bd(y):
        y = y.astype(BF16)
        return jnp.concatenate([jnp.where(lo_half, y, zero_bf), jnp.where(lo_half, zero_bf, y)], axis=0)

    def mm(x, y_bd, dims=NN):
        return _dot(x.astype(BF16), y_bd, dims)

    chains = [(c, p) for c in range(n_chunks) for p in range(n_pairs)]
    tile = lambda a, c, p: a[c * cs:(c + 1) * cs, p * LANES:(p + 1) * LANES]

    s = [mm(jnp.concatenate([tile(at, c, p), tile(rt, c, p)], axis=0),
            jnp.concatenate([bd(tile(bt, c, p)), bd(tile(kt, c, p))], axis=0), NT) for c, p in chains]
    a_ab = [jnp.where(strict, x[:cs, :LANES], 0.0) for x in s]
    a_ak = [jnp.where(strict, x[:cs, LANES:], 0.0) for x in s]
    a_rbk = [jnp.concatenate([jnp.where(incl, x[cs:, :LANES], 0.0), jnp.where(incl, x[cs:, LANES:], 0.0)], axis=1)
             for x in s]
    a0 = [jnp.where((ti >> 3) == (tj >> 3), x, 0.0) for x in a_ab]
    a2 = [mm(x, bd(x)) for x in a0]
    a2_bd = [bd(x) for x in a2]
    x0 = [eye_f + x for x in a0]
    x1 = [x + mm(x, y) for x, y in zip(x0, a2_bd)]
    a4 = [mm(x, y) for x, y in zip(a2, a2_bd)]
    minv = [x + mm(x, bd(y)) for x, y in zip(x1, a4)]
    lvl = 3
    while (1 << lvl) < cs:
        off = ((ti >> (lvl + 1)) == (tj >> (lvl + 1))) & ((ti >> lvl) != (tj >> lvl))
        t = [mm(jnp.where(off, x, 0.0), bd(y)) for x, y in zip(a_ab, minv)]
        minv = [x + mm(x, bd(y)) for x, y in zip(minv, t)]
        lvl += 1
    v_bd = [bd(tile(v, c, p)) for c, p in chains]
    av = [mm(x, y) for x, y in zip(a_ak, v_bd)]
    wu = [mm(m, jnp.concatenate([bd(tile(at, c, p)), bd(y)], axis=1)) for m, y, (c, p) in zip(minv, av, chains)]
    top = [mm(jnp.concatenate([tile(bh, c, p), tile(kh, c, p)], axis=0),
              jnp.concatenate([x.astype(BF16), jnp.concatenate([zero_bf, tile(v, c, p).astype(BF16)], axis=1)],
                              axis=0), TN) for x, (c, p) in zip(wu, chains)]
    bot = [mm(x, jnp.concatenate([jnp.concatenate([bd(y[:, :LANES]), bd(y[:, LANES:])], axis=1),
                                  jnp.concatenate([jnp.zeros((2 * cs, LANES), BF16), vb], axis=1)], axis=0))
           for x, y, vb in zip(a_rbk, wu, v_bd)]
    for c in range(n_chunks):
        sel = lambda x, off: jnp.where(lo_half, x[:cs, off:off + LANES], x[cs:, off:off + LANES])
        idx = [c * n_pairs + p for p in range(n_pairs)]
        p_out[0, c] = jnp.concatenate(
            [sel(top[i], 0) + jnp.where(eye, tile(e_tot, c, p), 0.0) for p, i in enumerate(idx)], axis=1)
        q_out[0, c] = jnp.concatenate([sel(top[i], LANES) for i in idx], axis=1)
        rp_out[0, c * cs:(c + 1) * cs, :] = jnp.concatenate(
            [tile(rt, c, p) + bot[i][:, :LANES] for p, i in enumerate(idx)], axis=1)
        y0_out[0, c * cs:(c + 1) * cs, :] = jnp.concatenate([bot[i][:, LANES:] for i in idx], axis=1)


CHUNKS_PER_STEP = 2


def _chunk_local(r, v, kk, lw, kd, b):
    n = r.shape[0]
    nc = n // CHUNK
    rows = CHUNKS_PER_STEP * CHUNK
    shared = pl.BlockSpec((rows, D_RWKV), lambda d, j: (j, 0))
    per_dir = pl.BlockSpec((1, rows, D_RWKV), lambda d, j: (d, j, 0))
    mat = pl.BlockSpec((1, CHUNKS_PER_STEP, HEAD, D_RWKV), lambda d, j: (d, j, 0, 0))
    sm = jax.ShapeDtypeStruct((2, nc, HEAD, D_RWKV), F32)
    st = jax.ShapeDtypeStruct((2, n, D_RWKV), F32)
    return pl.pallas_call(
        _chunk_body,
        grid=(2, nc // CHUNKS_PER_STEP),
        in_specs=[shared, shared, shared, per_dir, per_dir, per_dir],
        out_specs=[mat, mat, per_dir, per_dir],
        out_shape=[sm, sm, st, st],
        compiler_params=_params("parallel", "parallel"),
        name="chunk_local",
    )(r, v, kk, lw, kd, b)


SCAN_CHUNKS = 4


def _scan_body(pf_ref, qf_ref, rpf_ref, y0f_ref, pb_ref, qb_ref, rpb_ref, y0b_ref, yf_out, yb_out, h_scr):
    @pl.when(pl.program_id(0) == 0)
    def _():
        h_scr[...] = jnp.zeros_like(h_scr)

    cs = CHUNK
    bsz = h_scr.shape[1]
    n_pairs = D_RWKV // LANES
    lane = lax.broadcasted_iota(jnp.int32, (cs, LANES), 1)
    lo_half = lane < HEAD
    zero_bf = jnp.zeros((cs, LANES), BF16)

    def bd(y):
        return jnp.concatenate([jnp.where(lo_half, y, zero_bf), jnp.where(lo_half, zero_bf, y)], axis=0)

    dirs = ((pf_ref, qf_ref, rpf_ref, y0f_ref, yf_out), (pb_ref, qb_ref, rpb_ref, y0b_ref, yb_out))
    chains = [(d, b, p) for d in range(2) for b in range(bsz) for p in range(n_pairs)]
    state = [h_scr[d, b, :, p * LANES:(p + 1) * LANES] for d, b, p in chains]
    for s in range(SCAN_CHUNKS):
        outs = []
        for (d, b, p), hcur in zip(chains, state):
            p_ref, _, rp_ref, _, _ = dirs[d]
            c = s if d == 0 else SCAN_CHUNKS - 1 - s
            ls = slice(p * LANES, (p + 1) * LANES)
            x = jnp.concatenate([p_ref[0, b, c, :, ls], rp_ref[0, b, c * cs:(c + 1) * cs, ls]], axis=0)
            xh, xl = _split2(x)
            hh, hl = _split2(hcur)
            hh, hl = bd(hh), bd(hl)
            outs.append(_dot(xh, hh) + (_dot(xh, hl) + _dot(xl, hh)))
        new_state = []
        for (d, b, p), o in zip(chains, outs):
            _, q_ref, _, y0_ref, y_out = dirs[d]
            c = s if d == 0 else SCAN_CHUNKS - 1 - s
            ls = slice(p * LANES, (p + 1) * LANES)
            new_state.append(o[:cs] + q_ref[0, b, c, :, ls])
            y_out[b, c * cs:(c + 1) * cs, ls] = y0_ref[0, b, c * cs:(c + 1) * cs, ls] + o[cs:]
        state = new_state
    for (d, b, p), hcur in zip(chains, state):
        h_scr[d, b, :, p * LANES:(p + 1) * LANES] = hcur


def _chunk_scan(p, q, rp, y0, seq, ctx_len, bsz):
    n = rp.shape[1]
    rows_b = ctx_len + seq
    ncb = rows_b // CHUNK
    assert ctx_len % (SCAN_CHUNKS * CHUNK) == 0 and seq % (SCAN_CHUNKS * CHUNK) == 0
    steps = ncb // SCAN_CHUNKS
    ctx_steps = ctx_len // (SCAN_CHUNKS * CHUNK)
    p5 = p.reshape(2, bsz, ncb, HEAD, D_RWKV)
    q5 = q.reshape(2, bsz, ncb, HEAD, D_RWKV)
    rp4 = rp.reshape(2, bsz, rows_b, D_RWKV)
    y04 = y0.reshape(2, bsz, rows_b, D_RWKV)
    pos_f = lambda i: i
    pos_b = lambda i: jnp.where(i < ctx_steps, ctx_steps - 1 - i, steps - 1 - (i - ctx_steps))
    rows = SCAN_CHUNKS * CHUNK

    def specs(d, pos):
        mat = pl.BlockSpec((1, bsz, SCAN_CHUNKS, HEAD, D_RWKV), lambda i: (d, 0, pos(i), 0, 0))
        tok = pl.BlockSpec((1, bsz, rows, D_RWKV), lambda i: (d, 0, pos(i), 0))
        return [mat, mat, tok, tok]

    out_f = pl.BlockSpec((bsz, rows, D_RWKV), lambda i: (0, pos_f(i), 0))
    out_b = pl.BlockSpec((bsz, rows, D_RWKV), lambda i: (0, pos_b(i), 0))
    shp = jax.ShapeDtypeStruct((bsz, rows_b, D_RWKV), F32)
    yf, yb = pl.pallas_call(
        _scan_body,
        grid=(steps,),
        in_specs=specs(0, pos_f) + specs(1, pos_b),
        out_specs=[out_f, out_b],
        out_shape=[shp, shp],
        scratch_shapes=[pltpu.VMEM((2, bsz, HEAD, D_RWKV), F32)],
        compiler_params=_params("arbitrary"),
        name="chunk_scan",
    )(p5, q5, rp4, y04, p5, q5, rp4, y04)
    return yf.reshape(n, D_RWKV), yb.reshape(n, D_RWKV)


def _hilo(a):
    if isinstance(a, np.ndarray):
        a = a.astype(np.float32)
        hi = a.astype(BF16)
        return jnp.asarray(hi), jnp.asarray((a - hi.astype(np.float32)).astype(BF16))
    hi = a.astype(BF16)
    return hi, (a - hi.astype(F32)).astype(BF16)


def _channel_tables(length):
    j = np.arange(FOUR_GROUP)
    ang = 2.0 * np.pi * np.outer(j, j) / FOUR_GROUP
    scale = 1.0 / np.sqrt(float(length) * FOUR_GROUP)
    groups = D_FOUR // FOUR_GROUP
    c4 = np.kron(np.eye(groups), np.cos(ang)) * scale
    s4 = np.kron(np.eye(groups), np.sin(ang)) * scale
    return _hilo(c4) + _hilo(s4)


def _four_finish(fr, fi, c4h, c4l, s4h, s4l, gain):
    y = _dot3r(fr, c4h, c4l) + _dot3r(fi, s4h, s4l)
    return _rms(y) * gain


def _four_direct_body(f_ref, mh_ref, ml_ref, c4h, c4l, s4h, s4l, gain_ref, o_ref):
    length = f_ref.shape[1]
    fc = _dot3c(mh_ref[...], ml_ref[...], f_ref[0])
    o_ref[0] = _four_finish(fc[:length], fc[length:], c4h[...], c4l[...], s4h[...], s4l[...], gain_ref[...])


def _fourier_direct(f, gain):
    bsz, length, _ = f.shape
    t = np.arange(length)
    ang = 2.0 * np.pi * (np.outer(t, t) % length) / length
    mh, ml = _hilo(np.concatenate([np.cos(ang), -np.sin(ang)], axis=0))
    consts = (mh, ml) + _channel_tables(length) + (gain,)
    full = lambda a: pl.BlockSpec(a.shape, lambda b: (0,) * a.ndim)
    blk = pl.BlockSpec((1, length, D_FOUR), lambda b: (b, 0, 0))
    return pl.pallas_call(
        _four_direct_body,
        grid=(bsz,),
        in_specs=[blk] + [full(a) for a in consts],
        out_specs=blk,
        out_shape=jax.ShapeDtypeStruct(f.shape, F32),
        compiler_params=_params("parallel"),
        name="fourier_direct",
    )(f, *consts)


def _four_stage1_body(f_ref, mh_ref, ml_ref, o_ref):
    o_ref[0] = _dot3c(mh_ref[...], ml_ref[...], f_ref[0])


def _four_stage2_body(zr_ref, zi_ref, mh_ref, ml_ref, c4h, c4l, s4h, s4l, gain_ref, o_ref):
    inner = zr_ref.shape[2]
    zz = jnp.concatenate([zr_ref[0, 0], zi_ref[0, 0]], axis=0)
    fc = _dot3c(mh_ref[0], ml_ref[0], zz)
    o_ref[0] = _four_finish(fc[:inner], fc[inner:], c4h[...], c4l[...], s4h[...], s4l[...], gain_ref[...])


def _fourier_long(f, gain):
    bsz, length, _ = f.shape
    l2 = FOUR_INNER
    l1 = length // l2
    assert l1 * l2 == length and l1 % 16 == 0
    cols = l2 * D_FOUR
    col_tile = 4096
    th = np.arange(l1)
    ang1 = 2.0 * np.pi * (np.outer(th, th) % l1) / l1
    m1h, m1l = _hilo(np.concatenate([np.cos(ang1), -np.sin(ang1)], axis=0))
    z = pl.pallas_call(
        _four_stage1_body,
        grid=(bsz, cols // col_tile),
        in_specs=[pl.BlockSpec((1, l1, col_tile), lambda b, c: (b, 0, c)),
                  pl.BlockSpec(m1h.shape, lambda b, c: (0, 0)),
                  pl.BlockSpec(m1l.shape, lambda b, c: (0, 0))],
        out_specs=pl.BlockSpec((1, 2 * l1, col_tile), lambda b, c: (b, 0, c)),
        out_shape=jax.ShapeDtypeStruct((bsz, 2 * l1, cols), F32),
        compiler_params=_params("parallel", "parallel"),
        name="fourier_stage1",
    )(f.reshape(bsz, l1, cols), m1h, m1l)
    z = z.reshape(bsz, 2 * l1, l2, D_FOUR)

    ma = np.arange(l1)[:, None, None]
    mb = np.arange(l2)[None, :, None]
    tl = np.arange(l2)[None, None, :]
    ang2 = 2.0 * np.pi * (((ma + l1 * mb) * tl) % length) / length
    cos2, sin2 = np.cos(ang2), np.sin(ang2)
    m2 = np.concatenate([np.concatenate([cos2, sin2], axis=2), np.concatenate([-sin2, cos2], axis=2)], axis=1)
    m2h, m2l = _hilo(m2)
    consts = _channel_tables(length) + (gain,)
    full = lambda a: pl.BlockSpec(a.shape, lambda b, m: (0,) * a.ndim)
    tab = pl.BlockSpec((1, 2 * l2, 2 * l2), lambda b, m: (m, 0, 0))
    out = pl.pallas_call(
        _four_stage2_body,
        grid=(bsz, l1),
        in_specs=[pl.BlockSpec((1, 1, l2, D_FOUR), lambda b, m: (b, m, 0, 0)),
                  pl.BlockSpec((1, 1, l2, D_FOUR), lambda b, m: (b, l1 + m, 0, 0)),
                  tab, tab] + [full(a) for a in consts],
        out_specs=pl.BlockSpec((1, l2, D_FOUR), lambda b, m: (b, 0, m)),
        out_shape=jax.ShapeDtypeStruct((bsz, l2, l1 * D_FOUR), F32),
        compiler_params=_params("parallel", "parallel"),
        name="fourier_stage2",
    )(z, z, m2h, m2l, *consts)
    return out.reshape(bsz, length, D_FOUR)


def _route(logits):
    lane = lax.broadcasted_iota(jnp.int32, (1, LANES), 1)
    lane_f = lane.astype(F32)
    neg = jnp.float32(-1e30)
    big = jnp.float32(1e9)
    gl = jnp.where(lane < N_GROUPS, logits, neg)
    gmax = jnp.max(gl, axis=-1, keepdims=True)
    pg_top = 1.0 / jnp.sum(jnp.exp(gl - gmax), axis=-1, keepdims=True)
    grp = jnp.min(jnp.where(gl == gmax, lane_f, big), axis=-1, keepdims=True)
    e_lane = lane - N_GROUPS
    in_grp = (e_lane >= 0) & (e_lane < N_EXPERTS) & ((e_lane >> 3).astype(F32) == grp)
    el = jnp.where(in_grp, logits, neg)
    m1 = jnp.max(el, axis=-1, keepdims=True)
    i1 = jnp.min(jnp.where(el == m1, lane_f, big), axis=-1, keepdims=True)
    el2 = jnp.where(lane_f == i1, neg, el)
    m2 = jnp.max(el2, axis=-1, keepdims=True)
    i2 = jnp.min(jnp.where(el2 == m2, lane_f, big), axis=-1, keepdims=True)
    e2 = jnp.exp(m2 - m1)
    den = 1.0 + e2
    return pg_top / den, pg_top * e2 / den, i1 - N_GROUPS, i2 - N_GROUPS


def _post_body(blocks_per_batch, yf_ref, yb_ref, bon_ref, g_ref, cv_ref, cvp_ref, cvn_ref, fo_ref, x_ref,
               wo_ref, gnw_ref, gnb_ref, cw_ref, cg_ref, g1_ref, sc2_ref, sh2_ref, n2_ref, wrh_ref, wrl_ref,
               rb_ref, ones_ref, x_out, h_out, route_out, count_out, count_scr):
    i = pl.program_id(0)

    @pl.when(i == 0)
    def _():
        count_scr[...] = jnp.zeros_like(count_scr)

    ones_bd = ones_ref[...]
    y = yf_ref[...] + yb_ref[...]
    mu = _headsum(y, ones_bd) * (1.0 / HEAD)
    yc = y - mu
    var = _headsum(yc * yc, ones_bd) * (1.0 / HEAD)
    yn = yc * lax.rsqrt(var + GN_EPS) * gnw_ref[...] + gnb_ref[...]
    o_rwkv = (yn + bon_ref[...]) * g_ref[...]

    cv = cv_ref[...]
    tb = cv.shape[0]
    t = lax.broadcasted_iota(jnp.int32, (tb, 1), 0)
    seq_pos = i % blocks_per_batch
    first = (seq_pos <= 1).astype(F32)
    final = jnp.logical_or(seq_pos == 0, seq_pos == blocks_per_batch - 1).astype(F32)
    zc = cv[:, D_CONV:2 * D_CONV] * cv[:, 2 * D_CONV:]
    zp_row = cvp_ref[7:8, D_CONV:2 * D_CONV] * cvp_ref[7:8, 2 * D_CONV:] * (1.0 - first)
    zn_row = cvn_ref[0:1, D_CONV:2 * D_CONV] * cvn_ref[0:1, 2 * D_CONV:] * (1.0 - final)
    prev = jnp.where(t == 0, zp_row, pltpu.roll(zc, 1, axis=0))
    nxt = jnp.where(t == tb - 1, zn_row, pltpu.roll(zc, tb - 1, axis=0))
    cw = cw_ref[...]
    conv = cv[:, :D_CONV] * (cw[0:1] * prev + cw[1:2] * zc + cw[2:3] * nxt)
    conv_o = _rms(conv) * cg_ref[...]

    mix = (_dot(o_rwkv.astype(BF16), wo_ref[0:D_RWKV, :])
           + _dot(conv_o.astype(BF16), wo_ref[D_RWKV:D_RWKV + D_CONV, :])
           + _dot(fo_ref[...].astype(BF16), wo_ref[D_RWKV + D_CONV:, :]))
    x = x_ref[...] + g1_ref[0] * mix
    x_out[...] = x
    h2 = _rms(x) * n2_ref[...] * (1.0 + sc2_ref[0]) + sh2_ref[0]
    h_out[...] = h2

    gate0, gate1, e0, e1 = _route(_dot3r(h2, wrh_ref[...], wrl_ref[...]) + rb_ref[...])
    lane = lax.broadcasted_iota(jnp.int32, (1, LANES), 1)
    lane_f = lane.astype(F32)
    oh0 = (lane_f == e0).astype(F32)
    oh1 = (lane_f == e1).astype(F32)
    ri = lax.broadcasted_iota(jnp.int32, (tb, tb), 0)
    rj = lax.broadcasted_iota(jnp.int32, (tb, tb), 1)
    earlier = (rj < ri).astype(F32).astype(BF16)
    seen = count_scr[...]
    tot0 = jnp.sum(oh0, axis=0, keepdims=True)
    before0 = _dot(earlier, oh0.astype(BF16)) + seen
    before1 = _dot(earlier, oh1.astype(BF16)) + (seen + tot0)
    rank0 = jnp.sum(oh0 * before0, axis=-1, keepdims=True)
    rank1 = jnp.sum(oh1 * before1, axis=-1, keepdims=True)
    seen = seen + tot0 + jnp.sum(oh1, axis=0, keepdims=True)
    count_scr[...] = seen
    count_out[...] = seen
    route_out[...] = jnp.where(
        lane == 0, gate0, jnp.where(lane == 1, gate1, jnp.where(lane == 2, e0, jnp.where(
            lane == 3, e1, jnp.where(lane == 4, rank0, jnp.where(lane == 5, rank1, 0.0))))))


def _post(yf, yb, bonus, g, pcv, four_o, xa, seq, ctx_len, ent, wo_bf, gnw, gnb, cw, cg, g1, sc2, sh2, n2,
          wrh, wrl, rb, ones_bd):
    n = xa.shape[0]
    assert ctx_len == ROW_BLOCK
    sub = ROW_BLOCK // 8
    last = n // 8 - 1
    full = lambda a: pl.BlockSpec(a.shape, lambda i: (0,) * a.ndim)
    row = lambda w: pl.BlockSpec((ROW_BLOCK, w), lambda i: (i, 0))
    mod = pl.BlockSpec((1, 1, D_MODEL), lambda i: (ent(i), 0, 0))
    return pl.pallas_call(
        functools.partial(_post_body, (ctx_len + seq) // ROW_BLOCK),
        grid=(n // ROW_BLOCK,),
        in_specs=[row(D_RWKV), row(D_RWKV), row(D_RWKV), row(D_RWKV), row(3 * D_CONV),
                  pl.BlockSpec((8, 3 * D_CONV), lambda i: (jnp.maximum(i * sub - 1, 0), 0)),
                  pl.BlockSpec((8, 3 * D_CONV), lambda i: (jnp.minimum((i + 1) * sub, last), 0)),
                  row(D_FOUR), row(D_MODEL), full(wo_bf), full(gnw), full(gnb), full(cw), full(cg),
                  mod, mod, mod, full(n2), full(wrh), full(wrl), full(rb), full(ones_bd)],
        out_specs=[row(D_MODEL), row(D_MODEL), row(LANES), pl.BlockSpec((1, LANES), lambda i: (0, 0))],
        out_shape=[jax.ShapeDtypeStruct((n, D_MODEL), F32), jax.ShapeDtypeStruct((n, D_MODEL), F32),
                   jax.ShapeDtypeStruct((n, LANES), F32), jax.ShapeDtypeStruct((1, LANES), F32)],
        scratch_shapes=[pltpu.VMEM((1, LANES), F32)],
        compiler_params=_params("arbitrary"),
        name="post_mix",
    )(yf, yb, bonus, g, pcv, pcv, pcv, four_o, xa, wo_bf, gnw, gnb, cw, cg, g1, sc2, sh2, n2, wrh, wrl, rb,
      ones_bd)


def _scatter_rows_body(dest_ref, h_ref, xb_in_ref, xb_ref, sem):
    del xb_in_ref
    tb = h_ref.shape[0]

    def row_copy(t, k):
        return pltpu.make_async_copy(h_ref.at[pl.ds(t, 1), :],
                                     xb_ref.at[pl.ds(dest_ref[0, 0, 2 * t + k], 1), :], sem)

    def issue(t, carry):
        row_copy(t, 0).start()
        row_copy(t, 1).start()
        return carry

    def drain(t, carry):
        row_copy(t, 0).wait()
        row_copy(t, 1).wait()
        return carry

    lax.fori_loop(0, tb, issue, 0)
    lax.fori_loop(0, tb, drain, 0)


def _scatter_rows(h2, dest3, cap):
    n = h2.shape[0]
    return pl.pallas_call(
        _scatter_rows_body,
        grid=(n // ROW_BLOCK,),
        in_specs=[pl.BlockSpec((1, 1, 2 * ROW_BLOCK), lambda i: (i, 0, 0), memory_space=pltpu.SMEM),
                  pl.BlockSpec((ROW_BLOCK, D_MODEL), lambda i: (i, 0)),
                  pl.BlockSpec(memory_space=pl.ANY)],
        out_specs=pl.BlockSpec(memory_space=pl.ANY),
        out_shape=jax.ShapeDtypeStruct((cap, D_MODEL), F32),
        scratch_shapes=[pltpu.SemaphoreType.DMA(())],
        input_output_aliases={2: 0},
        compiler_params=_params("arbitrary"),
        name="moe_scatter",
    )(dest3, h2, jnp.zeros((cap, D_MODEL), F32))


def _expert_body(be_ref, nu_ref, xb_ref, wg_ref, wu_ref, wd_ref, o_ref, wg_s, wu_s, wd_s):
    i = pl.program_id(0)
    prev = be_ref[jnp.maximum(i - 1, 0)]
    fresh = jnp.logical_or(i == 0, be_ref[i] != prev)

    @pl.when(jnp.logical_and(fresh, i < nu_ref[0]))
    def _():
        wg_s[...] = wg_ref[0].astype(BF16)
        wu_s[...] = wu_ref[0].astype(BF16)
        wd_s[...] = wd_ref[0].astype(BF16)

    @pl.when(i < nu_ref[0])
    def _():
        xb = xb_ref[...].astype(BF16)
        gate = _dot(xb, wg_s[...])
        up = _dot(xb, wu_s[...])
        act = gate * _sigmoid(gate) * up
        o_ref[...] = _dot(act.astype(BF16), wd_s[...])

    @pl.when(i >= nu_ref[0])
    def _():
        o_ref[...] = jnp.zeros_like(o_ref)


def _experts(xb, block_e, n_used, exp_gate, exp_up, exp_down):
    cap = xb.shape[0]
    nb = cap // MOE_ROWS
    grid_spec = pltpu.PrefetchScalarGridSpec(
        num_scalar_prefetch=2,
        grid=(nb,),
        in_specs=[pl.BlockSpec((MOE_ROWS, D_MODEL), lambda i, be, nu: (i, 0)),
                  pl.BlockSpec((1, D_MODEL, D_EXPERT), lambda i, be, nu: (be[i], 0, 0)),
                  pl.BlockSpec((1, D_MODEL, D_EXPERT), lambda i, be, nu: (be[i], 0, 0)),
                  pl.BlockSpec((1, D_EXPERT, D_MODEL), lambda i, be, nu: (be[i], 0, 0))],
        out_specs=pl.BlockSpec((MOE_ROWS, D_MODEL), lambda i, be, nu: (i, 0)),
        scratch_shapes=[pltpu.VMEM((D_MODEL, D_EXPERT), BF16), pltpu.VMEM((D_MODEL, D_EXPERT), BF16),
                        pltpu.VMEM((D_EXPERT, D_MODEL), BF16)],
    )
    return pl.pallas_call(
        _expert_body,
        grid_spec=grid_spec,
        out_shape=jax.ShapeDtypeStruct((cap, D_MODEL), F32),
        compiler_params=_params("arbitrary"),
        name="experts",
    )(block_e, n_used, xb, exp_gate, exp_up, exp_down)


def _dispatch(route, counts):
    n = route.shape[0]
    eid = route[:, 2:4].astype(jnp.int32)
    rank = route[:, 4:6].astype(jnp.int32)
    counts = counts[0, :N_EXPERTS].astype(jnp.int32)
    pcounts = (counts + MOE_ROWS - 1) // MOE_ROWS * MOE_ROWS
    pend = jnp.cumsum(pcounts)
    pstart = pend - pcounts
    experts = jnp.arange(N_EXPERTS, dtype=jnp.int32)
    dest = jnp.sum(jnp.where(eid[:, :, None] == experts, pstart, 0), axis=-1) + rank
    nb = -(-2 * n // MOE_ROWS) + N_EXPERTS
    block_start = jnp.arange(nb, dtype=jnp.int32) * MOE_ROWS
    block_e = jnp.minimum(jnp.sum((pend[None, :] <= block_start[:, None]).astype(jnp.int32), axis=1),
                          N_EXPERTS - 1)
    n_used = (pend[-1] // MOE_ROWS).astype(jnp.int32).reshape(1)
    return dest.reshape(n // ROW_BLOCK, 1, 2 * ROW_BLOCK), block_e, n_used, nb * MOE_ROWS


def _combine_body(final_norm, dest_ref, x_ref, route_ref, g2_ref, fg_ref, yb_ref, o_ref, buf, sem):
    tb = x_ref.shape[0]

    def row_copy(t, k):
        return pltpu.make_async_copy(yb_ref.at[pl.ds(dest_ref[0, 0, 2 * t + k], 1), :],
                                     buf.at[k, pl.ds(t, 1), :], sem)

    def issue(t, carry):
        row_copy(t, 0).start()
        row_copy(t, 1).start()
        return carry

    def drain(t, carry):
        row_copy(t, 0).wait()
        row_copy(t, 1).wait()
        return carry

    lax.fori_loop(0, tb, issue, 0)
    lax.fori_loop(0, tb, drain, 0)
    route = route_ref[...]
    x = x_ref[...] + g2_ref[0] * (route[:, 0:1] * buf[0] + route[:, 1:2] * buf[1])
    if final_norm:
        x = _rms(x) * fg_ref[...]
    o_ref[...] = x


def _combine(xa, yb, route, dest3, g2, final_g, blk, ent, n_out_blocks, final_norm):
    return pl.pallas_call(
        functools.partial(_combine_body, final_norm),
        grid=(n_out_blocks,),
        in_specs=[pl.BlockSpec((1, 1, 2 * ROW_BLOCK), lambda i: (blk(i), 0, 0), memory_space=pltpu.SMEM),
                  pl.BlockSpec((ROW_BLOCK, D_MODEL), lambda i: (blk(i), 0)),
                  pl.BlockSpec((ROW_BLOCK, LANES), lambda i: (blk(i), 0)),
                  pl.BlockSpec((1, 1, D_MODEL), lambda i: (ent(blk(i)), 0, 0)),
                  pl.BlockSpec(final_g.shape, lambda i: (0, 0)),
                  pl.BlockSpec(memory_space=pl.ANY)],
        out_specs=pl.BlockSpec((ROW_BLOCK, D_MODEL), lambda i: (i, 0)),
        out_shape=jax.ShapeDtypeStruct((n_out_blocks * ROW_BLOCK, D_MODEL), F32),
        scratch_shapes=[pltpu.VMEM((2, ROW_BLOCK, D_MODEL), F32), pltpu.SemaphoreType.DMA(())],
        compiler_params=_params("arbitrary"),
        name="moe_combine",
    )(dest3, xa, route, g2, final_g, yb)


def _block_diag2(w):
    k, n = w.shape[1], w.shape[2]
    z = jnp.zeros((k, n), w.dtype)
    return jnp.concatenate([jnp.concatenate([w[0], z], axis=1), jnp.concatenate([z, w[1]], axis=1)], axis=0)


def kernel(x, c, ctx, c_ctx, ada_w, ada_b, norm1_g, norm2_g, w_in, mu_shift, decay_w0, decay_w2, iclr_a0, iclr_a2, gate_g2, k_k, k_a, r_k, gn_w, gn_b, conv_w, conv_gain, four_gain, w_out, router_g_w, router_g_b, router_e_w, router_e_b, exp_gate, exp_up, exp_down, final_g):
    bsz, seq, d = x.shape
    ctx_len = ctx.shape[1]
    depth = ada_w.shape[0]
    rows_b = ctx_len + seq
    n = bsz * rows_b
    bpb = rows_b // ROW_BLOCK
    lat_bpb = seq // ROW_BLOCK
    ent = lambda i: jnp.where(i % bpb == 0, bsz, i // bpb)

    xa = jnp.concatenate([ctx, x], axis=1).reshape(n, d)
    cvec = jnp.concatenate([c, c_ctx[None, :]], axis=0)
    cvec = cvec * jax.nn.sigmoid(cvec)
    head_id = np.arange(D_RWKV) // HEAD
    ones_bd = jnp.asarray(head_id[:, None] == head_id[None, :], BF16)
    row2 = lambda a: a.reshape(1, -1)

    for l in range(depth):
        last = l == depth - 1
        mod = jnp.dot(cvec, ada_w[l], precision=lax.Precision.HIGHEST) + ada_b[l]
        sh1, sc1, g1, sh2, sc2, g2 = [mod[:, j * d:(j + 1) * d].reshape(bsz + 1, 1, d) for j in range(6)]

        pz, pcv, pfo = _inproj(xa, row2(norm1_g[l]), sc1, sh1, w_in[l].astype(BF16), ent)
        r, v, kk, g, bonus, lw, kd, b = _streams(
            pz, seq, ctx_len, bsz, row2(mu_shift[l]), _block_diag2(decay_w2[l]), _block_diag2(iclr_a2[l]),
            gate_g2[l], row2(decay_w0[l]), row2(iclr_a0[l]), row2(k_k[l]), row2(k_a[l]), row2(r_k[l]), ones_bd)
        pm, qm, rp, y0 = _chunk_local(r, v, kk, lw, kd, b)
        yf, yb = _chunk_scan(pm, qm, rp, y0, seq, ctx_len, bsz)

        fgain = row2(four_gain[l])
        pfo3 = pfo.reshape(bsz, rows_b, D_FOUR)
        four_o = jnp.concatenate([_fourier_direct(pfo3[:, :ctx_len], fgain),
                                  _fourier_long(pfo3[:, ctx_len:], fgain)], axis=1).reshape(n, D_FOUR)

        wr = jnp.zeros((d, LANES), F32).at[:, :N_GROUPS].set(router_g_w[l])
        wr = wr.at[:, N_GROUPS:N_GROUPS + N_EXPERTS].set(router_e_w[l])
        rb = jnp.zeros((1, LANES), F32).at[0, :N_GROUPS].set(router_g_b[l])
        rb = rb.at[0, N_GROUPS:N_GROUPS + N_EXPERTS].set(router_e_b[l])
        wrh, wrl = _hilo(wr)
        xa, h2, route, counts = _post(yf, yb, bonus, g, pcv, four_o, xa, seq, ctx_len, ent, w_out[l].astype(BF16),
                                      row2(gn_w[l]), row2(gn_b[l]), conv_w[l], row2(conv_gain[l]), g1, sc2, sh2,
                                      row2(norm2_g[l]), wrh, wrl, rb, ones_bd)

        dest3, block_e, n_used, cap = _dispatch(route, counts)
        ye = _experts(_scatter_rows(h2, dest3, cap), block_e, n_used, exp_gate[l], exp_up[l], exp_down[l])
        if last:
            blk = lambda i: (i // lat_bpb) * bpb + 1 + i % lat_bpb
            xa = _combine(xa, ye, route, dest3, g2, row2(final_g), blk, ent, bsz * lat_bpb, True)
        else:
            xa = _combine(xa, ye, route, dest3, g2, row2(final_g), lambda i: i, ent, n // ROW_BLOCK, False)

    return xa.reshape(bsz, seq, d)
```

```python
import functools

import numpy as np
import jax
import jax.numpy as jnp
from jax import lax
from jax.experimental import pallas as pl
from jax.experimental.pallas import tpu as pltpu

F32 = jnp.float32
BF16 = jnp.bfloat16

D_MODEL = 1024
HEAD = 64
D_RWKV = 512
H_RWKV = D_RWKV // HEAD
D_CONV = 256
D_FOUR = 256
FOUR_GROUP = 64
D_Z = 3 * D_RWKV + 2 * 64 + 2 * 64 + 128
D_IN = D_Z + 3 * D_CONV + D_FOUR
GRID_W = 64
N_GROUPS = 4
EXPERTS_PER_GROUP = 8
N_EXPERTS = N_GROUPS * EXPERTS_PER_GROUP
D_EXPERT = D_MODEL // 2
RMS_EPS = 1e-6
GN_EPS = 64e-5

CHUNK = 64
ROW_BLOCK = 256
MOE_ROWS = 256
FOUR_INNER = 128
LANES = 128
VMEM_LIMIT = 48 * 1024 * 1024

NN = (((1,), (0,)), ((), ()))
NT = (((1,), (1,)), ((), ()))
TN = (((0,), (0,)), ((), ()))


def _params(*sem, **kw):
    return pltpu.CompilerParams(dimension_semantics=sem, vmem_limit_bytes=VMEM_LIMIT, **kw)


def _split2(x):
    hi = x.astype(BF16)
    lo = (x - hi.astype(F32)).astype(BF16)
    return hi, lo


def _dot(a, b, dims=NN):
    return lax.dot_general(a, b, dims, preferred_element_type=F32)


def _dot1(a, b, dims=NN):
    return _dot(a.astype(BF16), b.astype(BF16), dims)


def _dot3(a, b, dims=NN):
    ah, al = _split2(a)
    bh, bl = _split2(b)
    return _dot(ah, bh, dims) + (_dot(ah, bl, dims) + _dot(al, bh, dims))


def _dot3c(ch, cl, x, dims=NN):
    xh, xl = _split2(x)
    return _dot(ch, xh, dims) + (_dot(cl, xh, dims) + _dot(ch, xl, dims))


def _dot3r(x, ch, cl, dims=NN):
    xh, xl = _split2(x)
    return _dot(xh, ch, dims) + (_dot(xl, ch, dims) + _dot(xh, cl, dims))


def _headsum(x, ones_bd):
    hi, lo = _split2(x)
    return _dot(hi, ones_bd) + _dot(lo, ones_bd)


def _rms(x, eps=RMS_EPS):
    return x * lax.rsqrt(jnp.mean(x * x, axis=-1, keepdims=True) + eps)


def _sigmoid(x):
    return 1.0 / (1.0 + jnp.exp(-x))


def _softplus(x):
    return jnp.maximum(x, 0.0) + jnp.log(1.0 + jnp.exp(-jnp.abs(x)))


def _inproj_body(x_ref, g_ref, sc_ref, sh_ref, w_ref, z_ref, cv_ref, fo_ref):
    h = _rms(x_ref[...]) * g_ref[...] * (1.0 + sc_ref[0]) + sh_ref[0]
    p = _dot(h.astype(BF16), w_ref[...])
    z_ref[...] = p[:, :D_Z]
    cv_ref[...] = p[:, D_Z:D_Z + 3 * D_CONV]
    fo_ref[...] = p[:, D_Z + 3 * D_CONV:]


def _inproj(xa, gain, sc, sh, w_bf, ent):
    n = xa.shape[0]
    row = lambda w: pl.BlockSpec((ROW_BLOCK, w), lambda i: (i, 0))
    full = lambda a: pl.BlockSpec(a.shape, lambda i: (0,) * a.ndim)
    mod = pl.BlockSpec((1, 1, D_MODEL), lambda i: (ent(i), 0, 0))
    return pl.pallas_call(
        _inproj_body,
        grid=(n // ROW_BLOCK,),
        in_specs=[row(D_MODEL), full(gain), mod, mod, full(w_bf)],
        out_specs=[row(D_Z), row(3 * D_CONV), row(D_FOUR)],
        out_shape=[jax.ShapeDtypeStruct((n, D_Z), F32), jax.ShapeDtypeStruct((n, 3 * D_CONV), F32),
                   jax.ShapeDtypeStruct((n, D_FOUR), F32)],
        compiler_params=_params("parallel"),
        name="inproj",
    )(xa, gain, sc, sh, w_bf)


def _streams_body(blocks_per_batch, zm_ref, zp_ref, zn_ref, mu_ref, w2_ref, a2_ref, g2_ref,
                  w0_ref, a0_ref, kkw_ref, ka_ref, rk_ref, ones_ref,
                  r_out, v_out, kk_out, g_out, bon_out, lw_out, kd_out, b_out):
    i = pl.program_id(0)
    seq_pos = i % blocks_per_batch
    ctx_i = (seq_pos == 0).astype(jnp.int32)
    z = zm_ref[...]
    tb = z.shape[0]
    t = lax.broadcasted_iota(jnp.int32, (tb, 1), 0)
    c = lax.broadcasted_iota(jnp.int32, (1, D_Z), 1)
    ctx_v = jnp.zeros((tb, 1), jnp.int32) + ctx_i
    col = t & (GRID_W - 1)
    lmask = (col != 0) | ((ctx_v != 0) & (t != 0))
    rmask = (col != GRID_W - 1) | ((ctx_v != 0) & (t != tb - 1))
    top_v = jnp.zeros((tb, 1), jnp.int32) + (seq_pos == 1).astype(jnp.int32)
    bot_v = jnp.zeros((tb, 1), jnp.int32) + (seq_pos == blocks_per_batch - 1).astype(jnp.int32)
    umask = jnp.logical_not((top_v != 0) & (t < GRID_W))
    dmask = jnp.logical_not((bot_v != 0) & (t >= tb - GRID_W))
    left = jnp.where(lmask, pltpu.roll(z, 1, axis=0), 0.0)
    right = jnp.where(rmask, pltpu.roll(z, tb - 1, axis=0), 0.0)
    up = jnp.where(umask, jnp.concatenate([zp_ref[...], z[:tb - GRID_W]], axis=0), 0.0)
    down = jnp.where(dmask, jnp.concatenate([z[GRID_W:], zn_ref[...]], axis=0), 0.0)
    q = c & jnp.where(ctx_i != 0, 1, 3)
    shifted = jnp.where(q == 0, left, jnp.where(q == 1, right, jnp.where(q == 2, up, down)))
    z = z + (shifted - z) * mu_ref[...]

    r = z[:, 0:D_RWKV]
    k = z[:, D_RWKV:2 * D_RWKV]
    v = z[:, 2 * D_RWKV:3 * D_RWKV]
    lw_in = z[:, 3 * D_RWKV:3 * D_RWKV + 128]
    la_in = z[:, 3 * D_RWKV + 128:3 * D_RWKV + 256]
    lg = z[:, 3 * D_RWKV + 256:]
    ones_bd = ones_ref[...]

    g_out[...] = _dot3(_sigmoid(lg), g2_ref[...])
    kq = k * kkw_ref[...]
    kk = kq / jnp.maximum(jnp.sqrt(_headsum(kq * kq, ones_bd)), 1e-12)
    w_log = -_softplus(-(w0_ref[...] + _dot3(jnp.tanh(lw_in), w2_ref[...]))) - 0.5
    lw = -jnp.exp(w_log)
    a = _sigmoid(a0_ref[...] + _dot3(la_in, a2_ref[...]))
    ka = ka_ref[...]
    ksum = jnp.zeros_like(k)
    for d in range(2):
        a_d = a[:, d * D_RWKV:(d + 1) * D_RWKV]
        k_d = k * (1.0 + (a_d - 1.0) * ka)
        ksum = ksum + k_d
        lw_out[d] = lw[:, d * D_RWKV:(d + 1) * D_RWKV]
        kd_out[d] = k_d
        b_out[d] = kk * a_d
    r_out[...] = r
    v_out[...] = v
    kk_out[...] = kk
    bon_out[...] = _headsum(r * ksum * rk_ref[...], ones_bd) * v


def _streams(pz, seq, ctx_len, bsz, mu, w2bd, a2bd, g2, w0, a0, kkw, ka, rk, ones_bd):
    n = pz.shape[0]
    assert ctx_len == ROW_BLOCK and seq % ROW_BLOCK == 0 and n == bsz * (ctx_len + seq)
    sub = ROW_BLOCK // GRID_W
    last = n // GRID_W - 1
    full = lambda a: pl.BlockSpec(a.shape, lambda i: (0,) * a.ndim)
    row = pl.BlockSpec((ROW_BLOCK, D_RWKV), lambda i: (i, 0))
    row2 = pl.BlockSpec((2, ROW_BLOCK, D_RWKV), lambda i: (0, i, 0))
    s1 = jax.ShapeDtypeStruct((n, D_RWKV), F32)
    s2 = jax.ShapeDtypeStruct((2, n, D_RWKV), F32)
    consts = (mu, w2bd, a2bd, g2, w0, a0, kkw, ka, rk, ones_bd)
    return pl.pallas_call(
        functools.partial(_streams_body, (ctx_len + seq) // ROW_BLOCK),
        grid=(n // ROW_BLOCK,),
        in_specs=[pl.BlockSpec((ROW_BLOCK, D_Z), lambda i: (i, 0)),
                  pl.BlockSpec((GRID_W, D_Z), lambda i: (jnp.maximum(i * sub - 1, 0), 0)),
                  pl.BlockSpec((GRID_W, D_Z), lambda i: (jnp.minimum((i + 1) * sub, last), 0))]
                 + [full(a) for a in consts],
        out_specs=[row] * 5 + [row2] * 3,
        out_shape=[s1] * 5 + [s2] * 3,
        compiler_params=_params("parallel"),
        name="streams",
    )(pz, pz, pz, *consts)


def _chunk_body(r_ref, v_ref, kk_ref, lw_ref, kd_ref, b_ref, p_out, q_out, rp_out, y0_out):
    d = pl.program_id(0)
    cs = CHUNK
    rows = r_ref.shape[0]
    n_chunks = rows // cs
    n_pairs = D_RWKV // LANES
    sgn = jnp.where(d == 0, 1, -1)

    ri = lax.broadcasted_iota(jnp.int32, (rows, rows), 0)
    rj = lax.broadcasted_iota(jnp.int32, (rows, rows), 1)
    same_chunk = (ri >> 6) == (rj >> 6)
    t_incl = (same_chunk & ((ri - rj) * sgn >= 0)).astype(F32).astype(BF16)
    t_all = same_chunk.astype(F32).astype(BF16)
    lw = lw_ref[0]
    l1 = lw.astype(BF16)
    rem = lw - l1.astype(F32)
    l2 = rem.astype(BF16)
    l3 = (rem - l2.astype(F32)).astype(BF16)
    gcum = _dot(t_incl, l1) + (_dot(t_incl, l2) + _dot(t_incl, l3))
    gtot = _dot(t_all, l1) + (_dot(t_all, l2) + _dot(t_all, l3))
    e_pos = jnp.exp(gcum)
    e_neg = jnp.exp(-gcum)
    e_rem = jnp.exp(gtot - gcum)
    e_prev = jnp.exp(gcum - lw)
    e_tot = jnp.exp(gtot)

    r = r_ref[...]
    v = v_ref[...]
    kk = kk_ref[...]
    kd = kd_ref[0]
    b = b_ref[0]
    at = -(kk * e_prev)
    bt = b * e_neg
    kt = kd * e_neg
    rt = r * e_pos
    bh = b * e_rem
    kh = kd * e_rem

    ti = lax.broadcasted_iota(jnp.int32, (cs, LANES), 0)
    lane = lax.broadcasted_iota(jnp.int32, (cs, LANES), 1)
    tj = lane & (HEAD - 1)
    lo_half = lane < HEAD
    order = (ti - tj) * sgn
    strict = order > 0
    incl = order >= 0
    eye = ti == tj
    eye_f = eye.astype(F32)
    zero_bf = jnp.zeros((cs, LANES), BF16)

    def bd(y):
        y = y.astype(BF16)
        return jnp.concatenate([jnp.where(lo_half, y, zero_bf), jnp.where(lo_half, zero_bf, y)], axis=0)

    def mm(x, y_bd, dims=NN):
        return _dot(x.astype(BF16), y_bd, dims)

    chains = [(c, p) for c in range(n_chunks) for p in range(n_pairs)]
    tile = lambda a, c, p: a[c * cs:(c + 1) * cs, p * LANES:(p + 1) * LANES]

    s = [mm(jnp.concatenate([tile(at, c, p), tile(rt, c, p)], axis=0),
            jnp.concatenate([bd(tile(bt, c, p)), bd(tile(kt, c, p))], axis=0), NT) for c, p in chains]
    a_ab = [jnp.where(strict, x[:cs, :LANES], 0.0) for x in s]
    a_ak = [jnp.where(strict, x[:cs, LANES:], 0.0) for x in s]
    a_rbk = [jnp.concatenate([jnp.where(incl, x[cs:, :LANES], 0.0), jnp.where(incl, x[cs:, LANES:], 0.0)], axis=1)
             for x in s]
    a0 = [jnp.where((ti >> 3) == (tj >> 3), x, 0.0) for x in a_ab]
    a2 = [mm(x, bd(x)) for x in a0]
    a2_bd = [bd(x) for x in a2]
    x0 = [eye_f + x for x in a0]
    x1 = [x + mm(x, y) for x, y in zip(x0, a2_bd)]
    a4 = [mm(x, y) for x, y in zip(a2, a2_bd)]
    minv = [x + mm(x, bd(y)) for x, y in zip(x1, a4)]
    lvl = 3
    while (1 << lvl) < cs:
        off = ((ti >> (lvl + 1)) == (tj >> (lvl + 1))) & ((ti >> lvl) != (tj >> lvl))
        t = [mm(jnp.where(off, x, 0.0), bd(y)) for x, y in zip(a_ab, minv)]
        minv = [x + mm(x, bd(y)) for x, y in zip(minv, t)]
        lvl += 1
    v_bd = [bd(tile(v, c, p)) for c, p in chains]
    av = [mm(x, y) for x, y in zip(a_ak, v_bd)]
    wu = [mm(m, jnp.concatenate([bd(tile(at, c, p)), bd(y)], axis=1)) for m, y, (c, p) in zip(minv, av, chains)]
    top = [mm(jnp.concatenate([tile(bh, c, p), tile(kh, c, p)], axis=0),
              jnp.concatenate([x.astype(BF16), jnp.concatenate([zero_bf, tile(v, c, p).astype(BF16)], axis=1)],
                              axis=0), TN) for x, (c, p) in zip(wu, chains)]
    bot = [mm(x, jnp.concatenate([jnp.concatenate([bd(y[:, :LANES]), bd(y[:, LANES:])], axis=1),
                                  jnp.concatenate([jnp.zeros((2 * cs, LANES), BF16), vb], axis=1)], axis=0))
           for x, y, vb in zip(a_rbk, wu, v_bd)]
    for c in range(n_chunks):
        sel = lambda x, off: jnp.where(lo_half, x[:cs, off:off + LANES], x[cs:, off:off + LANES])
        idx = [c * n_pairs + p for p in range(n_pairs)]
        p_out[0, c] = jnp.concatenate(
            [sel(top[i], 0) + jnp.where(eye, tile(e_tot, c, p), 0.0) for p, i in enumerate(idx)], axis=1)
        q_out[0, c] = jnp.concatenate([sel(top[i], LANES) for i in idx], axis=1)
        rp_out[0, c * cs:(c + 1) * cs, :] = jnp.concatenate(
            [tile(rt, c, p) + bot[i][:, :LANES] for p, i in enumerate(idx)], axis=1)
        y0_out[0, c * cs:(c + 1) * cs, :] = jnp.concatenate([bot[i][:, LANES:] for i in idx], axis=1)


CHUNKS_PER_STEP = 4


def _chunk_local(r, v, kk, lw, kd, b):
    n = r.shape[0]
    nc = n // CHUNK
    rows = CHUNKS_PER_STEP * CHUNK
    shared = pl.BlockSpec((rows, D_RWKV), lambda d, j: (j, 0))
    per_dir = pl.BlockSpec((1, rows, D_RWKV), lambda d, j: (d, j, 0))
    mat = pl.BlockSpec((1, CHUNKS_PER_STEP, HEAD, D_RWKV), lambda d, j: (d, j, 0, 0))
    sm = jax.ShapeDtypeStruct((2, nc, HEAD, D_RWKV), F32)
    st = jax.ShapeDtypeStruct((2, n, D_RWKV), F32)
    return pl.pallas_call(
        _chunk_body,
        grid=(2, nc // CHUNKS_PER_STEP),
        in_specs=[shared, shared, shared, per_dir, per_dir, per_dir],
        out_specs=[mat, mat, per_dir, per_dir],
        out_shape=[sm, sm, st, st],
        compiler_params=_params("parallel", "parallel"),
        name="chunk_local",
    )(r, v, kk, lw, kd, b)


SCAN_CHUNKS = 4


def _scan_body(pf_ref, qf_ref, rpf_ref, y0f_ref, pb_ref, qb_ref, rpb_ref, y0b_ref, yf_out, yb_out, h_scr):
    @pl.when(pl.program_id(0) == 0)
    def _():
        h_scr[...] = jnp.zeros_like(h_scr)

    cs = CHUNK
    bsz = h_scr.shape[1]
    n_pairs = D_RWKV // LANES
    lane = lax.broadcasted_iota(jnp.int32, (cs, LANES), 1)
    lo_half = lane < HEAD
    zero_bf = jnp.zeros((cs, LANES), BF16)

    def bd(y):
        return jnp.concatenate([jnp.where(lo_half, y, zero_bf), jnp.where(lo_half, zero_bf, y)], axis=0)

    dirs = ((pf_ref, qf_ref, rpf_ref, y0f_ref, yf_out), (pb_ref, qb_ref, rpb_ref, y0b_ref, yb_out))
    chains = [(d, b, p) for d in range(2) for b in range(bsz) for p in range(n_pairs)]
    state = [h_scr[d, b, :, p * LANES:(p + 1) * LANES] for d, b, p in chains]
    for s in range(SCAN_CHUNKS):
        outs = []
        for (d, b, p), hcur in zip(chains, state):
            p_ref, _, rp_ref, _, _ = dirs[d]
            c = s if d == 0 else SCAN_CHUNKS - 1 - s
            ls = slice(p * LANES, (p + 1) * LANES)
            x = jnp.concatenate([p_ref[0, b, c, :, ls], rp_ref[0, b, c * cs:(c + 1) * cs, ls]], axis=0)
            xh, xl = _split2(x)
            hh, hl = _split2(hcur)
            hh, hl = bd(hh), bd(hl)
            outs.append(_dot(xh, hh) + (_dot(xh, hl) + _dot(xl, hh)))
        new_state = []
        for (d, b, p), o in zip(chains, outs):
            _, q_ref, _, y0_ref, y_out = dirs[d]
            c = s if d == 0 else SCAN_CHUNKS - 1 - s
            ls = slice(p * LANES, (p + 1) * LANES)
            new_state.append(o[:cs] + q_ref[0, b, c, :, ls])
            y_out[b, c * cs:(c + 1) * cs, ls] = y0_ref[0, b, c * cs:(c + 1) * cs, ls] + o[cs:]
        state = new_state
    for (d, b, p), hcur in zip(chains, state):
        h_scr[d, b, :, p * LANES:(p + 1) * LANES] = hcur


def _chunk_scan(p, q, rp, y0, seq, ctx_len, bsz):
    n = rp.shape[1]
    rows_b = ctx_len + seq
    ncb = rows_b // CHUNK
    assert ctx_len % (SCAN_CHUNKS * CHUNK) == 0 and seq % (SCAN_CHUNKS * CHUNK) == 0
    steps = ncb // SCAN_CHUNKS
    ctx_steps = ctx_len // (SCAN_CHUNKS * CHUNK)
    p5 = p.reshape(2, bsz, ncb, HEAD, D_RWKV)
    q5 = q.reshape(2, bsz, ncb, HEAD, D_RWKV)
    rp4 = rp.reshape(2, bsz, rows_b, D_RWKV)
    y04 = y0.reshape(2, bsz, rows_b, D_RWKV)
    pos_f = lambda i: i
    pos_b = lambda i: jnp.where(i < ctx_steps, ctx_steps - 1 - i, steps - 1 - (i - ctx_steps))
    rows = SCAN_CHUNKS * CHUNK

    def specs(d, pos):
        mat = pl.BlockSpec((1, bsz, SCAN_CHUNKS, HEAD, D_RWKV), lambda i: (d, 0, pos(i), 0, 0))
        tok = pl.BlockSpec((1, bsz, rows, D_RWKV), lambda i: (d, 0, pos(i), 0))
        return [mat, mat, tok, tok]

    out_f = pl.BlockSpec((bsz, rows, D_RWKV), lambda i: (0, pos_f(i), 0))
    out_b = pl.BlockSpec((bsz, rows, D_RWKV), lambda i: (0, pos_b(i), 0))
    shp = jax.ShapeDtypeStruct((bsz, rows_b, D_RWKV), F32)
    yf, yb = pl.pallas_call(
        _scan_body,
        grid=(steps,),
        in_specs=specs(0, pos_f) + specs(1, pos_b),
        out_specs=[out_f, out_b],
        out_shape=[shp, shp],
        scratch_shapes=[pltpu.VMEM((2, bsz, HEAD, D_RWKV), F32)],
        compiler_params=_params("arbitrary"),
        name="chunk_scan",
    )(p5, q5, rp4, y04, p5, q5, rp4, y04)
    return yf.reshape(n, D_RWKV), yb.reshape(n, D_RWKV)


def _hilo(a):
    if isinstance(a, np.ndarray):
        a = a.astype(np.float32)
        hi = a.astype(BF16)
        return jnp.asarray(hi), jnp.asarray((a - hi.astype(np.float32)).astype(BF16))
    hi = a.astype(BF16)
    return hi, (a - hi.astype(F32)).astype(BF16)


def _channel_tables(length):
    j = np.arange(FOUR_GROUP)
    ang = 2.0 * np.pi * np.outer(j, j) / FOUR_GROUP
    scale = 1.0 / np.sqrt(float(length) * FOUR_GROUP)
    groups = D_FOUR // FOUR_GROUP
    c4 = np.kron(np.eye(groups), np.cos(ang)) * scale
    s4 = np.kron(np.eye(groups), np.sin(ang)) * scale
    return _hilo(c4) + _hilo(s4)


def _four_finish(fr, fi, c4h, c4l, s4h, s4l, gain):
    y = _dot3r(fr, c4h, c4l) + _dot3r(fi, s4h, s4l)
    return _rms(y) * gain


def _four_direct_body(f_ref, mh_ref, ml_ref, c4h, c4l, s4h, s4l, gain_ref, o_ref):
    length = f_ref.shape[1]
    fc = _dot3c(mh_ref[...], ml_ref[...], f_ref[0])
    o_ref[0] = _four_finish(fc[:length], fc[length:], c4h[...], c4l[...], s4h[...], s4l[...], gain_ref[...])


def _fourier_direct(f, gain):
    bsz, length, _ = f.shape
    t = np.arange(length)
    ang = 2.0 * np.pi * (np.outer(t, t) % length) / length
    mh, ml = _hilo(np.concatenate([np.cos(ang), -np.sin(ang)], axis=0))
    consts = (mh, ml) + _channel_tables(length) + (gain,)
    full = lambda a: pl.BlockSpec(a.shape, lambda b: (0,) * a.ndim)
    blk = pl.BlockSpec((1, length, D_FOUR), lambda b: (b, 0, 0))
    return pl.pallas_call(
        _four_direct_body,
        grid=(bsz,),
        in_specs=[blk] + [full(a) for a in consts],
        out_specs=blk,
        out_shape=jax.ShapeDtypeStruct(f.shape, F32),
        compiler_params=_params("parallel"),
        name="fourier_direct",
    )(f, *consts)


def _four_stage1_body(f_ref, mh_ref, ml_ref, o_ref):
    o_ref[0] = _dot3c(mh_ref[...], ml_ref[...], f_ref[0])


def _four_stage2_body(zr_ref, zi_ref, mh_ref, ml_ref, c4h, c4l, s4h, s4l, gain_ref, o_ref):
    inner = zr_ref.shape[2]
    zz = jnp.concatenate([zr_ref[0, 0], zi_ref[0, 0]], axis=0)
    fc = _dot3c(mh_ref[0], ml_ref[0], zz)
    o_ref[0] = _four_finish(fc[:inner], fc[inner:], c4h[...], c4l[...], s4h[...], s4l[...], gain_ref[...])


def _fourier_long(f, gain):
    bsz, length, _ = f.shape
    l2 = FOUR_INNER
    l1 = length // l2
    assert l1 * l2 == length and l1 % 16 == 0
    cols = l2 * D_FOUR
    col_tile = 4096
    th = np.arange(l1)
    ang1 = 2.0 * np.pi * (np.outer(th, th) % l1) / l1
    m1h, m1l = _hilo(np.concatenate([np.cos(ang1), -np.sin(ang1)], axis=0))
    z = pl.pallas_call(
        _four_stage1_body,
        grid=(bsz, cols // col_tile),
        in_specs=[pl.BlockSpec((1, l1, col_tile), lambda b, c: (b, 0, c)),
                  pl.BlockSpec(m1h.shape, lambda b, c: (0, 0)),
                  pl.BlockSpec(m1l.shape, lambda b, c: (0, 0))],
        out_specs=pl.BlockSpec((1, 2 * l1, col_tile), lambda b, c: (b, 0, c)),
        out_shape=jax.ShapeDtypeStruct((bsz, 2 * l1, cols), F32),
        compiler_params=_params("parallel", "parallel"),
        name="fourier_stage1",
    )(f.reshape(bsz, l1, cols), m1h, m1l)
    z = z.reshape(bsz, 2 * l1, l2, D_FOUR)

    ma = np.arange(l1)[:, None, None]
    mb = np.arange(l2)[None, :, None]
    tl = np.arange(l2)[None, None, :]
    ang2 = 2.0 * np.pi * (((ma + l1 * mb) * tl) % length) / length
    cos2, sin2 = np.cos(ang2), np.sin(ang2)
    m2 = np.concatenate([np.concatenate([cos2, sin2], axis=2), np.concatenate([-sin2, cos2], axis=2)], axis=1)
    m2h, m2l = _hilo(m2)
    consts = _channel_tables(length) + (gain,)
    full = lambda a: pl.BlockSpec(a.shape, lambda b, m: (0,) * a.ndim)
    tab = pl.BlockSpec((1, 2 * l2, 2 * l2), lambda b, m: (m, 0, 0))
    out = pl.pallas_call(
        _four_stage2_body,
        grid=(bsz, l1),
        in_specs=[pl.BlockSpec((1, 1, l2, D_FOUR), lambda b, m: (b, m, 0, 0)),
                  pl.BlockSpec((1, 1, l2, D_FOUR), lambda b, m: (b, l1 + m, 0, 0)),
                  tab, tab] + [full(a) for a in consts],
        out_specs=pl.BlockSpec((1, l2, D_FOUR), lambda b, m: (b, 0, m)),
        out_shape=jax.ShapeDtypeStruct((bsz, l2, l1 * D_FOUR), F32),
        compiler_params=_params("parallel", "parallel"),
        name="fourier_stage2",
    )(z, z, m2h, m2l, *consts)
    return out.reshape(bsz, length, D_FOUR)


def _route(logits):
    lane = lax.broadcasted_iota(jnp.int32, (1, LANES), 1)
    lane_f = lane.astype(F32)
    neg = jnp.float32(-1e30)
    big = jnp.float32(1e9)
    gl = jnp.where(lane < N_GROUPS, logits, neg)
    gmax = jnp.max(gl, axis=-1, keepdims=True)
    pg_top = 1.0 / jnp.sum(jnp.exp(gl - gmax), axis=-1, keepdims=True)
    grp = jnp.min(jnp.where(gl == gmax, lane_f, big), axis=-1, keepdims=True)
    e_lane = lane - N_GROUPS
    in_grp = (e_lane >= 0) & (e_lane < N_EXPERTS) & ((e_lane >> 3).astype(F32) == grp)
    el = jnp.where(in_grp, logits, neg)
    m1 = jnp.max(el, axis=-1, keepdims=True)
    i1 = jnp.min(jnp.where(el == m1, lane_f, big), axis=-1, keepdims=True)
    el2 = jnp.where(lane_f == i1, neg, el)
    m2 = jnp.max(el2, axis=-1, keepdims=True)
    i2 = jnp.min(jnp.where(el2 == m2, lane_f, big), axis=-1, keepdims=True)
    e2 = jnp.exp(m2 - m1)
    den = 1.0 + e2
    return pg_top / den, pg_top * e2 / den, i1 - N_GROUPS, i2 - N_GROUPS


def _post_body(blocks_per_batch, yf_ref, yb_ref, bon_ref, g_ref, cv_ref, cvp_ref, cvn_ref, fo_ref, x_ref,
               wo_ref, gnw_ref, gnb_ref, cw_ref, cg_ref, g1_ref, sc2_ref, sh2_ref, n2_ref, wrh_ref, wrl_ref,
               rb_ref, ones_ref, x_out, h_out, route_out, count_out, count_scr):
    i = pl.program_id(0)

    @pl.when(i == 0)
    def _():
        count_scr[...] = jnp.zeros_like(count_scr)

    ones_bd = ones_ref[...]
    y = yf_ref[...] + yb_ref[...]
    mu = _headsum(y, ones_bd) * (1.0 / HEAD)
    yc = y - mu
    var = _headsum(yc * yc, ones_bd) * (1.0 / HEAD)
    yn = yc * lax.rsqrt(var + GN_EPS) * gnw_ref[...] + gnb_ref[...]
    o_rwkv = (yn + bon_ref[...]) * g_ref[...]

    cv = cv_ref[...]
    tb = cv.shape[0]
    t = lax.broadcasted_iota(jnp.int32, (tb, 1), 0)
    seq_pos = i % blocks_per_batch
    first = (seq_pos <= 1).astype(F32)
    final = jnp.logical_or(seq_pos == 0, seq_pos == blocks_per_batch - 1).astype(F32)
    zc = cv[:, D_CONV:2 * D_CONV] * cv[:, 2 * D_CONV:]
    zp_row = cvp_ref[7:8, D_CONV:2 * D_CONV] * cvp_ref[7:8, 2 * D_CONV:] * (1.0 - first)
    zn_row = cvn_ref[0:1, D_CONV:2 * D_CONV] * cvn_ref[0:1, 2 * D_CONV:] * (1.0 - final)
    prev = jnp.where(t == 0, zp_row, pltpu.roll(zc, 1, axis=0))
    nxt = jnp.where(t == tb - 1, zn_row, pltpu.roll(zc, tb - 1, axis=0))
    cw = cw_ref[...]
    conv = cv[:, :D_CONV] * (cw[0:1] * prev + cw[1:2] * zc + cw[2:3] * nxt)
    conv_o = _rms(conv) * cg_ref[...]

    mix = (_dot(o_rwkv.astype(BF16), wo_ref[0:D_RWKV, :])
           + _dot(conv_o.astype(BF16), wo_ref[D_RWKV:D_RWKV + D_CONV, :])
           + _dot(fo_ref[...].astype(BF16), wo_ref[D_RWKV + D_CONV:, :]))
    x = x_ref[...] + g1_ref[0] * mix
    x_out[...] = x
    h2 = _rms(x) * n2_ref[...] * (1.0 + sc2_ref[0]) + sh2_ref[0]
    h_out[...] = h2

    gate0, gate1, e0, e1 = _route(_dot3r(h2, wrh_ref[...], wrl_ref[...]) + rb_ref[...])
    lane = lax.broadcasted_iota(jnp.int32, (1, LANES), 1)
    lane_f = lane.astype(F32)
    oh0 = (lane_f == e0).astype(F32)
    oh1 = (lane_f == e1).astype(F32)
    ri = lax.broadcasted_iota(jnp.int32, (tb, tb), 0)
    rj = lax.broadcasted_iota(jnp.int32, (tb, tb), 1)
    earlier = (rj < ri).astype(F32).astype(BF16)
    seen = count_scr[...]
    tot0 = jnp.sum(oh0, axis=0, keepdims=True)
    before0 = _dot(earlier, oh0.astype(BF16)) + seen
    before1 = _dot(earlier, oh1.astype(BF16)) + (seen + tot0)
    rank0 = jnp.sum(oh0 * before0, axis=-1, keepdims=True)
    rank1 = jnp.sum(oh1 * before1, axis=-1, keepdims=True)
    seen = seen + tot0 + jnp.sum(oh1, axis=0, keepdims=True)
    count_scr[...] = seen
    count_out[...] = seen
    route_out[...] = jnp.where(
        lane == 0, gate0, jnp.where(lane == 1, gate1, jnp.where(lane == 2, e0, jnp.where(
            lane == 3, e1, jnp.where(lane == 4, rank0, jnp.where(lane == 5, rank1, 0.0))))))


def _post(yf, yb, bonus, g, pcv, four_o, xa, seq, ctx_len, ent, wo_bf, gnw, gnb, cw, cg, g1, sc2, sh2, n2,
          wrh, wrl, rb, ones_bd):
    n = xa.shape[0]
    assert ctx_len == ROW_BLOCK
    sub = ROW_BLOCK // 8
    last = n // 8 - 1
    full = lambda a: pl.BlockSpec(a.shape, lambda i: (0,) * a.ndim)
    row = lambda w: pl.BlockSpec((ROW_BLOCK, w), lambda i: (i, 0))
    mod = pl.BlockSpec((1, 1, D_MODEL), lambda i: (ent(i), 0, 0))
    return pl.pallas_call(
        functools.partial(_post_body, (ctx_len + seq) // ROW_BLOCK),
        grid=(n // ROW_BLOCK,),
        in_specs=[row(D_RWKV), row(D_RWKV), row(D_RWKV), row(D_RWKV), row(3 * D_CONV),
                  pl.BlockSpec((8, 3 * D_CONV), lambda i: (jnp.maximum(i * sub - 1, 0), 0)),
                  pl.BlockSpec((8, 3 * D_CONV), lambda i: (jnp.minimum((i + 1) * sub, last), 0)),
                  row(D_FOUR), row(D_MODEL), full(wo_bf), full(gnw), full(gnb), full(cw), full(cg),
                  mod, mod, mod, full(n2), full(wrh), full(wrl), full(rb), full(ones_bd)],
        out_specs=[row(D_MODEL), row(D_MODEL), row(LANES), pl.BlockSpec((1, LANES), lambda i: (0, 0))],
        out_shape=[jax.ShapeDtypeStruct((n, D_MODEL), F32), jax.ShapeDtypeStruct((n, D_MODEL), F32),
                   jax.ShapeDtypeStruct((n, LANES), F32), jax.ShapeDtypeStruct((1, LANES), F32)],
        scratch_shapes=[pltpu.VMEM((1, LANES), F32)],
        compiler_params=_params("arbitrary"),
        name="post_mix",
    )(yf, yb, bonus, g, pcv, pcv, pcv, four_o, xa, wo_bf, gnw, gnb, cw, cg, g1, sc2, sh2, n2, wrh, wrl, rb,
      ones_bd)


def _scatter_rows_body(dest_ref, h_ref, xb_in_ref, xb_ref, sem):
    del xb_in_ref
    tb = h_ref.shape[0]

    def row_copy(t, k):
        return pltpu.make_async_copy(h_ref.at[pl.ds(t, 1), :],
                                     xb_ref.at[pl.ds(dest_ref[0, 0, 2 * t + k], 1), :], sem)

    def issue(t, carry):
        row_copy(t, 0).start()
        row_copy(t, 1).start()
        return carry

    def drain(t, carry):
        row_copy(t, 0).wait()
        row_copy(t, 1).wait()
        return carry

    lax.fori_loop(0, tb, issue, 0, unroll=DMA_UNROLL)
    lax.fori_loop(0, tb, drain, 0, unroll=DMA_UNROLL)


DMA_UNROLL = 8
SCATTER_ROWS = 512


def _scatter_rows(h2, dest, cap):
    n = h2.shape[0]
    steps = n // SCATTER_ROWS
    return pl.pallas_call(
        _scatter_rows_body,
        grid=(steps,),
        in_specs=[pl.BlockSpec((1, 1, 2 * SCATTER_ROWS), lambda i: (i, 0, 0), memory_space=pltpu.SMEM),
                  pl.BlockSpec((SCATTER_ROWS, D_MODEL), lambda i: (i, 0)),
                  pl.BlockSpec(memory_space=pl.ANY)],
        out_specs=pl.BlockSpec(memory_space=pl.ANY),
        out_shape=jax.ShapeDtypeStruct((cap, D_MODEL), F32),
        scratch_shapes=[pltpu.SemaphoreType.DMA(())],
        input_output_aliases={2: 0},
        compiler_params=_params("arbitrary", disable_bounds_checks=True),
        name="moe_scatter",
    )(dest.reshape(steps, 1, 2 * SCATTER_ROWS), h2, jnp.zeros((cap, D_MODEL), F32))


def _expert_body(be_ref, nu_ref, xb_ref, wg_ref, wu_ref, wd_ref, o_ref, wg_s, wu_s, wd_s):
    i = pl.program_id(0)
    prev = be_ref[jnp.maximum(i - 1, 0)]
    fresh = jnp.logical_or(i == 0, be_ref[i] != prev)

    @pl.when(jnp.logical_and(fresh, i < nu_ref[0]))
    def _():
        wg_s[...] = wg_ref[0, 0].astype(BF16)
        wu_s[...] = wu_ref[0, 0].astype(BF16)
        wd_s[...] = wd_ref[0, 0].astype(BF16)

    @pl.when(i < nu_ref[0])
    def _():
        xb = xb_ref[...].astype(BF16)
        gate = _dot(xb, wg_s[...])
        up = _dot(xb, wu_s[...])
        act = gate * _sigmoid(gate) * up
        o_ref[...] = _dot(act.astype(BF16), wd_s[...])

    @pl.when(i >= nu_ref[0])
    def _():
        o_ref[...] = jnp.zeros_like(o_ref)


def _experts(xb, block_e, n_used, exp_gate, exp_up, exp_down, layer):
    cap = xb.shape[0]
    nb = cap // MOE_ROWS
    grid_spec = pltpu.PrefetchScalarGridSpec(
        num_scalar_prefetch=2,
        grid=(nb,),
        in_specs=[pl.BlockSpec((MOE_ROWS, D_MODEL), lambda i, be, nu: (i, 0)),
                  pl.BlockSpec((1, 1, D_MODEL, D_EXPERT), lambda i, be, nu: (layer, be[i], 0, 0)),
                  pl.BlockSpec((1, 1, D_MODEL, D_EXPERT), lambda i, be, nu: (layer, be[i], 0, 0)),
                  pl.BlockSpec((1, 1, D_EXPERT, D_MODEL), lambda i, be, nu: (layer, be[i], 0, 0))],
        out_specs=pl.BlockSpec((MOE_ROWS, D_MODEL), lambda i, be, nu: (i, 0)),
        scratch_shapes=[pltpu.VMEM((D_MODEL, D_EXPERT), BF16), pltpu.VMEM((D_MODEL, D_EXPERT), BF16),
                        pltpu.VMEM((D_EXPERT, D_MODEL), BF16)],
    )
    return pl.pallas_call(
        _expert_body,
        grid_spec=grid_spec,
        out_shape=jax.ShapeDtypeStruct((cap, D_MODEL), F32),
        compiler_params=_params("arbitrary"),
        name="experts",
    )(block_e, n_used, xb, exp_gate, exp_up, exp_down)


def _dispatch(route, counts):
    n = route.shape[0]
    eid = route[:, 2:4].astype(jnp.int32)
    rank = route[:, 4:6].astype(jnp.int32)
    counts = counts[0, :N_EXPERTS].astype(jnp.int32)
    pcounts = (counts + MOE_ROWS - 1) // MOE_ROWS * MOE_ROWS
    pend = jnp.cumsum(pcounts)
    pstart = pend - pcounts
    experts = jnp.arange(N_EXPERTS, dtype=jnp.int32)
    dest = jnp.sum(jnp.where(eid[:, :, None] == experts, pstart, 0), axis=-1) + rank
    nb = -(-2 * n // MOE_ROWS) + N_EXPERTS
    block_start = jnp.arange(nb, dtype=jnp.int32) * MOE_ROWS
    block_e = jnp.minimum(jnp.sum((pend[None, :] <= block_start[:, None]).astype(jnp.int32), axis=1),
                          N_EXPERTS - 1)
    n_used = (pend[-1] // MOE_ROWS).astype(jnp.int32).reshape(1)
    return dest.reshape(n // ROW_BLOCK, 1, 2 * ROW_BLOCK), block_e, n_used, nb * MOE_ROWS


def _combine_body(final_norm, dest_ref, x_ref, route_ref, g2_ref, fg_ref, yb_ref, o_ref, buf, sem):
    tb = x_ref.shape[0]

    def row_copy(t, k):
        return pltpu.make_async_copy(yb_ref.at[pl.ds(dest_ref[0, 0, 2 * t + k], 1), :],
                                     buf.at[k, pl.ds(t, 1), :], sem)

    def issue(t, carry):
        row_copy(t, 0).start()
        row_copy(t, 1).start()
        return carry

    def drain(t, carry):
        row_copy(t, 0).wait()
        row_copy(t, 1).wait()
        return carry

    lax.fori_loop(0, tb, issue, 0, unroll=DMA_UNROLL)
    lax.fori_loop(0, tb, drain, 0, unroll=DMA_UNROLL)
    route = route_ref[...]
    x = x_ref[...] + g2_ref[0] * (route[:, 0:1] * buf[0] + route[:, 1:2] * buf[1])
    if final_norm:
        x = _rms(x) * fg_ref[...]
    o_ref[...] = x


def _combine(xa, yb, route, dest3, g2, final_g, blk, ent, n_out_blocks, final_norm):
    return pl.pallas_call(
        functools.partial(_combine_body, final_norm),
        grid=(n_out_blocks,),
        in_specs=[pl.BlockSpec((1, 1, 2 * ROW_BLOCK), lambda i: (blk(i), 0, 0), memory_space=pltpu.SMEM),
                  pl.BlockSpec((ROW_BLOCK, D_MODEL), lambda i: (blk(i), 0)),
                  pl.BlockSpec((ROW_BLOCK, LANES), lambda i: (blk(i), 0)),
                  pl.BlockSpec((1, 1, D_MODEL), lambda i: (ent(blk(i)), 0, 0)),
                  pl.BlockSpec(final_g.shape, lambda i: (0, 0)),
                  pl.BlockSpec(memory_space=pl.ANY)],
        out_specs=pl.BlockSpec((ROW_BLOCK, D_MODEL), lambda i: (i, 0)),
        out_shape=jax.ShapeDtypeStruct((n_out_blocks * ROW_BLOCK, D_MODEL), F32),
        scratch_shapes=[pltpu.VMEM((2, ROW_BLOCK, D_MODEL), F32), pltpu.SemaphoreType.DMA(())],
        compiler_params=_params("arbitrary", disable_bounds_checks=True),
        name="moe_combine",
    )(dest3, xa, route, g2, final_g, yb)


def _block_diag2(w):
    k, n = w.shape[1], w.shape[2]
    z = jnp.zeros((k, n), w.dtype)
    return jnp.concatenate([jnp.concatenate([w[0], z], axis=1), jnp.concatenate([z, w[1]], axis=1)], axis=0)


def kernel(x, c, ctx, c_ctx, ada_w, ada_b, norm1_g, norm2_g, w_in, mu_shift, decay_w0, decay_w2, iclr_a0, iclr_a2, gate_g2, k_k, k_a, r_k, gn_w, gn_b, conv_w, conv_gain, four_gain, w_out, router_g_w, router_g_b, router_e_w, router_e_b, exp_gate, exp_up, exp_down, final_g):
    bsz, seq, d = x.shape
    ctx_len = ctx.shape[1]
    depth = ada_w.shape[0]
    rows_b = ctx_len + seq
    n = bsz * rows_b
    bpb = rows_b // ROW_BLOCK
    lat_bpb = seq // ROW_BLOCK
    ent = lambda i: jnp.where(i % bpb == 0, bsz, i // bpb)

    xa = jnp.concatenate([ctx, x], axis=1).reshape(n, d)
    cvec = jnp.concatenate([c, c_ctx[None, :]], axis=0)
    cvec = cvec * jax.nn.sigmoid(cvec)
    head_id = np.arange(D_RWKV) // HEAD
    ones_bd = jnp.asarray(head_id[:, None] == head_id[None, :], BF16)
    row2 = lambda a: a.reshape(1, -1)

    for l in range(depth):
        last = l == depth - 1
        mod = jnp.dot(cvec, ada_w[l], precision=lax.Precision.HIGHEST) + ada_b[l]
        sh1, sc1, g1, sh2, sc2, g2 = [mod[:, j * d:(j + 1) * d].reshape(bsz + 1, 1, d) for j in range(6)]

        pz, pcv, pfo = _inproj(xa, row2(norm1_g[l]), sc1, sh1, w_in[l].astype(BF16), ent)
        r, v, kk, g, bonus, lw, kd, b = _streams(
            pz, seq, ctx_len, bsz, row2(mu_shift[l]), _block_diag2(decay_w2[l]), _block_diag2(iclr_a2[l]),
            gate_g2[l], row2(decay_w0[l]), row2(iclr_a0[l]), row2(k_k[l]), row2(k_a[l]), row2(r_k[l]), ones_bd)
        pm, qm, rp, y0 = _chunk_local(r, v, kk, lw, kd, b)
        yf, yb = _chunk_scan(pm, qm, rp, y0, seq, ctx_len, bsz)

        fgain = row2(four_gain[l])
        pfo3 = pfo.reshape(bsz, rows_b, D_FOUR)
        four_o = jnp.concatenate([_fourier_direct(pfo3[:, :ctx_len], fgain),
                                  _fourier_long(pfo3[:, ctx_len:], fgain)], axis=1).reshape(n, D_FOUR)

        wr = jnp.zeros((d, LANES), F32).at[:, :N_GROUPS].set(router_g_w[l])
        wr = wr.at[:, N_GROUPS:N_GROUPS + N_EXPERTS].set(router_e_w[l])
        rb = jnp.zeros((1, LANES), F32).at[0, :N_GROUPS].set(router_g_b[l])
        rb = rb.at[0, N_GROUPS:N_GROUPS + N_EXPERTS].set(router_e_b[l])
        wrh, wrl = _hilo(wr)
        xa, h2, route, counts = _post(yf, yb, bonus, g, pcv, four_o, xa, seq, ctx_len, ent, w_out[l].astype(BF16),
                                      row2(gn_w[l]), row2(gn_b[l]), conv_w[l], row2(conv_gain[l]), g1, sc2, sh2,
                                      row2(norm2_g[l]), wrh, wrl, rb, ones_bd)

        dest3, block_e, n_used, cap = _dispatch(route, counts)
        ye = _experts(_scatter_rows(h2, dest3, cap), block_e, n_used, exp_gate, exp_up, exp_down, l)
        if last:
            blk = lambda i: (i // lat_bpb) * bpb + 1 + i % lat_bpb
            xa = _combine(xa, ye, route, dest3, g2, row2(final_g), blk, ent, bsz * lat_bpb, True)
        else:
            xa = _combine(xa, ye, route, dest3, g2, row2(final_g), lambda i: i, ent, n // ROW_BLOCK, False)

    return xa.reshape(bsz, seq, d)
```

```python
import functools

import numpy as np
import jax
import jax.numpy as jnp
from jax import lax
from jax.experimental import pallas as pl
from jax.experimental.pallas import tpu as pltpu

F32 = jnp.float32
BF16 = jnp.bfloat16

D_MODEL = 1024
HEAD = 64
D_RWKV = 512
H_RWKV = D_RWKV // HEAD
D_CONV = 256
D_FOUR = 256
FOUR_GROUP = 64
D_Z = 3 * D_RWKV + 2 * 64 + 2 * 64 + 128
D_IN = D_Z + 3 * D_CONV + D_FOUR
GRID_W = 64
N_GROUPS = 4
EXPERTS_PER_GROUP = 8
N_EXPERTS = N_GROUPS * EXPERTS_PER_GROUP
D_EXPERT = D_MODEL // 2
RMS_EPS = 1e-6
GN_EPS = 64e-5

CHUNK = 64
ROW_BLOCK = 256
MOE_ROWS = 256
FOUR_INNER = 128
FOUR_GROUP_STEP = 4
LANES = 128
VMEM_LIMIT = 48 * 1024 * 1024

NN = (((1,), (0,)), ((), ()))
NT = (((1,), (1,)), ((), ()))
TN = (((0,), (0,)), ((), ()))


def _params(*sem, **kw):
    return pltpu.CompilerParams(dimension_semantics=sem, vmem_limit_bytes=VMEM_LIMIT, **kw)


def _split2(x):
    hi = x.astype(BF16)
    lo = (x - hi.astype(F32)).astype(BF16)
    return hi, lo


def _dot(a, b, dims=NN):
    return lax.dot_general(a, b, dims, preferred_element_type=F32)


def _dot1(a, b, dims=NN):
    return _dot(a.astype(BF16), b.astype(BF16), dims)


def _dot3(a, b, dims=NN):
    ah, al = _split2(a)
    bh, bl = _split2(b)
    return _dot(ah, bh, dims) + (_dot(ah, bl, dims) + _dot(al, bh, dims))


def _dot3c(ch, cl, x, dims=NN):
    xh, xl = _split2(x)
    return _dot(ch, xh, dims) + (_dot(cl, xh, dims) + _dot(ch, xl, dims))


def _dot3r(x, ch, cl, dims=NN):
    xh, xl = _split2(x)
    return _dot(xh, ch, dims) + (_dot(xl, ch, dims) + _dot(xh, cl, dims))


def _headsum(x, ones_bd):
    hi, lo = _split2(x)
    return _dot(hi, ones_bd) + _dot(lo, ones_bd)


def _rms(x, eps=RMS_EPS):
    return x * lax.rsqrt(jnp.mean(x * x, axis=-1, keepdims=True) + eps)


def _sigmoid(x):
    return 1.0 / (1.0 + jnp.exp(-x))


def _softplus(x):
    return jnp.maximum(x, 0.0) + jnp.log(1.0 + jnp.exp(-jnp.abs(x)))


def _inproj_body(x_ref, g_ref, sc_ref, sh_ref, w_ref, z_ref, cv_ref, fo_ref):
    h = _rms(x_ref[...]) * g_ref[...] * (1.0 + sc_ref[0]) + sh_ref[0]
    p = _dot(h.astype(BF16), w_ref[...])
    z_ref[...] = p[:, :D_Z]
    cv_ref[...] = p[:, D_Z:D_Z + 3 * D_CONV]
    fo_ref[...] = p[:, D_Z + 3 * D_CONV:]


def _inproj(xa, gain, sc, sh, w_bf, ent):
    n = xa.shape[0]
    row = lambda w: pl.BlockSpec((ROW_BLOCK, w), lambda i: (i, 0))
    full = lambda a: pl.BlockSpec(a.shape, lambda i: (0,) * a.ndim)
    mod = pl.BlockSpec((1, 1, D_MODEL), lambda i: (ent(i), 0, 0))
    return pl.pallas_call(
        _inproj_body,
        grid=(n // ROW_BLOCK,),
        in_specs=[row(D_MODEL), full(gain), mod, mod, full(w_bf)],
        out_specs=[row(D_Z), row(3 * D_CONV), row(D_FOUR)],
        out_shape=[jax.ShapeDtypeStruct((n, D_Z), F32), jax.ShapeDtypeStruct((n, 3 * D_CONV), F32),
                   jax.ShapeDtypeStruct((n, D_FOUR), F32)],
        compiler_params=_params("parallel"),
        name="inproj",
    )(xa, gain, sc, sh, w_bf)


def _streams_math(blocks_per_batch, zm_ref, zp_ref, zn_ref, mu_ref, w2_ref, a2_ref, g2_ref,
                  w0_ref, a0_ref, kkw_ref, ka_ref, rk_ref, ones_ref):
    i = pl.program_id(0)
    seq_pos = i % blocks_per_batch
    ctx_i = (seq_pos == 0).astype(jnp.int32)
    z = zm_ref[...]
    tb = z.shape[0]
    t = lax.broadcasted_iota(jnp.int32, (tb, 1), 0)
    c = lax.broadcasted_iota(jnp.int32, (1, D_Z), 1)
    ctx_v = jnp.zeros((tb, 1), jnp.int32) + ctx_i
    col = t & (GRID_W - 1)
    lmask = (col != 0) | ((ctx_v != 0) & (t != 0))
    rmask = (col != GRID_W - 1) | ((ctx_v != 0) & (t != tb - 1))
    top_v = jnp.zeros((tb, 1), jnp.int32) + (seq_pos == 1).astype(jnp.int32)
    bot_v = jnp.zeros((tb, 1), jnp.int32) + (seq_pos == blocks_per_batch - 1).astype(jnp.int32)
    umask = jnp.logical_not((top_v != 0) & (t < GRID_W))
    dmask = jnp.logical_not((bot_v != 0) & (t >= tb - GRID_W))
    left = jnp.where(lmask, pltpu.roll(z, 1, axis=0), 0.0)
    right = jnp.where(rmask, pltpu.roll(z, tb - 1, axis=0), 0.0)
    up = jnp.where(umask, jnp.concatenate([zp_ref[...], z[:tb - GRID_W]], axis=0), 0.0)
    down = jnp.where(dmask, jnp.concatenate([z[GRID_W:], zn_ref[...]], axis=0), 0.0)
    q = c & jnp.where(ctx_i != 0, 1, 3)
    shifted = jnp.where(q == 0, left, jnp.where(q == 1, right, jnp.where(q == 2, up, down)))
    z = z + (shifted - z) * mu_ref[...]

    r = z[:, 0:D_RWKV]
    k = z[:, D_RWKV:2 * D_RWKV]
    v = z[:, 2 * D_RWKV:3 * D_RWKV]
    lw_in = z[:, 3 * D_RWKV:3 * D_RWKV + 128]
    la_in = z[:, 3 * D_RWKV + 128:3 * D_RWKV + 256]
    lg = z[:, 3 * D_RWKV + 256:]
    ones_bd = ones_ref[...]

    g = _dot3(_sigmoid(lg), g2_ref[...])
    kq = k * kkw_ref[...]
    kk = kq / jnp.maximum(jnp.sqrt(_headsum(kq * kq, ones_bd)), 1e-12)
    w_log = -_softplus(-(w0_ref[...] + _dot3(jnp.tanh(lw_in), w2_ref[...]))) - 0.5
    lw = -jnp.exp(w_log)
    a = _sigmoid(a0_ref[...] + _dot3(la_in, a2_ref[...]))
    ka = ka_ref[...]
    ksum = jnp.zeros_like(k)
    lw_d, k_d, b_d = [], [], []
    for d in range(2):
        a_d = a[:, d * D_RWKV:(d + 1) * D_RWKV]
        k_d.append(k * (1.0 + (a_d - 1.0) * ka))
        ksum = ksum + k_d[d]
        lw_d.append(lw[:, d * D_RWKV:(d + 1) * D_RWKV])
        b_d.append(kk * a_d)
    bonus = _headsum(r * ksum * rk_ref[...], ones_bd) * v
    return r, v, kk, g, bonus, lw_d, k_d, b_d


def _chunk_math(d, r, v, kk, lw, kd, b, p_out, q_out, rp_out, y0_out):
    cs = CHUNK
    rows = r.shape[0]
    n_chunks = rows // cs
    n_pairs = D_RWKV // LANES
    sgn = 1 if d == 0 else -1

    ri = lax.broadcasted_iota(jnp.int32, (rows, rows), 0)
    rj = lax.broadcasted_iota(jnp.int32, (rows, rows), 1)
    same_chunk = (ri >> 6) == (rj >> 6)
    t_incl = (same_chunk & ((ri - rj) * sgn >= 0)).astype(F32).astype(BF16)
    t_all = same_chunk.astype(F32).astype(BF16)
    l1 = lw.astype(BF16)
    rem = lw - l1.astype(F32)
    l2 = rem.astype(BF16)
    l3 = (rem - l2.astype(F32)).astype(BF16)
    gcum = _dot(t_incl, l1) + (_dot(t_incl, l2) + _dot(t_incl, l3))
    gtot = _dot(t_all, l1) + (_dot(t_all, l2) + _dot(t_all, l3))
    e_pos = jnp.exp(gcum)
    e_neg = jnp.exp(-gcum)
    e_rem = jnp.exp(gtot - gcum)
    e_prev = jnp.exp(gcum - lw)
    e_tot = jnp.exp(gtot)

    at = -(kk * e_prev)
    bt = b * e_neg
    kt = kd * e_neg
    rt = r * e_pos
    bh = b * e_rem
    kh = kd * e_rem

    ti = lax.broadcasted_iota(jnp.int32, (cs, LANES), 0)
    lane = lax.broadcasted_iota(jnp.int32, (cs, LANES), 1)
    tj = lane & (HEAD - 1)
    lo_half = lane < HEAD
    order = (ti - tj) * sgn
    strict = order > 0
    incl = order >= 0
    eye = ti == tj
    eye_f = eye.astype(F32)
    zero_bf = jnp.zeros((cs, LANES), BF16)

    def bd(y):
        y = y.astype(BF16)
        return jnp.concatenate([jnp.where(lo_half, y, zero_bf), jnp.where(lo_half, zero_bf, y)], axis=0)

    def mm(x, y_bd, dims=NN):
        return _dot(x.astype(BF16), y_bd, dims)

    chains = [(c, p) for c in range(n_chunks) for p in range(n_pairs)]
    tile = lambda a, c, p: a[c * cs:(c + 1) * cs, p * LANES:(p + 1) * LANES]

    s = [mm(jnp.concatenate([tile(at, c, p), tile(rt, c, p)], axis=0),
            jnp.concatenate([bd(tile(bt, c, p)), bd(tile(kt, c, p))], axis=0), NT) for c, p in chains]
    a_ab = [jnp.where(strict, x[:cs, :LANES], 0.0) for x in s]
    a_ak = [jnp.where(strict, x[:cs, LANES:], 0.0) for x in s]
    a_rbk = [jnp.concatenate([jnp.where(incl, x[cs:, :LANES], 0.0), jnp.where(incl, x[cs:, LANES:], 0.0)], axis=1)
             for x in s]
    a0 = [jnp.where((ti >> 3) == (tj >> 3), x, 0.0) for x in a_ab]
    a2 = [mm(x, bd(x)) for x in a0]
    a2_bd = [bd(x) for x in a2]
    x0 = [eye_f + x for x in a0]
    x1 = [x + mm(x, y) for x, y in zip(x0, a2_bd)]
    a4 = [mm(x, y) for x, y in zip(a2, a2_bd)]
    minv = [x + mm(x, bd(y)) for x, y in zip(x1, a4)]
    lvl = 3
    while (1 << lvl) < cs:
        off = ((ti >> (lvl + 1)) == (tj >> (lvl + 1))) & ((ti >> lvl) != (tj >> lvl))
        t = [mm(jnp.where(off, x, 0.0), bd(y)) for x, y in zip(a_ab, minv)]
        minv = [x + mm(x, bd(y)) for x, y in zip(minv, t)]
        lvl += 1
    v_bd = [bd(tile(v, c, p)) for c, p in chains]
    av = [mm(x, y) for x, y in zip(a_ak, v_bd)]
    wu = [mm(m, jnp.concatenate([bd(tile(at, c, p)), bd(y)], axis=1)) for m, y, (c, p) in zip(minv, av, chains)]
    top = [mm(jnp.concatenate([tile(bh, c, p), tile(kh, c, p)], axis=0),
              jnp.concatenate([x.astype(BF16), jnp.concatenate([zero_bf, tile(v, c, p).astype(BF16)], axis=1)],
                              axis=0), TN) for x, (c, p) in zip(wu, chains)]
    bot = [mm(x, jnp.concatenate([jnp.concatenate([bd(y[:, :LANES]), bd(y[:, LANES:])], axis=1),
                                  jnp.concatenate([jnp.zeros((2 * cs, LANES), BF16), vb], axis=1)], axis=0))
           for x, y, vb in zip(a_rbk, wu, v_bd)]
    for c in range(n_chunks):
        sel = lambda x, off: jnp.where(lo_half, x[:cs, off:off + LANES], x[cs:, off:off + LANES])
        idx = [c * n_pairs + p for p in range(n_pairs)]
        p_out[d, c] = jnp.concatenate(
            [sel(top[i], 0) + jnp.where(eye, tile(e_tot, c, p), 0.0) for p, i in enumerate(idx)], axis=1)
        q_out[d, c] = jnp.concatenate([sel(top[i], LANES) for i in idx], axis=1)
        rp_out[d, c * cs:(c + 1) * cs, :] = jnp.concatenate(
            [tile(rt, c, p) + bot[i][:, :LANES] for p, i in enumerate(idx)], axis=1)
        y0_out[d, c * cs:(c + 1) * cs, :] = jnp.concatenate([bot[i][:, LANES:] for i in idx], axis=1)


def _rwkv_chunks_body(blocks_per_batch, zm_ref, zp_ref, zn_ref, mu_ref, w2_ref, a2_ref, g2_ref, w0_ref, a0_ref,
                      kkw_ref, ka_ref, rk_ref, ones_ref, g_out, bon_out, p_out, q_out, rp_out, y0_out):
    r, v, kk, g, bonus, lw_d, k_d, b_d = _streams_math(
        blocks_per_batch, zm_ref, zp_ref, zn_ref, mu_ref, w2_ref, a2_ref, g2_ref, w0_ref, a0_ref, kkw_ref,
        ka_ref, rk_ref, ones_ref)
    g_out[...] = g
    bon_out[...] = bonus
    for d in range(2):
        _chunk_math(d, r, v, kk, lw_d[d], k_d[d], b_d[d], p_out, q_out, rp_out, y0_out)


def _rwkv_chunks(pz, seq, ctx_len, bsz, mu, w2bd, a2bd, g2, w0, a0, kkw, ka, rk, ones_bd):
    n = pz.shape[0]
    assert ctx_len == ROW_BLOCK and seq % ROW_BLOCK == 0 and n == bsz * (ctx_len + seq)
    sub = ROW_BLOCK // GRID_W
    last = n // GRID_W - 1
    cps = ROW_BLOCK // CHUNK
    full = lambda a: pl.BlockSpec(a.shape, lambda i: (0,) * a.ndim)
    row = pl.BlockSpec((ROW_BLOCK, D_RWKV), lambda i: (i, 0))
    row2 = pl.BlockSpec((2, ROW_BLOCK, D_RWKV), lambda i: (0, i, 0))
    mat = pl.BlockSpec((2, cps, HEAD, D_RWKV), lambda i: (0, i, 0, 0))
    s1 = jax.ShapeDtypeStruct((n, D_RWKV), F32)
    s2 = jax.ShapeDtypeStruct((2, n, D_RWKV), F32)
    sm = jax.ShapeDtypeStruct((2, n // CHUNK, HEAD, D_RWKV), F32)
    consts = (mu, w2bd, a2bd, g2, w0, a0, kkw, ka, rk, ones_bd)
    return pl.pallas_call(
        functools.partial(_rwkv_chunks_body, (ctx_len + seq) // ROW_BLOCK),
        grid=(n // ROW_BLOCK,),
        in_specs=[pl.BlockSpec((ROW_BLOCK, D_Z), lambda i: (i, 0)),
                  pl.BlockSpec((GRID_W, D_Z), lambda i: (jnp.maximum(i * sub - 1, 0), 0)),
                  pl.BlockSpec((GRID_W, D_Z), lambda i: (jnp.minimum((i + 1) * sub, last), 0))]
                 + [full(a) for a in consts],
        out_specs=[row, row, mat, mat, row2, row2],
        out_shape=[s1, s1, sm, sm, s2, s2],
        compiler_params=_params("parallel"),
        name="rwkv_chunks",
    )(pz, pz, pz, *consts)


SCAN_CHUNKS = 4


def _scan_body(pf_ref, qf_ref, rpf_ref, y0f_ref, pb_ref, qb_ref, rpb_ref, y0b_ref, yf_out, yb_out, h_scr):
    @pl.when(pl.program_id(0) == 0)
    def _():
        h_scr[...] = jnp.zeros_like(h_scr)

    cs = CHUNK
    bsz = h_scr.shape[1]
    n_pairs = D_RWKV // LANES
    lane = lax.broadcasted_iota(jnp.int32, (cs, LANES), 1)
    lo_half = lane < HEAD
    zero_bf = jnp.zeros((cs, LANES), BF16)

    def bd(y):
        return jnp.concatenate([jnp.where(lo_half, y, zero_bf), jnp.where(lo_half, zero_bf, y)], axis=0)

    dirs = ((pf_ref, qf_ref, rpf_ref, y0f_ref, yf_out), (pb_ref, qb_ref, rpb_ref, y0b_ref, yb_out))
    chains = [(d, b, p) for d in range(2) for b in range(bsz) for p in range(n_pairs)]
    state = [h_scr[d, b, :, p * LANES:(p + 1) * LANES] for d, b, p in chains]
    for s in range(SCAN_CHUNKS):
        outs = []
        for (d, b, p), hcur in zip(chains, state):
            p_ref, _, rp_ref, _, _ = dirs[d]
            c = s if d == 0 else SCAN_CHUNKS - 1 - s
            ls = slice(p * LANES, (p + 1) * LANES)
            x = jnp.concatenate([p_ref[0, b, c, :, ls], rp_ref[0, b, c * cs:(c + 1) * cs, ls]], axis=0)
            xh, xl = _split2(x)
            hh, hl = _split2(hcur)
            hh, hl = bd(hh), bd(hl)
            outs.append(_dot(xh, hh) + (_dot(xh, hl) + _dot(xl, hh)))
        new_state = []
        for (d, b, p), o in zip(chains, outs):
            _, q_ref, _, y0_ref, y_out = dirs[d]
            c = s if d == 0 else SCAN_CHUNKS - 1 - s
            ls = slice(p * LANES, (p + 1) * LANES)
            new_state.append(o[:cs] + q_ref[0, b, c, :, ls])
            y_out[b, c * cs:(c + 1) * cs, ls] = y0_ref[0, b, c * cs:(c + 1) * cs, ls] + o[cs:]
        state = new_state
    for (d, b, p), hcur in zip(chains, state):
        h_scr[d, b, :, p * LANES:(p + 1) * LANES] = hcur


def _chunk_scan(p, q, rp, y0, seq, ctx_len, bsz):
    n = rp.shape[1]
    rows_b = ctx_len + seq
    ncb = rows_b // CHUNK
    assert ctx_len % (SCAN_CHUNKS * CHUNK) == 0 and seq % (SCAN_CHUNKS * CHUNK) == 0
    steps = ncb // SCAN_CHUNKS
    ctx_steps = ctx_len // (SCAN_CHUNKS * CHUNK)
    p5 = p.reshape(2, bsz, ncb, HEAD, D_RWKV)
    q5 = q.reshape(2, bsz, ncb, HEAD, D_RWKV)
    rp4 = rp.reshape(2, bsz, rows_b, D_RWKV)
    y04 = y0.reshape(2, bsz, rows_b, D_RWKV)
    pos_f = lambda i: i
    pos_b = lambda i: jnp.where(i < ctx_steps, ctx_steps - 1 - i, steps - 1 - (i - ctx_steps))
    rows = SCAN_CHUNKS * CHUNK

    def specs(d, pos):
        mat = pl.BlockSpec((1, bsz, SCAN_CHUNKS, HEAD, D_RWKV), lambda i: (d, 0, pos(i), 0, 0))
        tok = pl.BlockSpec((1, bsz, rows, D_RWKV), lambda i: (d, 0, pos(i), 0))
        return [mat, mat, tok, tok]

    out_f = pl.BlockSpec((bsz, rows, D_RWKV), lambda i: (0, pos_f(i), 0))
    out_b = pl.BlockSpec((bsz, rows, D_RWKV), lambda i: (0, pos_b(i), 0))
    shp = jax.ShapeDtypeStruct((bsz, rows_b, D_RWKV), F32)
    yf, yb = pl.pallas_call(
        _scan_body,
        grid=(steps,),
        in_specs=specs(0, pos_f) + specs(1, pos_b),
        out_specs=[out_f, out_b],
        out_shape=[shp, shp],
        scratch_shapes=[pltpu.VMEM((2, bsz, HEAD, D_RWKV), F32)],
        compiler_params=_params("arbitrary"),
        name="chunk_scan",
    )(p5, q5, rp4, y04, p5, q5, rp4, y04)
    return yf.reshape(n, D_RWKV), yb.reshape(n, D_RWKV)


def _hilo(a):
    if isinstance(a, np.ndarray):
        a = a.astype(np.float32)
        hi = a.astype(BF16)
        return jnp.asarray(hi), jnp.asarray((a - hi.astype(np.float32)).astype(BF16))
    hi = a.astype(BF16)
    return hi, (a - hi.astype(F32)).astype(BF16)


def _channel_tables(length):
    j = np.arange(FOUR_GROUP)
    ang = 2.0 * np.pi * np.outer(j, j) / FOUR_GROUP
    scale = 1.0 / np.sqrt(float(length) * FOUR_GROUP)
    groups = D_FOUR // FOUR_GROUP
    c4 = np.kron(np.eye(groups), np.cos(ang)) * scale
    s4 = np.kron(np.eye(groups), np.sin(ang)) * scale
    return _hilo(c4) + _hilo(s4)


def _four_finish(fr, fi, c4h, c4l, s4h, s4l, gain):
    y = _dot3r(fr, c4h, c4l) + _dot3r(fi, s4h, s4l)
    return _rms(y) * gain


def _four_direct_body(f_ref, mh_ref, ml_ref, c4h, c4l, s4h, s4l, gain_ref, o_ref):
    length = f_ref.shape[1]
    fc = _dot3c(mh_ref[...], ml_ref[...], f_ref[0])
    o_ref[0] = _four_finish(fc[:length], fc[length:], c4h[...], c4l[...], s4h[...], s4l[...], gain_ref[...])


def _fourier_direct(f, gain):
    bsz, length, _ = f.shape
    t = np.arange(length)
    ang = 2.0 * np.pi * (np.outer(t, t) % length) / length
    mh, ml = _hilo(np.concatenate([np.cos(ang), -np.sin(ang)], axis=0))
    consts = (mh, ml) + _channel_tables(length) + (gain,)
    full = lambda a: pl.BlockSpec(a.shape, lambda b: (0,) * a.ndim)
    blk = pl.BlockSpec((1, length, D_FOUR), lambda b: (b, 0, 0))
    return pl.pallas_call(
        _four_direct_body,
        grid=(bsz,),
        in_specs=[blk] + [full(a) for a in consts],
        out_specs=blk,
        out_shape=jax.ShapeDtypeStruct(f.shape, F32),
        compiler_params=_params("parallel"),
        name="fourier_direct",
    )(f, *consts)


def _four_stage1_body(f_ref, mh_ref, ml_ref, o_ref):
    o_ref[0] = _dot3c(mh_ref[...], ml_ref[...], f_ref[0])


def _four_stage2_body(zr_ref, zi_ref, mh_ref, ml_ref, c4h, c4l, s4h, s4l, gain_ref, o_ref):
    group, inner = zr_ref.shape[1], zr_ref.shape[2]
    fc = [_dot3c(mh_ref[j], ml_ref[j], jnp.concatenate([zr_ref[0, j], zi_ref[0, j]], axis=0))
          for j in range(group)]
    o_ref[0] = jnp.concatenate(
        [_four_finish(x[:inner], x[inner:], c4h[...], c4l[...], s4h[...], s4l[...], gain_ref[...]) for x in fc],
        axis=1)


def _fourier_long(f, gain):
    bsz, length, _ = f.shape
    l2 = FOUR_INNER
    l1 = length // l2
    assert l1 * l2 == length and l1 % 16 == 0
    cols = l2 * D_FOUR
    col_tile = 4096
    th = np.arange(l1)
    ang1 = 2.0 * np.pi * (np.outer(th, th) % l1) / l1
    m1h, m1l = _hilo(np.concatenate([np.cos(ang1), -np.sin(ang1)], axis=0))
    z = pl.pallas_call(
        _four_stage1_body,
        grid=(bsz, cols // col_tile),
        in_specs=[pl.BlockSpec((1, l1, col_tile), lambda b, c: (b, 0, c)),
                  pl.BlockSpec(m1h.shape, lambda b, c: (0, 0)),
                  pl.BlockSpec(m1l.shape, lambda b, c: (0, 0))],
        out_specs=pl.BlockSpec((1, 2 * l1, col_tile), lambda b, c: (b, 0, c)),
        out_shape=jax.ShapeDtypeStruct((bsz, 2 * l1, cols), F32),
        compiler_params=_params("parallel", "parallel"),
        name="fourier_stage1",
    )(f.reshape(bsz, l1, cols), m1h, m1l)
    z = z.reshape(bsz, 2 * l1, l2, D_FOUR)

    ma = np.arange(l1)[:, None, None]
    mb = np.arange(l2)[None, :, None]
    tl = np.arange(l2)[None, None, :]
    ang2 = 2.0 * np.pi * (((ma + l1 * mb) * tl) % length) / length
    cos2, sin2 = np.cos(ang2), np.sin(ang2)
    m2 = np.concatenate([np.concatenate([cos2, sin2], axis=2), np.concatenate([-sin2, cos2], axis=2)], axis=1)
    m2h, m2l = _hilo(m2)
    consts = _channel_tables(length) + (gain,)
    full = lambda a: pl.BlockSpec(a.shape, lambda b, m: (0,) * a.ndim)
    grp = FOUR_GROUP_STEP
    tab = pl.BlockSpec((grp, 2 * l2, 2 * l2), lambda b, m: (m, 0, 0))
    out = pl.pallas_call(
        _four_stage2_body,
        grid=(bsz, l1 // grp),
        in_specs=[pl.BlockSpec((1, grp, l2, D_FOUR), lambda b, m: (b, m, 0, 0)),
                  pl.BlockSpec((1, grp, l2, D_FOUR), lambda b, m: (b, l1 // grp + m, 0, 0)),
                  tab, tab] + [full(a) for a in consts],
        out_specs=pl.BlockSpec((1, l2, grp * D_FOUR), lambda b, m: (b, 0, m)),
        out_shape=jax.ShapeDtypeStruct((bsz, l2, l1 * D_FOUR), F32),
        compiler_params=_params("parallel", "parallel"),
        name="fourier_stage2",
    )(z, z, m2h, m2l, *consts)
    return out.reshape(bsz, length, D_FOUR)


def _route(logits):
    lane = lax.broadcasted_iota(jnp.int32, (1, LANES), 1)
    lane_f = lane.astype(F32)
    neg = jnp.float32(-1e30)
    big = jnp.float32(1e9)
    gl = jnp.where(lane < N_GROUPS, logits, neg)
    gmax = jnp.max(gl, axis=-1, keepdims=True)
    pg_top = 1.0 / jnp.sum(jnp.exp(gl - gmax), axis=-1, keepdims=True)
    grp = jnp.min(jnp.where(gl == gmax, lane_f, big), axis=-1, keepdims=True)
    e_lane = lane - N_GROUPS
    in_grp = (e_lane >= 0) & (e_lane < N_EXPERTS) & ((e_lane >> 3).astype(F32) == grp)
    el = jnp.where(in_grp, logits, neg)
    m1 = jnp.max(el, axis=-1, keepdims=True)
    i1 = jnp.min(jnp.where(el == m1, lane_f, big), axis=-1, keepdims=True)
    el2 = jnp.where(lane_f == i1, neg, el)
    m2 = jnp.max(el2, axis=-1, keepdims=True)
    i2 = jnp.min(jnp.where(el2 == m2, lane_f, big), axis=-1, keepdims=True)
    e2 = jnp.exp(m2 - m1)
    den = 1.0 + e2
    return pg_top / den, pg_top * e2 / den, i1 - N_GROUPS, i2 - N_GROUPS


def _post_body(blocks_per_batch, yf_ref, yb_ref, bon_ref, g_ref, cv_ref, cvp_ref, cvn_ref, fo_ref, x_ref,
               wo_ref, gnw_ref, gnb_ref, cw_ref, cg_ref, g1_ref, sc2_ref, sh2_ref, n2_ref, wrh_ref, wrl_ref,
               rb_ref, ones_ref, x_out, h_out, route_out, count_out, count_scr):
    i = pl.program_id(0)

    @pl.when(i == 0)
    def _():
        count_scr[...] = jnp.zeros_like(count_scr)

    ones_bd = ones_ref[...]
    y = yf_ref[...] + yb_ref[...]
    mu = _headsum(y, ones_bd) * (1.0 / HEAD)
    yc = y - mu
    var = _headsum(yc * yc, ones_bd) * (1.0 / HEAD)
    yn = yc * lax.rsqrt(var + GN_EPS) * gnw_ref[...] + gnb_ref[...]
    o_rwkv = (yn + bon_ref[...]) * g_ref[...]

    cv = cv_ref[...]
    tb = cv.shape[0]
    t = lax.broadcasted_iota(jnp.int32, (tb, 1), 0)
    seq_pos = i % blocks_per_batch
    first = (seq_pos <= 1).astype(F32)
    final = jnp.logical_or(seq_pos == 0, seq_pos == blocks_per_batch - 1).astype(F32)
    zc = cv[:, D_CONV:2 * D_CONV] * cv[:, 2 * D_CONV:]
    zp_row = cvp_ref[7:8, D_CONV:2 * D_CONV] * cvp_ref[7:8, 2 * D_CONV:] * (1.0 - first)
    zn_row = cvn_ref[0:1, D_CONV:2 * D_CONV] * cvn_ref[0:1, 2 * D_CONV:] * (1.0 - final)
    prev = jnp.where(t == 0, zp_row, pltpu.roll(zc, 1, axis=0))
    nxt = jnp.where(t == tb - 1, zn_row, pltpu.roll(zc, tb - 1, axis=0))
    cw = cw_ref[...]
    conv = cv[:, :D_CONV] * (cw[0:1] * prev + cw[1:2] * zc + cw[2:3] * nxt)
    conv_o = _rms(conv) * cg_ref[...]

    mix = (_dot(o_rwkv.astype(BF16), wo_ref[0:D_RWKV, :])
           + _dot(conv_o.astype(BF16), wo_ref[D_RWKV:D_RWKV + D_CONV, :])
           + _dot(fo_ref[...].astype(BF16), wo_ref[D_RWKV + D_CONV:, :]))
    x = x_ref[...] + g1_ref[0] * mix
    x_out[...] = x
    h2 = _rms(x) * n2_ref[...] * (1.0 + sc2_ref[0]) + sh2_ref[0]
    h_out[...] = h2

    gate0, gate1, e0, e1 = _route(_dot3r(h2, wrh_ref[...], wrl_ref[...]) + rb_ref[...])
    lane = lax.broadcasted_iota(jnp.int32, (1, LANES), 1)
    lane_f = lane.astype(F32)
    oh0 = (lane_f == e0).astype(F32)
    oh1 = (lane_f == e1).astype(F32)
    ri = lax.broadcasted_iota(jnp.int32, (tb, tb), 0)
    rj = lax.broadcasted_iota(jnp.int32, (tb, tb), 1)
    earlier = (rj < ri).astype(F32).astype(BF16)
    seen = count_scr[...]
    tot0 = jnp.sum(oh0, axis=0, keepdims=True)
    before0 = _dot(earlier, oh0.astype(BF16)) + seen
    before1 = _dot(earlier, oh1.astype(BF16)) + (seen + tot0)
    rank0 = jnp.sum(oh0 * before0, axis=-1, keepdims=True)
    rank1 = jnp.sum(oh1 * before1, axis=-1, keepdims=True)
    seen = seen + tot0 + jnp.sum(oh1, axis=0, keepdims=True)
    count_scr[...] = seen
    count_out[...] = seen
    route_out[...] = jnp.where(
        lane == 0, gate0, jnp.where(lane == 1, gate1, jnp.where(lane == 2, e0, jnp.where(
            lane == 3, e1, jnp.where(lane == 4, rank0, jnp.where(lane == 5, rank1, 0.0))))))


def _post(yf, yb, bonus, g, pcv, four_o, xa, seq, ctx_len, ent, wo_bf, gnw, gnb, cw, cg, g1, sc2, sh2, n2,
          wrh, wrl, rb, ones_bd):
    n = xa.shape[0]
    assert ctx_len == ROW_BLOCK
    sub = ROW_BLOCK // 8
    last = n // 8 - 1
    full = lambda a: pl.BlockSpec(a.shape, lambda i: (0,) * a.ndim)
    row = lambda w: pl.BlockSpec((ROW_BLOCK, w), lambda i: (i, 0))
    mod = pl.BlockSpec((1, 1, D_MODEL), lambda i: (ent(i), 0, 0))
    return pl.pallas_call(
        functools.partial(_post_body, (ctx_len + seq) // ROW_BLOCK),
        grid=(n // ROW_BLOCK,),
        in_specs=[row(D_RWKV), row(D_RWKV), row(D_RWKV), row(D_RWKV), row(3 * D_CONV),
                  pl.BlockSpec((8, 3 * D_CONV), lambda i: (jnp.maximum(i * sub - 1, 0), 0)),
                  pl.BlockSpec((8, 3 * D_CONV), lambda i: (jnp.minimum((i + 1) * sub, last), 0)),
                  row(D_FOUR), row(D_MODEL), full(wo_bf), full(gnw), full(gnb), full(cw), full(cg),
                  mod, mod, mod, full(n2), full(wrh), full(wrl), full(rb), full(ones_bd)],
        out_specs=[row(D_MODEL), row(D_MODEL), row(LANES), pl.BlockSpec((1, LANES), lambda i: (0, 0))],
        out_shape=[jax.ShapeDtypeStruct((n, D_MODEL), F32), jax.ShapeDtypeStruct((n, D_MODEL), F32),
                   jax.ShapeDtypeStruct((n, LANES), F32), jax.ShapeDtypeStruct((1, LANES), F32)],
        scratch_shapes=[pltpu.VMEM((1, LANES), F32)],
        compiler_params=_params("arbitrary"),
        name="post_mix",
    )(yf, yb, bonus, g, pcv, pcv, pcv, four_o, xa, wo_bf, gnw, gnb, cw, cg, g1, sc2, sh2, n2, wrh, wrl, rb,
      ones_bd)


def _scatter_rows_body(zb_ref, zv_ref, dest_ref, h_ref, xb_ref, zero_buf, sem):
    tb = h_ref.shape[0]

    @pl.when(pl.program_id(0) == 0)
    def _():
        zero_buf[...] = jnp.zeros_like(zero_buf)

        def block_fill(j):
            start = pl.multiple_of(zb_ref[j] * MOE_ROWS, MOE_ROWS)
            return pltpu.make_async_copy(zero_buf, xb_ref.at[pl.ds(start, MOE_ROWS), :], sem)

        def fill_start(j, carry):
            @pl.when(zv_ref[j] != 0)
            def _():
                block_fill(j).start()
            return carry

        def fill_wait(j, carry):
            @pl.when(zv_ref[j] != 0)
            def _():
                block_fill(j).wait()
            return carry

        lax.fori_loop(0, zb_ref.shape[0], fill_start, 0)
        lax.fori_loop(0, zb_ref.shape[0], fill_wait, 0)

    def row_copy(t, k):
        return pltpu.make_async_copy(h_ref.at[pl.ds(t, 1), :],
                                     xb_ref.at[pl.ds(dest_ref[0, 0, 2 * t + k], 1), :], sem)

    def issue(t, carry):
        row_copy(t, 0).start()
        row_copy(t, 1).start()
        return carry

    def drain(t, carry):
        row_copy(t, 0).wait()
        row_copy(t, 1).wait()
        return carry

    lax.fori_loop(0, tb, issue, 0, unroll=DMA_UNROLL)
    lax.fori_loop(0, tb, drain, 0, unroll=DMA_UNROLL)


DMA_UNROLL = 8
SCATTER_ROWS = 512


def _scatter_rows(h2, dest, zero_blocks, zero_valid, cap):
    n = h2.shape[0]
    steps = n // SCATTER_ROWS
    grid_spec = pltpu.PrefetchScalarGridSpec(
        num_scalar_prefetch=2,
        grid=(steps,),
        in_specs=[pl.BlockSpec((1, 1, 2 * SCATTER_ROWS), lambda i, zb, zv: (i, 0, 0), memory_space=pltpu.SMEM),
                  pl.BlockSpec((SCATTER_ROWS, D_MODEL), lambda i, zb, zv: (i, 0))],
        out_specs=pl.BlockSpec(memory_space=pl.ANY),
        scratch_shapes=[pltpu.VMEM((MOE_ROWS, D_MODEL), F32), pltpu.SemaphoreType.DMA(())],
    )
    return pl.pallas_call(
        _scatter_rows_body,
        grid_spec=grid_spec,
        out_shape=jax.ShapeDtypeStruct((cap, D_MODEL), F32),
        compiler_params=_params("arbitrary", disable_bounds_checks=True),
        name="moe_scatter",
    )(zero_blocks, zero_valid, dest.reshape(steps, 1, 2 * SCATTER_ROWS), h2)


def _expert_body(be_ref, nu_ref, xb_ref, wg_ref, wu_ref, wd_ref, o_ref, wg_s, wu_s, wd_s):
    i = pl.program_id(0)
    prev = be_ref[jnp.maximum(i - 1, 0)]
    fresh = jnp.logical_or(i == 0, be_ref[i] != prev)

    @pl.when(jnp.logical_and(fresh, i < nu_ref[0]))
    def _():
        wg_s[...] = wg_ref[0, 0].astype(BF16)
        wu_s[...] = wu_ref[0, 0].astype(BF16)
        wd_s[...] = wd_ref[0, 0].astype(BF16)

    @pl.when(i < nu_ref[0])
    def _():
        xb = xb_ref[...].astype(BF16)
        gate = _dot(xb, wg_s[...])
        up = _dot(xb, wu_s[...])
        act = gate * _sigmoid(gate) * up
        o_ref[...] = _dot(act.astype(BF16), wd_s[...])

    @pl.when(i >= nu_ref[0])
    def _():
        o_ref[...] = jnp.zeros_like(o_ref)


def _experts(xb, block_e, n_used, exp_gate, exp_up, exp_down, layer):
    cap = xb.shape[0]
    nb = cap // MOE_ROWS
    blk = lambda i, nu: jnp.minimum(i, nu[0] - 1)
    grid_spec = pltpu.PrefetchScalarGridSpec(
        num_scalar_prefetch=2,
        grid=(nb,),
        in_specs=[pl.BlockSpec((MOE_ROWS, D_MODEL), lambda i, be, nu: (blk(i, nu), 0)),
                  pl.BlockSpec((1, 1, D_MODEL, D_EXPERT), lambda i, be, nu: (layer, be[blk(i, nu)], 0, 0)),
                  pl.BlockSpec((1, 1, D_MODEL, D_EXPERT), lambda i, be, nu: (layer, be[blk(i, nu)], 0, 0)),
                  pl.BlockSpec((1, 1, D_EXPERT, D_MODEL), lambda i, be, nu: (layer, be[blk(i, nu)], 0, 0))],
        out_specs=pl.BlockSpec((MOE_ROWS, D_MODEL), lambda i, be, nu: (i, 0)),
        scratch_shapes=[pltpu.VMEM((D_MODEL, D_EXPERT), BF16), pltpu.VMEM((D_MODEL, D_EXPERT), BF16),
                        pltpu.VMEM((D_EXPERT, D_MODEL), BF16)],
    )
    return pl.pallas_call(
        _expert_body,
        grid_spec=grid_spec,
        out_shape=jax.ShapeDtypeStruct((cap, D_MODEL), F32),
        compiler_params=_params("arbitrary"),
        name="experts",
    )(block_e, n_used, xb, exp_gate, exp_up, exp_down)


def _dispatch(route, counts):
    n = route.shape[0]
    eid = route[:, 2:4].astype(jnp.int32)
    rank = route[:, 4:6].astype(jnp.int32)
    counts = counts[0, :N_EXPERTS].astype(jnp.int32)
    pcounts = (counts + MOE_ROWS - 1) // MOE_ROWS * MOE_ROWS
    pend = jnp.cumsum(pcounts)
    pstart = pend - pcounts
    experts = jnp.arange(N_EXPERTS, dtype=jnp.int32)
    dest = jnp.sum(jnp.where(eid[:, :, None] == experts, pstart, 0), axis=-1) + rank
    nb = -(-2 * n // MOE_ROWS) + N_EXPERTS
    block_start = jnp.arange(nb, dtype=jnp.int32) * MOE_ROWS
    block_e = jnp.minimum(jnp.sum((pend[None, :] <= block_start[:, None]).astype(jnp.int32), axis=1),
                          N_EXPERTS - 1)
    n_used = pend[-1] // MOE_ROWS
    spare = n_used + experts
    zero_blocks = jnp.concatenate([pend // MOE_ROWS - 1, spare]).astype(jnp.int32)
    zero_valid = jnp.concatenate([pcounts > 0, spare < nb]).astype(jnp.int32)
    return (dest.reshape(n // ROW_BLOCK, 1, 2 * ROW_BLOCK), block_e, n_used.astype(jnp.int32).reshape(1),
            zero_blocks, zero_valid, nb * MOE_ROWS)


def _combine_body(final_norm, dest_ref, next_ref, x_ref, route_ref, g2_ref, fg_ref, yb_ref, o_ref, buf, sems):
    i = pl.program_id(0)
    tb = x_ref.shape[0]
    slot = i % 2

    def gather(idx_ref, s, wait):
        def row_copy(t, k):
            return pltpu.make_async_copy(yb_ref.at[pl.ds(idx_ref[0, 0, 2 * t + k], 1), :],
                                         buf.at[s, k, pl.ds(t, 1), :], sems.at[s])

        def body(t, carry):
            for k in range(2):
                if wait:
                    row_copy(t, k).wait()
                else:
                    row_copy(t, k).start()
            return carry

        lax.fori_loop(0, tb, body, 0, unroll=DMA_UNROLL)

    @pl.when(i == 0)
    def _():
        gather(dest_ref, slot, False)

    @pl.when(i + 1 < pl.num_programs(0))
    def _():
        gather(next_ref, 1 - slot, False)

    gather(dest_ref, slot, True)
    route = route_ref[...]
    x = x_ref[...] + g2_ref[0] * (route[:, 0:1] * buf[slot, 0] + route[:, 1:2] * buf[slot, 1])
    if final_norm:
        x = _rms(x) * fg_ref[...]
    o_ref[...] = x


def _combine(xa, yb, route, dest3, g2, final_g, blk, ent, n_out_blocks, final_norm):
    nxt = lambda i: blk(jnp.minimum(i + 1, n_out_blocks - 1))
    return pl.pallas_call(
        functools.partial(_combine_body, final_norm),
        grid=(n_out_blocks,),
        in_specs=[pl.BlockSpec((1, 1, 2 * ROW_BLOCK), lambda i: (blk(i), 0, 0), memory_space=pltpu.SMEM),
                  pl.BlockSpec((1, 1, 2 * ROW_BLOCK), lambda i: (nxt(i), 0, 0), memory_space=pltpu.SMEM),
                  pl.BlockSpec((ROW_BLOCK, D_MODEL), lambda i: (blk(i), 0)),
                  pl.BlockSpec((ROW_BLOCK, LANES), lambda i: (blk(i), 0)),
                  pl.BlockSpec((1, 1, D_MODEL), lambda i: (ent(blk(i)), 0, 0)),
                  pl.BlockSpec(final_g.shape, lambda i: (0, 0)),
                  pl.BlockSpec(memory_space=pl.ANY)],
        out_specs=pl.BlockSpec((ROW_BLOCK, D_MODEL), lambda i: (i, 0)),
        out_shape=jax.ShapeDtypeStruct((n_out_blocks * ROW_BLOCK, D_MODEL), F32),
        scratch_shapes=[pltpu.VMEM((2, 2, ROW_BLOCK, D_MODEL), F32), pltpu.SemaphoreType.DMA((2,))],
        compiler_params=_params("arbitrary", disable_bounds_checks=True),
        name="moe_combine",
    )(dest3, dest3, xa, route, g2, final_g, yb)


def _block_diag2(w):
    k, n = w.shape[1], w.shape[2]
    z = jnp.zeros((k, n), w.dtype)
    return jnp.concatenate([jnp.concatenate([w[0], z], axis=1), jnp.concatenate([z, w[1]], axis=1)], axis=0)


def kernel(x, c, ctx, c_ctx, ada_w, ada_b, norm1_g, norm2_g, w_in, mu_shift, decay_w0, decay_w2, iclr_a0, iclr_a2, gate_g2, k_k, k_a, r_k, gn_w, gn_b, conv_w, conv_gain, four_gain, w_out, router_g_w, router_g_b, router_e_w, router_e_b, exp_gate, exp_up, exp_down, final_g):
    bsz, seq, d = x.shape
    ctx_len = ctx.shape[1]
    depth = ada_w.shape[0]
    rows_b = ctx_len + seq
    n = bsz * rows_b
    bpb = rows_b // ROW_BLOCK
    lat_bpb = seq // ROW_BLOCK
    ent = lambda i: jnp.where(i % bpb == 0, bsz, i // bpb)

    xa = jnp.concatenate([ctx, x], axis=1).reshape(n, d)
    cvec = jnp.concatenate([c, c_ctx[None, :]], axis=0)
    cvec = cvec * jax.nn.sigmoid(cvec)
    head_id = np.arange(D_RWKV) // HEAD
    ones_bd = jnp.asarray(head_id[:, None] == head_id[None, :], BF16)
    row2 = lambda a: a.reshape(1, -1)

    for l in range(depth):
        last = l == depth - 1
        mod = jnp.dot(cvec, ada_w[l], precision=lax.Precision.HIGHEST) + ada_b[l]
        sh1, sc1, g1, sh2, sc2, g2 = [mod[:, j * d:(j + 1) * d].reshape(bsz + 1, 1, d) for j in range(6)]

        pz, pcv, pfo = _inproj(xa, row2(norm1_g[l]), sc1, sh1, w_in[l].astype(BF16), ent)
        g, bonus, pm, qm, rp, y0 = _rwkv_chunks(
            pz, seq, ctx_len, bsz, row2(mu_shift[l]), _block_diag2(decay_w2[l]), _block_diag2(iclr_a2[l]),
            gate_g2[l], row2(decay_w0[l]), row2(iclr_a0[l]), row2(k_k[l]), row2(k_a[l]), row2(r_k[l]), ones_bd)
        yf, yb = _chunk_scan(pm, qm, rp, y0, seq, ctx_len, bsz)

        fgain = row2(four_gain[l])
        pfo3 = pfo.reshape(bsz, rows_b, D_FOUR)
        four_o = jnp.concatenate([_fourier_direct(pfo3[:, :ctx_len], fgain),
                                  _fourier_long(pfo3[:, ctx_len:], fgain)], axis=1).reshape(n, D_FOUR)

        wr = jnp.zeros((d, LANES), F32).at[:, :N_GROUPS].set(router_g_w[l])
        wr = wr.at[:, N_GROUPS:N_GROUPS + N_EXPERTS].set(router_e_w[l])
        rb = jnp.zeros((1, LANES), F32).at[0, :N_GROUPS].set(router_g_b[l])
        rb = rb.at[0, N_GROUPS:N_GROUPS + N_EXPERTS].set(router_e_b[l])
        wrh, wrl = _hilo(wr)
        xa, h2, route, counts = _post(yf, yb, bonus, g, pcv, four_o, xa, seq, ctx_len, ent, w_out[l].astype(BF16),
                                      row2(gn_w[l]), row2(gn_b[l]), conv_w[l], row2(conv_gain[l]), g1, sc2, sh2,
                                      row2(norm2_g[l]), wrh, wrl, rb, ones_bd)

        dest3, block_e, n_used, zero_blocks, zero_valid, cap = _dispatch(route, counts)
        xe = _scatter_rows(h2, dest3, zero_blocks, zero_valid, cap)
        ye = _experts(xe, block_e, n_used, exp_gate, exp_up, exp_down, l)
        if last:
            blk = lambda i: (i // lat_bpb) * bpb + 1 + i % lat_bpb
            xa = _combine(xa, ye, route, dest3, g2, row2(final_g), blk, ent, bsz * lat_bpb, True)
        else:
            xa = _combine(xa, ye, route, dest3, g2, row2(final_g), lambda i: i, ent, n // ROW_BLOCK, False)

    return xa.reshape(bsz, seq, d)
```

```python
import functools

import numpy as np
import jax
import jax.numpy as jnp
from jax import lax
from jax.experimental import pallas as pl
from jax.experimental.pallas import tpu as pltpu

F32 = jnp.float32
BF16 = jnp.bfloat16

D_MODEL = 1024
HEAD = 64
D_RWKV = 512
H_RWKV = D_RWKV // HEAD
D_CONV = 256
D_FOUR = 256
FOUR_GROUP = 64
D_Z = 3 * D_RWKV + 2 * 64 + 2 * 64 + 128
D_IN = D_Z + 3 * D_CONV + D_FOUR
GRID_W = 64
N_GROUPS = 4
EXPERTS_PER_GROUP = 8
N_EXPERTS = N_GROUPS * EXPERTS_PER_GROUP
D_EXPERT = D_MODEL // 2
RMS_EPS = 1e-6
GN_EPS = 64e-5

CHUNK = 64
ROW_BLOCK = 256
MOE_ROWS = 256
FOUR_INNER = 128
FOUR_GROUP_STEP = 4
LANES = 128
VMEM_LIMIT = 48 * 1024 * 1024

NN = (((1,), (0,)), ((), ()))
NT = (((1,), (1,)), ((), ()))
TN = (((0,), (0,)), ((), ()))


def _params(*sem, **kw):
    return pltpu.CompilerParams(dimension_semantics=sem, vmem_limit_bytes=VMEM_LIMIT, **kw)


def _split2(x):
    hi = x.astype(BF16)
    lo = (x - hi.astype(F32)).astype(BF16)
    return hi, lo


def _dot(a, b, dims=NN):
    return lax.dot_general(a, b, dims, preferred_element_type=F32)


def _dot1(a, b, dims=NN):
    return _dot(a.astype(BF16), b.astype(BF16), dims)


def _dot3(a, b, dims=NN):
    ah, al = _split2(a)
    bh, bl = _split2(b)
    return _dot(ah, bh, dims) + (_dot(ah, bl, dims) + _dot(al, bh, dims))


def _dot3c(ch, cl, x, dims=NN):
    xh, xl = _split2(x)
    return _dot(ch, xh, dims) + (_dot(cl, xh, dims) + _dot(ch, xl, dims))


def _dot3r(x, ch, cl, dims=NN):
    xh, xl = _split2(x)
    return _dot(xh, ch, dims) + (_dot(xl, ch, dims) + _dot(xh, cl, dims))


def _headsums(xs, ones_pair):
    rows = xs[0].shape[0]
    tiles = xs[0].shape[1] // LANES
    parts = []
    for x in xs:
        for part in _split2(x):
            parts.extend(part[:, t * LANES:(t + 1) * LANES] for t in range(tiles))
    s = _dot(jnp.concatenate(parts, axis=0), ones_pair)
    outs = []
    for j in range(len(xs)):
        base = j * 2 * tiles
        outs.append(jnp.concatenate(
            [s[(base + t) * rows:(base + t + 1) * rows] + s[(base + tiles + t) * rows:(base + tiles + t + 1) * rows]
             for t in range(tiles)], axis=1))
    return outs


def _rms(x, eps=RMS_EPS):
    return x * lax.rsqrt(jnp.mean(x * x, axis=-1, keepdims=True) + eps)


def _sigmoid(x):
    return 1.0 / (1.0 + jnp.exp(-x))


def _softplus(x):
    return jnp.maximum(x, 0.0) + jnp.log(1.0 + jnp.exp(-jnp.abs(x)))


def _inproj_body(x_ref, g_ref, sc_ref, sh_ref, w_ref, z_ref, cv_ref, fo_ref):
    h = _rms(x_ref[...]) * g_ref[...] * (1.0 + sc_ref[0]) + sh_ref[0]
    p = _dot(h.astype(BF16), w_ref[...])
    z_ref[...] = p[:, :D_Z]
    cv_ref[...] = p[:, D_Z:D_Z + 3 * D_CONV]
    fo_ref[...] = p[:, D_Z + 3 * D_CONV:]


def _inproj(xa, gain, sc, sh, w_bf, ent):
    n = xa.shape[0]
    row = lambda w: pl.BlockSpec((ROW_BLOCK, w), lambda i: (i, 0))
    full = lambda a: pl.BlockSpec(a.shape, lambda i: (0,) * a.ndim)
    mod = pl.BlockSpec((1, 1, D_MODEL), lambda i: (ent(i), 0, 0))
    return pl.pallas_call(
        _inproj_body,
        grid=(n // ROW_BLOCK,),
        in_specs=[row(D_MODEL), full(gain), mod, mod, full(w_bf)],
        out_specs=[row(D_Z), row(3 * D_CONV), row(D_FOUR)],
        out_shape=[jax.ShapeDtypeStruct((n, D_Z), F32), jax.ShapeDtypeStruct((n, 3 * D_CONV), F32),
                   jax.ShapeDtypeStruct((n, D_FOUR), F32)],
        compiler_params=_params("parallel"),
        name="inproj",
    )(xa, gain, sc, sh, w_bf)


def _streams_math(blocks_per_batch, zm_ref, zp_ref, zn_ref, mu_ref, w2_ref, a2_ref, g2_ref,
                  w0_ref, a0_ref, kkw_ref, ka_ref, rk_ref, ones_ref):
    i = pl.program_id(0)
    seq_pos = i % blocks_per_batch
    ctx_i = (seq_pos == 0).astype(jnp.int32)
    z = zm_ref[...]
    tb = z.shape[0]
    t = lax.broadcasted_iota(jnp.int32, (tb, 1), 0)
    c = lax.broadcasted_iota(jnp.int32, (1, D_Z), 1)
    ctx_v = jnp.zeros((tb, 1), jnp.int32) + ctx_i
    col = t & (GRID_W - 1)
    lmask = (col != 0) | ((ctx_v != 0) & (t != 0))
    rmask = (col != GRID_W - 1) | ((ctx_v != 0) & (t != tb - 1))
    top_v = jnp.zeros((tb, 1), jnp.int32) + (seq_pos == 1).astype(jnp.int32)
    bot_v = jnp.zeros((tb, 1), jnp.int32) + (seq_pos == blocks_per_batch - 1).astype(jnp.int32)
    umask = jnp.logical_not((top_v != 0) & (t < GRID_W))
    dmask = jnp.logical_not((bot_v != 0) & (t >= tb - GRID_W))
    left = jnp.where(lmask, pltpu.roll(z, 1, axis=0), 0.0)
    right = jnp.where(rmask, pltpu.roll(z, tb - 1, axis=0), 0.0)
    up = jnp.where(umask, jnp.concatenate([zp_ref[...], z[:tb - GRID_W]], axis=0), 0.0)
    down = jnp.where(dmask, jnp.concatenate([z[GRID_W:], zn_ref[...]], axis=0), 0.0)
    q = c & jnp.where(ctx_i != 0, 1, 3)
    shifted = jnp.where(q == 0, left, jnp.where(q == 1, right, jnp.where(q == 2, up, down)))
    z = z + (shifted - z) * mu_ref[...]

    r = z[:, 0:D_RWKV]
    k = z[:, D_RWKV:2 * D_RWKV]
    v = z[:, 2 * D_RWKV:3 * D_RWKV]
    lw_in = z[:, 3 * D_RWKV:3 * D_RWKV + 128]
    la_in = z[:, 3 * D_RWKV + 128:3 * D_RWKV + 256]
    lg = z[:, 3 * D_RWKV + 256:]

    g = _dot3(_sigmoid(lg), g2_ref[...])
    kq = k * kkw_ref[...]
    w_log = -_softplus(-(w0_ref[...] + _dot3(jnp.tanh(lw_in), w2_ref[...]))) - 0.5
    lw = -jnp.exp(w_log)
    a = _sigmoid(a0_ref[...] + _dot3(la_in, a2_ref[...]))
    ka = ka_ref[...]
    a_d = [a[:, d * D_RWKV:(d + 1) * D_RWKV] for d in range(2)]
    k_d = [k * (1.0 + (x - 1.0) * ka) for x in a_d]
    lw_d = [lw[:, d * D_RWKV:(d + 1) * D_RWKV] for d in range(2)]
    kq_ss, rk_sum = _headsums([kq * kq, r * (k_d[0] + k_d[1]) * rk_ref[...]], ones_ref[...])
    kk = kq / jnp.maximum(jnp.sqrt(kq_ss), 1e-12)
    b_d = [kk * x for x in a_d]
    return r, v, kk, g, rk_sum * v, lw_d, k_d, b_d


def _chunk_maps(r, v, kk, lw_d, k_d, b_d, p_out, q_out, rp_out, y0_out):
    cs = CHUNK
    rows = r.shape[0]
    n_chunks = rows // cs
    n_pairs = D_RWKV // LANES

    ri = lax.broadcasted_iota(jnp.int32, (rows, rows), 0)
    rj = lax.broadcasted_iota(jnp.int32, (rows, rows), 1)
    same_chunk = (ri >> 6) == (rj >> 6)
    ti = lax.broadcasted_iota(jnp.int32, (cs, LANES), 0)
    lane = lax.broadcasted_iota(jnp.int32, (cs, LANES), 1)
    tj = lane & (HEAD - 1)
    lo_half = lane < HEAD
    eye = ti == tj
    eye_f = eye.astype(F32)
    zero_bf = jnp.zeros((cs, LANES), BF16)
    strict, incl = [], []
    at, bt, kt, rt, bh, kh, e_tot = [], [], [], [], [], [], []
    for d in range(2):
        sgn = 1 if d == 0 else -1
        order = (ti - tj) * sgn
        strict.append(order > 0)
        incl.append(order >= 0)
        t_mat = jnp.concatenate([(same_chunk & ((ri - rj) * sgn >= 0)).astype(F32).astype(BF16),
                                 same_chunk.astype(F32).astype(BF16)], axis=0)
        lw = lw_d[d]
        l1 = lw.astype(BF16)
        rem = lw - l1.astype(F32)
        l2 = rem.astype(BF16)
        l3 = (rem - l2.astype(F32)).astype(BF16)
        gsum = _dot(t_mat, l1) + (_dot(t_mat, l2) + _dot(t_mat, l3))
        gcum, gtot = gsum[:rows], gsum[rows:]
        e_neg = jnp.exp(-gcum)
        e_rem = jnp.exp(gtot - gcum)
        at.append(-(kk * jnp.exp(gcum - lw)))
        bt.append(b_d[d] * e_neg)
        kt.append(k_d[d] * e_neg)
        rt.append(r * jnp.exp(gcum))
        bh.append(b_d[d] * e_rem)
        kh.append(k_d[d] * e_rem)
        e_tot.append(jnp.exp(gtot))

    def bd(y):
        y = y.astype(BF16)
        return jnp.concatenate([jnp.where(lo_half, y, zero_bf), jnp.where(lo_half, zero_bf, y)], axis=0)

    def mm(x, y_bd, dims=NN):
        return _dot(x.astype(BF16), y_bd, dims)

    tile = lambda a, c, p: a[c * cs:(c + 1) * cs, p * LANES:(p + 1) * LANES]
    for d in range(2):
        _chunk_stages(d, [(d, c, p) for c in range(n_chunks) for p in range(n_pairs)], tile, bd, mm, strict,
                      incl, eye, eye_f, lo_half, ti, tj, at, bt, kt, rt, bh, kh, e_tot, v,
                      p_out, q_out, rp_out, y0_out)


def _chunk_stages(d, chains, tile, bd, mm, strict, incl, eye, eye_f, lo_half, ti, tj, at, bt, kt, rt, bh, kh,
                  e_tot, v, p_out, q_out, rp_out, y0_out):
    cs = CHUNK
    n_pairs = D_RWKV // LANES
    n_chunks = len(chains) // n_pairs
    s = [mm(jnp.concatenate([tile(at[d], c, p), tile(rt[d], c, p)], axis=0),
            jnp.concatenate([bd(tile(bt[d], c, p)), bd(tile(kt[d], c, p))], axis=0), NT) for d, c, p in chains]
    a_ab = [jnp.where(strict[d], x[:cs, :LANES], 0.0) for x, (d, _, _) in zip(s, chains)]
    a_ak = [jnp.where(strict[d], x[:cs, LANES:], 0.0) for x, (d, _, _) in zip(s, chains)]
    a_rb = [jnp.where(incl[d], x[cs:, :LANES], 0.0) for x, (d, _, _) in zip(s, chains)]
    a_rk = [jnp.where(incl[d], x[cs:, LANES:], 0.0) for x, (d, _, _) in zip(s, chains)]
    a0 = [jnp.where((ti >> 3) == (tj >> 3), x, 0.0) for x in a_ab]
    a2 = [mm(x, bd(x)) for x in a0]
    a2_bd = [bd(x) for x in a2]
    x0 = [eye_f + x for x in a0]
    x1 = [x + mm(x, y) for x, y in zip(x0, a2_bd)]
    a4 = [mm(x, y) for x, y in zip(a2, a2_bd)]
    minv = [x + mm(x, bd(y)) for x, y in zip(x1, a4)]
    lvl = 3
    while (1 << lvl) < cs:
        off = ((ti >> (lvl + 1)) == (tj >> (lvl + 1))) & ((ti >> lvl) != (tj >> lvl))
        t = [mm(jnp.where(off, x, 0.0), bd(y)) for x, y in zip(a_ab, minv)]
        minv = [x + mm(x, bd(y)) for x, y in zip(minv, t)]
        lvl += 1
    v_bd = [bd(tile(v, c, p)) for _, c, p in chains]
    avk = [mm(jnp.concatenate([x, z], axis=0), y) for x, z, y in zip(a_ak, a_rk, v_bd)]
    av = [x[:cs] for x in avk]
    bot_k = [x[cs:] for x in avk]
    wu = [mm(m, jnp.concatenate([bd(tile(at[d], c, p)), bd(y)], axis=1))
          for m, y, (d, c, p) in zip(minv, av, chains)]
    top_b = [mm(tile(bh[d], c, p), x.astype(BF16), TN) for x, (d, c, p) in zip(wu, chains)]
    top_k = [mm(tile(kh[d], c, p), tile(v, c, p).astype(BF16), TN) for d, c, p in chains]
    bot_b = [mm(x, jnp.concatenate([bd(y[:, :LANES]), bd(y[:, LANES:])], axis=1)) for x, y in zip(a_rb, wu)]
    sel = lambda x, off: jnp.where(lo_half, x[:cs, off:off + LANES], x[cs:, off:off + LANES])
    for c in range(n_chunks):
        idx = [c * n_pairs + p for p in range(n_pairs)]
        p_out[d, c] = jnp.concatenate(
            [sel(top_b[i], 0) + jnp.where(eye, tile(e_tot[d], c, p), 0.0) for p, i in enumerate(idx)], axis=1)
        q_out[d, c] = jnp.concatenate([sel(top_b[i], LANES) + sel(top_k[i], 0) for i in idx], axis=1)
        rp_out[d, c * cs:(c + 1) * cs, :] = jnp.concatenate(
            [tile(rt[d], c, p) + bot_b[i][:, :LANES] for p, i in enumerate(idx)], axis=1)
        y0_out[d, c * cs:(c + 1) * cs, :] = jnp.concatenate(
            [bot_b[i][:, LANES:] + bot_k[i] for i in idx], axis=1)


def _rwkv_chunks_body(blocks_per_batch, zm_ref, zp_ref, zn_ref, mu_ref, w2_ref, a2_ref, g2_ref, w0_ref, a0_ref,
                      kkw_ref, ka_ref, rk_ref, ones_ref, g_out, bon_out, p_out, q_out, rp_out, y0_out):
    r, v, kk, g, bonus, lw_d, k_d, b_d = _streams_math(
        blocks_per_batch, zm_ref, zp_ref, zn_ref, mu_ref, w2_ref, a2_ref, g2_ref, w0_ref, a0_ref, kkw_ref,
        ka_ref, rk_ref, ones_ref)
    g_out[...] = g
    bon_out[...] = bonus
    _chunk_maps(r, v, kk, lw_d, k_d, b_d, p_out, q_out, rp_out, y0_out)


def _rwkv_chunks(pz, seq, ctx_len, bsz, mu, w2bd, a2bd, g2, w0, a0, kkw, ka, rk, ones_bd):
    n = pz.shape[0]
    assert ctx_len == ROW_BLOCK and seq % ROW_BLOCK == 0 and n == bsz * (ctx_len + seq)
    sub = ROW_BLOCK // GRID_W
    last = n // GRID_W - 1
    cps = ROW_BLOCK // CHUNK
    full = lambda a: pl.BlockSpec(a.shape, lambda i: (0,) * a.ndim)
    row = pl.BlockSpec((ROW_BLOCK, D_RWKV), lambda i: (i, 0))
    row2 = pl.BlockSpec((2, ROW_BLOCK, D_RWKV), lambda i: (0, i, 0))
    mat = pl.BlockSpec((2, cps, HEAD, D_RWKV), lambda i: (0, i, 0, 0))
    s1 = jax.ShapeDtypeStruct((n, D_RWKV), F32)
    s2 = jax.ShapeDtypeStruct((2, n, D_RWKV), F32)
    sm = jax.ShapeDtypeStruct((2, n // CHUNK, HEAD, D_RWKV), F32)
    consts = (mu, w2bd, a2bd, g2, w0, a0, kkw, ka, rk, ones_bd)
    return pl.pallas_call(
        functools.partial(_rwkv_chunks_body, (ctx_len + seq) // ROW_BLOCK),
        grid=(n // ROW_BLOCK,),
        in_specs=[pl.BlockSpec((ROW_BLOCK, D_Z), lambda i: (i, 0)),
                  pl.BlockSpec((GRID_W, D_Z), lambda i: (jnp.maximum(i * sub - 1, 0), 0)),
                  pl.BlockSpec((GRID_W, D_Z), lambda i: (jnp.minimum((i + 1) * sub, last), 0))]
                 + [full(a) for a in consts],
        out_specs=[row, row, mat, mat, row2, row2],
        out_shape=[s1, s1, sm, sm, s2, s2],
        compiler_params=_params("parallel"),
        name="rwkv_chunks",
    )(pz, pz, pz, *consts)


SCAN_CHUNKS = 4


def _scan_body(pf_ref, qf_ref, rpf_ref, y0f_ref, pb_ref, qb_ref, rpb_ref, y0b_ref, yf_out, yb_out, h_scr):
    @pl.when(pl.program_id(0) == 0)
    def _():
        h_scr[...] = jnp.zeros_like(h_scr)

    cs = CHUNK
    bsz = h_scr.shape[1]
    n_pairs = D_RWKV // LANES
    lane = lax.broadcasted_iota(jnp.int32, (cs, LANES), 1)
    lo_half = lane < HEAD
    zero_bf = jnp.zeros((cs, LANES), BF16)

    def bd(y):
        return jnp.concatenate([jnp.where(lo_half, y, zero_bf), jnp.where(lo_half, zero_bf, y)], axis=0)

    dirs = ((pf_ref, qf_ref, rpf_ref, y0f_ref, yf_out), (pb_ref, qb_ref, rpb_ref, y0b_ref, yb_out))
    chains = [(d, b, p) for d in range(2) for b in range(bsz) for p in range(n_pairs)]
    state = [h_scr[d, b, :, p * LANES:(p + 1) * LANES] for d, b, p in chains]
    for s in range(SCAN_CHUNKS):
        outs = []
        for (d, b, p), hcur in zip(chains, state):
            p_ref, _, rp_ref, _, _ = dirs[d]
            c = s if d == 0 else SCAN_CHUNKS - 1 - s
            ls = slice(p * LANES, (p + 1) * LANES)
            x = jnp.concatenate([p_ref[0, b, c, :, ls], rp_ref[0, b, c * cs:(c + 1) * cs, ls]], axis=0)
            xh, xl = _split2(x)
            hh, hl = _split2(hcur)
            hh, hl = bd(hh), bd(hl)
            outs.append(_dot(xh, hh) + (_dot(xh, hl) + _dot(xl, hh)))
        new_state = []
        for (d, b, p), o in zip(chains, outs):
            _, q_ref, _, y0_ref, y_out = dirs[d]
            c = s if d == 0 else SCAN_CHUNKS - 1 - s
            ls = slice(p * LANES, (p + 1) * LANES)
            new_state.append(o[:cs] + q_ref[0, b, c, :, ls])
            y_out[b, c * cs:(c + 1) * cs, ls] = y0_ref[0, b, c * cs:(c + 1) * cs, ls] + o[cs:]
        state = new_state
    for (d, b, p), hcur in zip(chains, state):
        h_scr[d, b, :, p * LANES:(p + 1) * LANES] = hcur


def _chunk_scan(p, q, rp, y0, seq, ctx_len, bsz):
    n = rp.shape[1]
    rows_b = ctx_len + seq
    ncb = rows_b // CHUNK
    assert ctx_len % (SCAN_CHUNKS * CHUNK) == 0 and seq % (SCAN_CHUNKS * CHUNK) == 0
    steps = ncb // SCAN_CHUNKS
    ctx_steps = ctx_len // (SCAN_CHUNKS * CHUNK)
    p5 = p.reshape(2, bsz, ncb, HEAD, D_RWKV)
    q5 = q.reshape(2, bsz, ncb, HEAD, D_RWKV)
    rp4 = rp.reshape(2, bsz, rows_b, D_RWKV)
    y04 = y0.reshape(2, bsz, rows_b, D_RWKV)
    pos_f = lambda i: i
    pos_b = lambda i: jnp.where(i < ctx_steps, ctx_steps - 1 - i, steps - 1 - (i - ctx_steps))
    rows = SCAN_CHUNKS * CHUNK

    def specs(d, pos):
        mat = pl.BlockSpec((1, bsz, SCAN_CHUNKS, HEAD, D_RWKV), lambda i: (d, 0, pos(i), 0, 0))
        tok = pl.BlockSpec((1, bsz, rows, D_RWKV), lambda i: (d, 0, pos(i), 0))
        return [mat, mat, tok, tok]

    out_f = pl.BlockSpec((bsz, rows, D_RWKV), lambda i: (0, pos_f(i), 0))
    out_b = pl.BlockSpec((bsz, rows, D_RWKV), lambda i: (0, pos_b(i), 0))
    shp = jax.ShapeDtypeStruct((bsz, rows_b, D_RWKV), F32)
    yf, yb = pl.pallas_call(
        _scan_body,
        grid=(steps,),
        in_specs=specs(0, pos_f) + specs(1, pos_b),
        out_specs=[out_f, out_b],
        out_shape=[shp, shp],
        scratch_shapes=[pltpu.VMEM((2, bsz, HEAD, D_RWKV), F32)],
        compiler_params=_params("arbitrary"),
        name="chunk_scan",
    )(p5, q5, rp4, y04, p5, q5, rp4, y04)
    return yf.reshape(n, D_RWKV), yb.reshape(n, D_RWKV)


def _hilo(a):
    if isinstance(a, np.ndarray):
        a = a.astype(np.float32)
        hi = a.astype(BF16)
        return jnp.asarray(hi), jnp.asarray((a - hi.astype(np.float32)).astype(BF16))
    hi = a.astype(BF16)
    return hi, (a - hi.astype(F32)).astype(BF16)


def _channel_tables(length):
    j = np.arange(FOUR_GROUP)
    ang = 2.0 * np.pi * np.outer(j, j) / FOUR_GROUP
    scale = 1.0 / np.sqrt(float(length) * FOUR_GROUP)
    groups = D_FOUR // FOUR_GROUP
    c4 = np.kron(np.eye(groups), np.cos(ang)) * scale
    s4 = np.kron(np.eye(groups), np.sin(ang)) * scale
    return _hilo(c4) + _hilo(s4)


def _four_finish(fr, fi, c4h, c4l, s4h, s4l, gain):
    y = _dot3r(fr, c4h, c4l) + _dot3r(fi, s4h, s4l)
    return _rms(y) * gain


def _four_direct_body(f_ref, mh_ref, ml_ref, c4h, c4l, s4h, s4l, gain_ref, o_ref):
    length = f_ref.shape[1]
    fc = _dot3c(mh_ref[...], ml_ref[...], f_ref[0])
    o_ref[0] = _four_finish(fc[:length], fc[length:], c4h[...], c4l[...], s4h[...], s4l[...], gain_ref[...])


def _fourier_direct(f, gain):
    bsz, length, _ = f.shape
    t = np.arange(length)
    ang = 2.0 * np.pi * (np.outer(t, t) % length) / length
    mh, ml = _hilo(np.concatenate([np.cos(ang), -np.sin(ang)], axis=0))
    consts = (mh, ml) + _channel_tables(length) + (gain,)
    full = lambda a: pl.BlockSpec(a.shape, lambda b: (0,) * a.ndim)
    blk = pl.BlockSpec((1, length, D_FOUR), lambda b: (b, 0, 0))
    return pl.pallas_call(
        _four_direct_body,
        grid=(bsz,),
        in_specs=[blk] + [full(a) for a in consts],
        out_specs=blk,
        out_shape=jax.ShapeDtypeStruct(f.shape, F32),
        compiler_params=_params("parallel"),
        name="fourier_direct",
    )(f, *consts)


def _four_stage1_body(f_ref, mh_ref, ml_ref, o_ref):
    o_ref[0] = _dot3c(mh_ref[...], ml_ref[...], f_ref[0])


def _four_stage2_body(zr_ref, zi_ref, mh_ref, ml_ref, c4h, c4l, s4h, s4l, gain_ref, o_ref):
    group, inner = zr_ref.shape[1], zr_ref.shape[2]
    fc = [_dot3c(mh_ref[j], ml_ref[j], jnp.concatenate([zr_ref[0, j], zi_ref[0, j]], axis=0))
          for j in range(group)]
    o_ref[0] = jnp.concatenate(
        [_four_finish(x[:inner], x[inner:], c4h[...], c4l[...], s4h[...], s4l[...], gain_ref[...]) for x in fc],
        axis=1)


def _fourier_long(f, gain):
    bsz, length, _ = f.shape
    l2 = FOUR_INNER
    l1 = length // l2
    assert l1 * l2 == length and l1 % 16 == 0
    cols = l2 * D_FOUR
    col_tile = 4096
    th = np.arange(l1)
    ang1 = 2.0 * np.pi * (np.outer(th, th) % l1) / l1
    m1h, m1l = _hilo(np.concatenate([np.cos(ang1), -np.sin(ang1)], axis=0))
    z = pl.pallas_call(
        _four_stage1_body,
        grid=(bsz, cols // col_tile),
        in_specs=[pl.BlockSpec((1, l1, col_tile), lambda b, c: (b, 0, c)),
                  pl.BlockSpec(m1h.shape, lambda b, c: (0, 0)),
                  pl.BlockSpec(m1l.shape, lambda b, c: (0, 0))],
        out_specs=pl.BlockSpec((1, 2 * l1, col_tile), lambda b, c: (b, 0, c)),
        out_shape=jax.ShapeDtypeStruct((bsz, 2 * l1, cols), F32),
        compiler_params=_params("parallel", "parallel"),
        name="fourier_stage1",
    )(f.reshape(bsz, l1, cols), m1h, m1l)
    z = z.reshape(bsz, 2 * l1, l2, D_FOUR)

    ma = np.arange(l1)[:, None, None]
    mb = np.arange(l2)[None, :, None]
    tl = np.arange(l2)[None, None, :]
    ang2 = 2.0 * np.pi * (((ma + l1 * mb) * tl) % length) / length
    cos2, sin2 = np.cos(ang2), np.sin(ang2)
    m2 = np.concatenate([np.concatenate([cos2, sin2], axis=2), np.concatenate([-sin2, cos2], axis=2)], axis=1)
    m2h, m2l = _hilo(m2)
    consts = _channel_tables(length) + (gain,)
    full = lambda a: pl.BlockSpec(a.shape, lambda b, m: (0,) * a.ndim)
    grp = FOUR_GROUP_STEP
    tab = pl.BlockSpec((grp, 2 * l2, 2 * l2), lambda b, m: (m, 0, 0))
    out = pl.pallas_call(
        _four_stage2_body,
        grid=(bsz, l1 // grp),
        in_specs=[pl.BlockSpec((1, grp, l2, D_FOUR), lambda b, m: (b, m, 0, 0)),
                  pl.BlockSpec((1, grp, l2, D_FOUR), lambda b, m: (b, l1 // grp + m, 0, 0)),
                  tab, tab] + [full(a) for a in consts],
        out_specs=pl.BlockSpec((1, l2, grp * D_FOUR), lambda b, m: (b, 0, m)),
        out_shape=jax.ShapeDtypeStruct((bsz, l2, l1 * D_FOUR), F32),
        compiler_params=_params("parallel", "parallel"),
        name="fourier_stage2",
    )(z, z, m2h, m2l, *consts)
    return out.reshape(bsz, length, D_FOUR)


def _route(logits):
    lane = lax.broadcasted_iota(jnp.int32, (1, LANES), 1)
    lane_f = lane.astype(F32)
    neg = jnp.float32(-1e30)
    big = jnp.float32(1e9)
    gl = jnp.where(lane < N_GROUPS, logits, neg)
    gmax = jnp.max(gl, axis=-1, keepdims=True)
    pg_top = 1.0 / jnp.sum(jnp.exp(gl - gmax), axis=-1, keepdims=True)
    grp = jnp.min(jnp.where(gl == gmax, lane_f, big), axis=-1, keepdims=True)
    e_lane = lane - N_GROUPS
    in_grp = (e_lane >= 0) & (e_lane < N_EXPERTS) & ((e_lane >> 3).astype(F32) == grp)
    el = jnp.where(in_grp, logits, neg)
    m1 = jnp.max(el, axis=-1, keepdims=True)
    i1 = jnp.min(jnp.where(el == m1, lane_f, big), axis=-1, keepdims=True)
    el2 = jnp.where(lane_f == i1, neg, el)
    m2 = jnp.max(el2, axis=-1, keepdims=True)
    i2 = jnp.min(jnp.where(el2 == m2, lane_f, big), axis=-1, keepdims=True)
    e2 = jnp.exp(m2 - m1)
    den = 1.0 + e2
    return pg_top / den, pg_top * e2 / den, i1 - N_GROUPS, i2 - N_GROUPS


def _post_body(blocks_per_batch, yf_ref, yb_ref, bon_ref, g_ref, cv_ref, cvp_ref, cvn_ref, fo_ref, x_ref,
               wo_ref, gnw_ref, gnb_ref, cw_ref, cg_ref, g1_ref, sc2_ref, sh2_ref, n2_ref, wrh_ref, wrl_ref,
               rb_ref, ones_ref, x_out, h_out, route_out, count_out, count_scr):
    i = pl.program_id(0)

    @pl.when(i == 0)
    def _():
        count_scr[...] = jnp.zeros_like(count_scr)

    ones_pair = ones_ref[...]
    y = yf_ref[...] + yb_ref[...]
    mu = _headsums([y], ones_pair)[0] * (1.0 / HEAD)
    yc = y - mu
    var = _headsums([yc * yc], ones_pair)[0] * (1.0 / HEAD)
    yn = yc * lax.rsqrt(var + GN_EPS) * gnw_ref[...] + gnb_ref[...]
    o_rwkv = (yn + bon_ref[...]) * g_ref[...]

    cv = cv_ref[...]
    tb = cv.shape[0]
    t = lax.broadcasted_iota(jnp.int32, (tb, 1), 0)
    seq_pos = i % blocks_per_batch
    first = (seq_pos <= 1).astype(F32)
    final = jnp.logical_or(seq_pos == 0, seq_pos == blocks_per_batch - 1).astype(F32)
    zc = cv[:, D_CONV:2 * D_CONV] * cv[:, 2 * D_CONV:]
    zp_row = cvp_ref[7:8, D_CONV:2 * D_CONV] * cvp_ref[7:8, 2 * D_CONV:] * (1.0 - first)
    zn_row = cvn_ref[0:1, D_CONV:2 * D_CONV] * cvn_ref[0:1, 2 * D_CONV:] * (1.0 - final)
    prev = jnp.where(t == 0, zp_row, pltpu.roll(zc, 1, axis=0))
    nxt = jnp.where(t == tb - 1, zn_row, pltpu.roll(zc, tb - 1, axis=0))
    cw = cw_ref[...]
    conv = cv[:, :D_CONV] * (cw[0:1] * prev + cw[1:2] * zc + cw[2:3] * nxt)
    conv_o = _rms(conv) * cg_ref[...]

    mix = (_dot(o_rwkv.astype(BF16), wo_ref[0:D_RWKV, :])
           + _dot(conv_o.astype(BF16), wo_ref[D_RWKV:D_RWKV + D_CONV, :])
           + _dot(fo_ref[...].astype(BF16), wo_ref[D_RWKV + D_CONV:, :]))
    x = x_ref[...] + g1_ref[0] * mix
    x_out[...] = x
    h2 = _rms(x) * n2_ref[...] * (1.0 + sc2_ref[0]) + sh2_ref[0]
    h_out[...] = h2

    gate0, gate1, e0, e1 = _route(_dot3r(h2, wrh_ref[...], wrl_ref[...]) + rb_ref[...])
    lane = lax.broadcasted_iota(jnp.int32, (1, LANES), 1)
    lane_f = lane.astype(F32)
    oh0 = (lane_f == e0).astype(F32)
    oh1 = (lane_f == e1).astype(F32)
    ri = lax.broadcasted_iota(jnp.int32, (tb, tb), 0)
    rj = lax.broadcasted_iota(jnp.int32, (tb, tb), 1)
    earlier = (rj < ri).astype(F32).astype(BF16)
    seen = count_scr[...]
    tot0 = jnp.sum(oh0, axis=0, keepdims=True)
    before = _dot(earlier, jnp.concatenate([oh0, oh1], axis=1).astype(BF16))
    before0 = before[:, :LANES] + seen
    before1 = before[:, LANES:] + (seen + tot0)
    rank0 = jnp.sum(oh0 * before0, axis=-1, keepdims=True)
    rank1 = jnp.sum(oh1 * before1, axis=-1, keepdims=True)
    seen = seen + tot0 + jnp.sum(oh1, axis=0, keepdims=True)
    count_scr[...] = seen
    count_out[...] = seen
    route_out[...] = jnp.where(
        lane == 0, gate0, jnp.where(lane == 1, gate1, jnp.where(lane == 2, e0, jnp.where(
            lane == 3, e1, jnp.where(lane == 4, rank0, jnp.where(lane == 5, rank1, 0.0))))))


def _post(yf, yb, bonus, g, pcv, four_o, xa, seq, ctx_len, ent, wo_bf, gnw, gnb, cw, cg, g1, sc2, sh2, n2,
          wrh, wrl, rb, ones_bd):
    n = xa.shape[0]
    assert ctx_len == ROW_BLOCK
    sub = ROW_BLOCK // 8
    last = n // 8 - 1
    full = lambda a: pl.BlockSpec(a.shape, lambda i: (0,) * a.ndim)
    row = lambda w: pl.BlockSpec((ROW_BLOCK, w), lambda i: (i, 0))
    mod = pl.BlockSpec((1, 1, D_MODEL), lambda i: (ent(i), 0, 0))
    return pl.pallas_call(
        functools.partial(_post_body, (ctx_len + seq) // ROW_BLOCK),
        grid=(n // ROW_BLOCK,),
        in_specs=[row(D_RWKV), row(D_RWKV), row(D_RWKV), row(D_RWKV), row(3 * D_CONV),
                  pl.BlockSpec((8, 3 * D_CONV), lambda i: (jnp.maximum(i * sub - 1, 0), 0)),
                  pl.BlockSpec((8, 3 * D_CONV), lambda i: (jnp.minimum((i + 1) * sub, last), 0)),
                  row(D_FOUR), row(D_MODEL), full(wo_bf), full(gnw), full(gnb), full(cw), full(cg),
                  mod, mod, mod, full(n2), full(wrh), full(wrl), full(rb), full(ones_bd)],
        out_specs=[row(D_MODEL), row(D_MODEL), row(LANES), pl.BlockSpec((1, LANES), lambda i: (0, 0))],
        out_shape=[jax.ShapeDtypeStruct((n, D_MODEL), F32), jax.ShapeDtypeStruct((n, D_MODEL), F32),
                   jax.ShapeDtypeStruct((n, LANES), F32), jax.ShapeDtypeStruct((1, LANES), F32)],
        scratch_shapes=[pltpu.VMEM((1, LANES), F32)],
        compiler_params=_params("arbitrary"),
        name="post_mix",
    )(yf, yb, bonus, g, pcv, pcv, pcv, four_o, xa, wo_bf, gnw, gnb, cw, cg, g1, sc2, sh2, n2, wrh, wrl, rb,
      ones_bd)


def _scatter_rows_body(zb_ref, zv_ref, dest_ref, h_ref, xb_ref, zero_buf, sem):
    tb = h_ref.shape[0]

    @pl.when(pl.program_id(0) == 0)
    def _():
        zero_buf[...] = jnp.zeros_like(zero_buf)

        def block_fill(j):
            start = pl.multiple_of(zb_ref[j] * MOE_ROWS, MOE_ROWS)
            return pltpu.make_async_copy(zero_buf, xb_ref.at[pl.ds(start, MOE_ROWS), :], sem)

        def fill_start(j, carry):
            @pl.when(zv_ref[j] != 0)
            def _():
                block_fill(j).start()
            return carry

        def fill_wait(j, carry):
            @pl.when(zv_ref[j] != 0)
            def _():
                block_fill(j).wait()
            return carry

        lax.fori_loop(0, zb_ref.shape[0], fill_start, 0)
        lax.fori_loop(0, zb_ref.shape[0], fill_wait, 0)

    def row_copy(t, k):
        return pltpu.make_async_copy(h_ref.at[pl.ds(t, 1), :],
                                     xb_ref.at[pl.ds(dest_ref[0, 0, 2 * t + k], 1), :], sem)

    def issue(t, carry):
        row_copy(t, 0).start()
        row_copy(t, 1).start()
        return carry

    def drain(t, carry):
        row_copy(t, 0).wait()
        row_copy(t, 1).wait()
        return carry

    for t in range(tb):
        issue(t, 0)
    lax.fori_loop(0, tb, drain, 0, unroll=DMA_UNROLL)


DMA_UNROLL = 8
SCATTER_ROWS = 512


def _scatter_rows(h2, dest, zero_blocks, zero_valid, cap):
    n = h2.shape[0]
    steps = n // SCATTER_ROWS
    grid_spec = pltpu.PrefetchScalarGridSpec(
        num_scalar_prefetch=2,
        grid=(steps,),
        in_specs=[pl.BlockSpec((1, 1, 2 * SCATTER_ROWS), lambda i, zb, zv: (i, 0, 0), memory_space=pltpu.SMEM),
                  pl.BlockSpec((SCATTER_ROWS, D_MODEL), lambda i, zb, zv: (i, 0))],
        out_specs=pl.BlockSpec(memory_space=pl.ANY),
        scratch_shapes=[pltpu.VMEM((MOE_ROWS, D_MODEL), F32), pltpu.SemaphoreType.DMA(())],
    )
    return pl.pallas_call(
        _scatter_rows_body,
        grid_spec=grid_spec,
        out_shape=jax.ShapeDtypeStruct((cap, D_MODEL), F32),
        compiler_params=_params("arbitrary", disable_bounds_checks=True),
        name="moe_scatter",
    )(zero_blocks, zero_valid, dest.reshape(steps, 1, 2 * SCATTER_ROWS), h2)


def _expert_body(be_ref, nu_ref, xb_ref, wg_ref, wu_ref, wd_ref, o_ref, wg_s, wu_s, wd_s):
    i = pl.program_id(0)
    prev = be_ref[jnp.maximum(i - 1, 0)]
    fresh = jnp.logical_or(i == 0, be_ref[i] != prev)

    @pl.when(jnp.logical_and(fresh, i < nu_ref[0]))
    def _():
        wg_s[...] = wg_ref[0, 0].astype(BF16)
        wu_s[...] = wu_ref[0, 0].astype(BF16)
        wd_s[...] = wd_ref[0, 0].astype(BF16)

    @pl.when(i < nu_ref[0])
    def _():
        xb = xb_ref[...].astype(BF16)
        gate = _dot(xb, wg_s[...])
        up = _dot(xb, wu_s[...])
        act = gate * _sigmoid(gate) * up
        o_ref[...] = _dot(act.astype(BF16), wd_s[...])

    @pl.when(i >= nu_ref[0])
    def _():
        o_ref[...] = jnp.zeros_like(o_ref)


def _experts(xb, block_e, n_used, exp_gate, exp_up, exp_down, layer):
    cap = xb.shape[0]
    nb = cap // MOE_ROWS
    blk = lambda i, nu: jnp.minimum(i, nu[0] - 1)
    grid_spec = pltpu.PrefetchScalarGridSpec(
        num_scalar_prefetch=2,
        grid=(nb,),
        in_specs=[pl.BlockSpec((MOE_ROWS, D_MODEL), lambda i, be, nu: (blk(i, nu), 0)),
                  pl.BlockSpec((1, 1, D_MODEL, D_EXPERT), lambda i, be, nu: (layer, be[blk(i, nu)], 0, 0)),
                  pl.BlockSpec((1, 1, D_MODEL, D_EXPERT), lambda i, be, nu: (layer, be[blk(i, nu)], 0, 0)),
                  pl.BlockSpec((1, 1, D_EXPERT, D_MODEL), lambda i, be, nu: (layer, be[blk(i, nu)], 0, 0))],
        out_specs=pl.BlockSpec((MOE_ROWS, D_MODEL), lambda i, be, nu: (i, 0)),
        scratch_shapes=[pltpu.VMEM((D_MODEL, D_EXPERT), BF16), pltpu.VMEM((D_MODEL, D_EXPERT), BF16),
                        pltpu.VMEM((D_EXPERT, D_MODEL), BF16)],
    )
    return pl.pallas_call(
        _expert_body,
        grid_spec=grid_spec,
        out_shape=jax.ShapeDtypeStruct((cap, D_MODEL), F32),
        compiler_params=_params("arbitrary"),
        name="experts",
    )(block_e, n_used, xb, exp_gate, exp_up, exp_down)


def _dispatch(route, counts):
    n = route.shape[0]
    eid = route[:, 2:4].astype(jnp.int32)
    rank = route[:, 4:6].astype(jnp.int32)
    counts = counts[0, :N_EXPERTS].astype(jnp.int32)
    pcounts = (counts + MOE_ROWS - 1) // MOE_ROWS * MOE_ROWS
    pend = jnp.cumsum(pcounts)
    pstart = pend - pcounts
    experts = jnp.arange(N_EXPERTS, dtype=jnp.int32)
    dest = jnp.sum(jnp.where(eid[:, :, None] == experts, pstart, 0), axis=-1) + rank
    nb = -(-2 * n // MOE_ROWS) + N_EXPERTS
    block_start = jnp.arange(nb, dtype=jnp.int32) * MOE_ROWS
    block_e = jnp.minimum(jnp.sum((pend[None, :] <= block_start[:, None]).astype(jnp.int32), axis=1),
                          N_EXPERTS - 1)
    n_used = pend[-1] // MOE_ROWS
    spare = n_used + experts
    zero_blocks = jnp.concatenate([pend // MOE_ROWS - 1, spare]).astype(jnp.int32)
    zero_valid = jnp.concatenate([pcounts > 0, spare < nb]).astype(jnp.int32)
    return (dest.reshape(n // ROW_BLOCK, 1, 2 * ROW_BLOCK), block_e, n_used.astype(jnp.int32).reshape(1),
            zero_blocks, zero_valid, nb * MOE_ROWS)


def _combine_body(final_norm, dest_ref, next_ref, x_ref, route_ref, g2_ref, fg_ref, yb_ref, o_ref, buf, sems):
    i = pl.program_id(0)
    tb = x_ref.shape[0]
    slot = i % 2

    def gather(idx_ref, s, wait, unrolled=False):
        def row_copy(t, k):
            return pltpu.make_async_copy(yb_ref.at[pl.ds(idx_ref[0, 0, 2 * t + k], 1), :],
                                         buf.at[s, k, pl.ds(t, 1), :], sems.at[s])

        def body(t, carry):
            for k in range(2):
                if wait:
                    row_copy(t, k).wait()
                else:
                    row_copy(t, k).start()
            return carry

        if unrolled:
            for t in range(tb):
                body(t, 0)
        else:
            lax.fori_loop(0, tb, body, 0, unroll=DMA_UNROLL)

    @pl.when(i == 0)
    def _():
        gather(dest_ref, slot, False)

    for s in range(2):
        @pl.when(jnp.logical_and(i + 1 < pl.num_programs(0), slot == 1 - s))
        def _():
            gather(next_ref, s, False, unrolled=True)

    gather(dest_ref, slot, True)
    route = route_ref[...]
    x = x_ref[...] + g2_ref[0] * (route[:, 0:1] * buf[slot, 0] + route[:, 1:2] * buf[slot, 1])
    if final_norm:
        x = _rms(x) * fg_ref[...]
    o_ref[...] = x


def _combine(xa, yb, route, dest3, g2, final_g, blk, ent, n_out_blocks, final_norm):
    nxt = lambda i: blk(jnp.minimum(i + 1, n_out_blocks - 1))
    return pl.pallas_call(
        functools.partial(_combine_body, final_norm),
        grid=(n_out_blocks,),
        in_specs=[pl.BlockSpec((1, 1, 2 * ROW_BLOCK), lambda i: (blk(i), 0, 0), memory_space=pltpu.SMEM),
                  pl.BlockSpec((1, 1, 2 * ROW_BLOCK), lambda i: (nxt(i), 0, 0), memory_space=pltpu.SMEM),
                  pl.BlockSpec((ROW_BLOCK, D_MODEL), lambda i: (blk(i), 0)),
                  pl.BlockSpec((ROW_BLOCK, LANES), lambda i: (blk(i), 0)),
                  pl.BlockSpec((1, 1, D_MODEL), lambda i: (ent(blk(i)), 0, 0)),
                  pl.BlockSpec(final_g.shape, lambda i: (0, 0)),
                  pl.BlockSpec(memory_space=pl.ANY)],
        out_specs=pl.BlockSpec((ROW_BLOCK, D_MODEL), lambda i: (i, 0)),
        out_shape=jax.ShapeDtypeStruct((n_out_blocks * ROW_BLOCK, D_MODEL), F32),
        scratch_shapes=[pltpu.VMEM((2, 2, ROW_BLOCK, D_MODEL), F32), pltpu.SemaphoreType.DMA((2,))],
        compiler_params=_params("arbitrary", disable_bounds_checks=True),
        name="moe_combine",
    )(dest3, dest3, xa, route, g2, final_g, yb)


def _block_diag2(w):
    k, n = w.shape[1], w.shape[2]
    z = jnp.zeros((k, n), w.dtype)
    return jnp.concatenate([jnp.concatenate([w[0], z], axis=1), jnp.concatenate([z, w[1]], axis=1)], axis=0)


def kernel(x, c, ctx, c_ctx, ada_w, ada_b, norm1_g, norm2_g, w_in, mu_shift, decay_w0, decay_w2, iclr_a0, iclr_a2, gate_g2, k_k, k_a, r_k, gn_w, gn_b, conv_w, conv_gain, four_gain, w_out, router_g_w, router_g_b, router_e_w, router_e_b, exp_gate, exp_up, exp_down, final_g):
    bsz, seq, d = x.shape
    ctx_len = ctx.shape[1]
    depth = ada_w.shape[0]
    rows_b = ctx_len + seq
    n = bsz * rows_b
    bpb = rows_b // ROW_BLOCK
    lat_bpb = seq // ROW_BLOCK
    ent = lambda i: jnp.where(i % bpb == 0, bsz, i // bpb)

    xa = jnp.concatenate([ctx, x], axis=1).reshape(n, d)
    cvec = jnp.concatenate([c, c_ctx[None, :]], axis=0)
    cvec = cvec * jax.nn.sigmoid(cvec)
    head_id = np.arange(LANES) // HEAD
    ones_bd = jnp.asarray(head_id[:, None] == head_id[None, :], BF16)
    row2 = lambda a: a.reshape(1, -1)

    for l in range(depth):
        last = l == depth - 1
        mod = jnp.dot(cvec, ada_w[l], precision=lax.Precision.HIGHEST) + ada_b[l]
        sh1, sc1, g1, sh2, sc2, g2 = [mod[:, j * d:(j + 1) * d].reshape(bsz + 1, 1, d) for j in range(6)]

        pz, pcv, pfo = _inproj(xa, row2(norm1_g[l]), sc1, sh1, w_in[l].astype(BF16), ent)
        g, bonus, pm, qm, rp, y0 = _rwkv_chunks(
            pz, seq, ctx_len, bsz, row2(mu_shift[l]), _block_diag2(decay_w2[l]), _block_diag2(iclr_a2[l]),
            gate_g2[l], row2(decay_w0[l]), row2(iclr_a0[l]), row2(k_k[l]), row2(k_a[l]), row2(r_k[l]), ones_bd)
        yf, yb = _chunk_scan(pm, qm, rp, y0, seq, ctx_len, bsz)

        fgain = row2(four_gain[l])
        pfo3 = pfo.reshape(bsz, rows_b, D_FOUR)
        four_o = jnp.concatenate([_fourier_direct(pfo3[:, :ctx_len], fgain),
                                  _fourier_long(pfo3[:, ctx_len:], fgain)], axis=1).reshape(n, D_FOUR)

        wr = jnp.zeros((d, LANES), F32).at[:, :N_GROUPS].set(router_g_w[l])
        wr = wr.at[:, N_GROUPS:N_GROUPS + N_EXPERTS].set(router_e_w[l])
        rb = jnp.zeros((1, LANES), F32).at[0, :N_GROUPS].set(router_g_b[l])
        rb = rb.at[0, N_GROUPS:N_GROUPS + N_EXPERTS].set(router_e_b[l])
        wrh, wrl = _hilo(wr)
        xa, h2, route, counts = _post(yf, yb, bonus, g, pcv, four_o, xa, seq, ctx_len, ent, w_out[l].astype(BF16),
                                      row2(gn_w[l]), row2(gn_b[l]), conv_w[l], row2(conv_gain[l]), g1, sc2, sh2,
                                      row2(norm2_g[l]), wrh, wrl, rb, ones_bd)

        dest3, block_e, n_used, zero_blocks, zero_valid, cap = _dispatch(route, counts)
        xe = _scatter_rows(h2, dest3, zero_blocks, zero_valid, cap)
        ye = _experts(xe, block_e, n_used, exp_gate, exp_up, exp_down, l)
        if last:
            blk = lambda i: (i // lat_bpb) * bpb + 1 + i % lat_bpb
            xa = _combine(xa, ye, route, dest3, g2, row2(final_g), blk, ent, bsz * lat_bpb, True)
        else:
            xa = _combine(xa, ye, route, dest3, g2, row2(final_g), lambda i: i, ent, n // ROW_BLOCK, False)

    return xa.reshape(bsz, seq, d)
```

```python
import functools

import numpy as np
import jax
import jax.numpy as jnp
from jax import lax
from jax.experimental import pallas as pl
from jax.experimental.pallas import tpu as pltpu

F32 = jnp.float32
BF16 = jnp.bfloat16

D_MODEL = 1024
HEAD = 64
D_RWKV = 512
H_RWKV = D_RWKV // HEAD
D_CONV = 256
D_FOUR = 256
FOUR_GROUP = 64
D_Z = 3 * D_RWKV + 2 * 64 + 2 * 64 + 128
D_IN = D_Z + 3 * D_CONV + D_FOUR
GRID_W = 64
N_GROUPS = 4
EXPERTS_PER_GROUP = 8
N_EXPERTS = N_GROUPS * EXPERTS_PER_GROUP
D_EXPERT = D_MODEL // 2
RMS_EPS = 1e-6
GN_EPS = 64e-5

CHUNK = 64
ROW_BLOCK = 256
MOE_ROWS = 256
FOUR_INNER = 128
FOUR_GROUP_STEP = 4
LANES = 128
VMEM_LIMIT = 48 * 1024 * 1024

NN = (((1,), (0,)), ((), ()))
NT = (((1,), (1,)), ((), ()))
TN = (((0,), (0,)), ((), ()))


def _params(*sem, **kw):
    return pltpu.CompilerParams(dimension_semantics=sem, vmem_limit_bytes=VMEM_LIMIT, **kw)


def _split2(x):
    hi = x.astype(BF16)
    lo = (x - hi.astype(F32)).astype(BF16)
    return hi, lo


def _dot(a, b, dims=NN):
    return lax.dot_general(a, b, dims, preferred_element_type=F32)


def _dot1(a, b, dims=NN):
    return _dot(a.astype(BF16), b.astype(BF16), dims)


def _dot3(a, b, dims=NN):
    ah, al = _split2(a)
    bh, bl = _split2(b)
    return _dot(ah, bh, dims) + (_dot(ah, bl, dims) + _dot(al, bh, dims))


def _dot3c(ch, cl, x, dims=NN):
    xh, xl = _split2(x)
    return _dot(ch, xh, dims) + (_dot(cl, xh, dims) + _dot(ch, xl, dims))


def _dot3r(x, ch, cl, dims=NN):
    xh, xl = _split2(x)
    return _dot(xh, ch, dims) + (_dot(xl, ch, dims) + _dot(xh, cl, dims))


def _headsums(xs, ones_pair):
    rows = xs[0].shape[0]
    tiles = xs[0].shape[1] // LANES
    parts = []
    for x in xs:
        for part in _split2(x):
            parts.extend(part[:, t * LANES:(t + 1) * LANES] for t in range(tiles))
    s = _dot(jnp.concatenate(parts, axis=0), ones_pair)
    outs = []
    for j in range(len(xs)):
        base = j * 2 * tiles
        outs.append(jnp.concatenate(
            [s[(base + t) * rows:(base + t + 1) * rows] + s[(base + tiles + t) * rows:(base + tiles + t + 1) * rows]
             for t in range(tiles)], axis=1))
    return outs


def _rms(x, eps=RMS_EPS):
    return x * lax.rsqrt(jnp.mean(x * x, axis=-1, keepdims=True) + eps)


def _sigmoid(x):
    return 1.0 / (1.0 + jnp.exp(-x))


def _softplus(x):
    return jnp.maximum(x, 0.0) + jnp.log(1.0 + jnp.exp(-jnp.abs(x)))


def _inproj_body(blocks_per_batch, x_ref, g_ref, sc_ref, sh_ref, w_ref, z_ref, cv_ref, fx_ref, fc_ref):
    h = _rms(x_ref[...]) * g_ref[...] * (1.0 + sc_ref[0]) + sh_ref[0]
    p = _dot(h.astype(BF16), w_ref[...])
    z_ref[...] = p[:, :D_Z]
    cv_ref[...] = p[:, D_Z:D_Z + 3 * D_CONV]
    is_ctx = pl.program_id(0) % blocks_per_batch == 0

    @pl.when(is_ctx)
    def _():
        fc_ref[...] = p[:, D_Z + 3 * D_CONV:]

    @pl.when(jnp.logical_not(is_ctx))
    def _():
        fx_ref[...] = p[:, D_Z + 3 * D_CONV:]


def _latent_block(i, bpb):
    return (i // bpb) * (bpb - 1) + jnp.maximum(i % bpb - 1, 0)


def _inproj(xa, gain, sc, sh, w_bf, ent, bsz, bpb):
    n = xa.shape[0]
    row = lambda w: pl.BlockSpec((ROW_BLOCK, w), lambda i: (i, 0))
    full = lambda a: pl.BlockSpec(a.shape, lambda i: (0,) * a.ndim)
    mod = pl.BlockSpec((1, 1, D_MODEL), lambda i: (ent(i), 0, 0))
    return pl.pallas_call(
        functools.partial(_inproj_body, bpb),
        grid=(n // ROW_BLOCK,),
        in_specs=[row(D_MODEL), full(gain), mod, mod, full(w_bf)],
        out_specs=[row(D_Z), row(3 * D_CONV),
                   pl.BlockSpec((ROW_BLOCK, D_FOUR), lambda i: (_latent_block(i, bpb), 0)),
                   pl.BlockSpec((ROW_BLOCK, D_FOUR), lambda i: (i // bpb, 0))],
        out_shape=[jax.ShapeDtypeStruct((n, D_Z), F32), jax.ShapeDtypeStruct((n, 3 * D_CONV), F32),
                   jax.ShapeDtypeStruct((n - bsz * ROW_BLOCK, D_FOUR), F32),
                   jax.ShapeDtypeStruct((bsz * ROW_BLOCK, D_FOUR), F32)],
        compiler_params=_params("arbitrary"),
        name="inproj",
    )(xa, gain, sc, sh, w_bf)


def _streams_steps(res, i, blocks_per_batch, zm_ref, zp_ref, zn_ref, mu_ref, w2_ref, a2_ref, g2_ref,
                   w0_ref, a0_ref, kkw_ref, ka_ref, rk_ref, ones_ref):
    seq_pos = i % blocks_per_batch
    ctx_i = (seq_pos == 0).astype(jnp.int32)
    tb = zm_ref.shape[0]
    t = lax.broadcasted_iota(jnp.int32, (tb, 1), 0)
    ctx_v = jnp.zeros((tb, 1), jnp.int32) + ctx_i
    col = t & (GRID_W - 1)
    lmask = (col != 0) | ((ctx_v != 0) & (t != 0))
    rmask = (col != GRID_W - 1) | ((ctx_v != 0) & (t != tb - 1))
    top_v = jnp.zeros((tb, 1), jnp.int32) + (seq_pos == 1).astype(jnp.int32)
    bot_v = jnp.zeros((tb, 1), jnp.int32) + (seq_pos == blocks_per_batch - 1).astype(jnp.int32)
    umask = jnp.logical_not((top_v != 0) & (t < GRID_W))
    dmask = jnp.logical_not((bot_v != 0) & (t >= tb - GRID_W))
    lane_mask = jnp.where(ctx_i != 0, 1, 3)
    tiles = []
    for c0 in range(0, D_Z, LANES):
        cols = slice(c0, c0 + LANES)
        z = zm_ref[:, cols]
        left = jnp.where(lmask, pltpu.roll(z, 1, axis=0), 0.0)
        right = jnp.where(rmask, pltpu.roll(z, tb - 1, axis=0), 0.0)
        up = jnp.where(umask, jnp.concatenate([zp_ref[:, cols], z[:tb - GRID_W]], axis=0), 0.0)
        down = jnp.where(dmask, jnp.concatenate([z[GRID_W:], zn_ref[:, cols]], axis=0), 0.0)
        q = (lax.broadcasted_iota(jnp.int32, (1, LANES), 1) + c0) & lane_mask
        shifted = jnp.where(q == 0, left, jnp.where(q == 1, right, jnp.where(q == 2, up, down)))
        tiles.append(z + (shifted - z) * mu_ref[:, cols])
        yield
    per = D_RWKV // LANES
    r = jnp.concatenate(tiles[0:per], axis=1)
    k = jnp.concatenate(tiles[per:2 * per], axis=1)
    v = jnp.concatenate(tiles[2 * per:3 * per], axis=1)
    lw_in, la_in, lg = tiles[3 * per], tiles[3 * per + 1], tiles[3 * per + 2]

    g = _dot3(_sigmoid(lg), g2_ref[...])
    yield
    w_log = -_softplus(-(w0_ref[...] + _dot3(jnp.tanh(lw_in), w2_ref[...]))) - 0.5
    lw = -jnp.exp(w_log)
    yield
    a = _sigmoid(a0_ref[...] + _dot3(la_in, a2_ref[...]))
    yield
    ka = ka_ref[...]
    a_d = [a[:, d * D_RWKV:(d + 1) * D_RWKV] for d in range(2)]
    k_d = [k * (1.0 + (x - 1.0) * ka) for x in a_d]
    lw_d = [lw[:, d * D_RWKV:(d + 1) * D_RWKV] for d in range(2)]
    kq = k * kkw_ref[...]
    yield
    kq_ss, rk_sum = _headsums([kq * kq, r * (k_d[0] + k_d[1]) * rk_ref[...]], ones_ref[...])
    yield
    kk = kq / jnp.maximum(jnp.sqrt(kq_ss), 1e-12)
    b_d = [kk * x for x in a_d]
    res.update(r=r, v=v, kk=kk, g=g, bonus=rk_sum * v, lw_d=lw_d, k_d=k_d, b_d=b_d)


def _chunk_maps(r, v, kk, lw_d, k_d, b_d, p_out, q_out, rp_out, y0_out):
    cs = CHUNK
    rows = r.shape[0]
    n_chunks = rows // cs
    n_pairs = D_RWKV // LANES

    ri = lax.broadcasted_iota(jnp.int32, (rows, rows), 0)
    rj = lax.broadcasted_iota(jnp.int32, (rows, rows), 1)
    same_chunk = (ri >> 6) == (rj >> 6)
    ti = lax.broadcasted_iota(jnp.int32, (cs, LANES), 0)
    lane = lax.broadcasted_iota(jnp.int32, (cs, LANES), 1)
    tj = lane & (HEAD - 1)
    lo_half = lane < HEAD
    eye = ti == tj
    eye_f = eye.astype(F32)
    zero_bf = jnp.zeros((cs, LANES), BF16)
    strict, incl = [], []
    at, bt, kt, rt, bh, kh, e_tot = [], [], [], [], [], [], []
    for d in range(2):
        sgn = 1 if d == 0 else -1
        order = (ti - tj) * sgn
        strict.append(order > 0)
        incl.append(order >= 0)
        t_mat = jnp.concatenate([(same_chunk & ((ri - rj) * sgn >= 0)).astype(F32).astype(BF16),
                                 same_chunk.astype(F32).astype(BF16)], axis=0)
        lw = lw_d[d]
        l1 = lw.astype(BF16)
        rem = lw - l1.astype(F32)
        l2 = rem.astype(BF16)
        l3 = (rem - l2.astype(F32)).astype(BF16)
        gsum = _dot(t_mat, l1) + (_dot(t_mat, l2) + _dot(t_mat, l3))
        gcum, gtot = gsum[:rows], gsum[rows:]
        e_neg = jnp.exp(-gcum)
        e_rem = jnp.exp(gtot - gcum)
        at.append(-(kk * jnp.exp(gcum - lw)))
        bt.append(b_d[d] * e_neg)
        kt.append(k_d[d] * e_neg)
        rt.append(r * jnp.exp(gcum))
        bh.append(b_d[d] * e_rem)
        kh.append(k_d[d] * e_rem)
        e_tot.append(jnp.exp(gtot))
        yield

    def bd(y):
        y = y.astype(BF16)
        return jnp.concatenate([jnp.where(lo_half, y, zero_bf), jnp.where(lo_half, zero_bf, y)], axis=0)

    def mm(x, y_bd, dims=NN):
        return _dot(x.astype(BF16), y_bd, dims)

    tile = lambda a, c, p: a[c * cs:(c + 1) * cs, p * LANES:(p + 1) * LANES]
    for d in range(2):
        yield from _chunk_stages(d, [(d, c, p) for c in range(n_chunks) for p in range(n_pairs)], tile, bd, mm,
                                 strict, incl, eye, eye_f, lo_half, ti, tj, at, bt, kt, rt, bh, kh, e_tot, v,
                                 p_out, q_out, rp_out, y0_out)


def _chunk_stages(d, chains, tile, bd, mm, strict, incl, eye, eye_f, lo_half, ti, tj, at, bt, kt, rt, bh, kh,
                  e_tot, v, p_out, q_out, rp_out, y0_out):
    cs = CHUNK
    n_pairs = D_RWKV // LANES
    n_chunks = len(chains) // n_pairs
    s = [mm(jnp.concatenate([tile(at[d], c, p), tile(rt[d], c, p)], axis=0),
            jnp.concatenate([bd(tile(bt[d], c, p)), bd(tile(kt[d], c, p))], axis=0), NT) for d, c, p in chains]
    yield
    a_ab = [jnp.where(strict[d], x[:cs, :LANES], 0.0) for x, (d, _, _) in zip(s, chains)]
    a_ak = [jnp.where(strict[d], x[:cs, LANES:], 0.0) for x, (d, _, _) in zip(s, chains)]
    a_rb = [jnp.where(incl[d], x[cs:, :LANES], 0.0) for x, (d, _, _) in zip(s, chains)]
    a_rk = [jnp.where(incl[d], x[cs:, LANES:], 0.0) for x, (d, _, _) in zip(s, chains)]
    a0 = [jnp.where((ti >> 3) == (tj >> 3), x, 0.0) for x in a_ab]
    a2 = [mm(x, bd(x)) for x in a0]
    yield
    a2_bd = [bd(x) for x in a2]
    x0 = [eye_f + x for x in a0]
    x1 = [x + mm(x, y) for x, y in zip(x0, a2_bd)]
    yield
    a4 = [mm(x, y) for x, y in zip(a2, a2_bd)]
    yield
    minv = [x + mm(x, bd(y)) for x, y in zip(x1, a4)]
    yield
    lvl = 3
    while (1 << lvl) < cs:
        off = ((ti >> (lvl + 1)) == (tj >> (lvl + 1))) & ((ti >> lvl) != (tj >> lvl))
        t = [mm(jnp.where(off, x, 0.0), bd(y)) for x, y in zip(a_ab, minv)]
        yield
        minv = [x + mm(x, bd(y)) for x, y in zip(minv, t)]
        yield
        lvl += 1
    v_bd = [bd(tile(v, c, p)) for _, c, p in chains]
    avk = [mm(jnp.concatenate([x, z], axis=0), y) for x, z, y in zip(a_ak, a_rk, v_bd)]
    yield
    av = [x[:cs] for x in avk]
    bot_k = [x[cs:] for x in avk]
    wu = [mm(m, jnp.concatenate([bd(tile(at[d], c, p)), bd(y)], axis=1))
          for m, y, (d, c, p) in zip(minv, av, chains)]
    yield
    top_b = [mm(tile(bh[d], c, p), x.astype(BF16), TN) for x, (d, c, p) in zip(wu, chains)]
    yield
    top_k = [mm(tile(kh[d], c, p), tile(v, c, p).astype(BF16), TN) for d, c, p in chains]
    yield
    bot_b = [mm(x, jnp.concatenate([bd(y[:, :LANES]), bd(y[:, LANES:])], axis=1)) for x, y in zip(a_rb, wu)]
    yield
    sel = lambda x, off: jnp.where(lo_half, x[:cs, off:off + LANES], x[cs:, off:off + LANES])
    for c in range(n_chunks):
        idx = [c * n_pairs + p for p in range(n_pairs)]
        p_out[d, c] = jnp.concatenate(
            [sel(top_b[i], 0) + jnp.where(eye, tile(e_tot[d], c, p), 0.0) for p, i in enumerate(idx)], axis=1)
        q_out[d, c] = jnp.concatenate([sel(top_b[i], LANES) + sel(top_k[i], 0) for i in idx], axis=1)
        rp_out[d, c * cs:(c + 1) * cs, :] = jnp.concatenate(
            [tile(rt[d], c, p) + bot_b[i][:, :LANES] for p, i in enumerate(idx)], axis=1)
        y0_out[d, c * cs:(c + 1) * cs, :] = jnp.concatenate(
            [bot_b[i][:, LANES:] + bot_k[i] for i in idx], axis=1)


def _rwkv_chunks_body(blocks_per_batch, zm_ref, zp_ref, zn_ref, mu_ref, w2_ref, a2_ref, g2_ref, w0_ref,
                      a0_ref, kkw_ref, ka_ref, rk_ref, ones_ref, g_out, bon_out, p_out, q_out, rp_out, y0_out,
                      ):
    res = {}
    for _ in _streams_steps(res, pl.program_id(0), blocks_per_batch, zm_ref, zp_ref, zn_ref, mu_ref, w2_ref, a2_ref,
                            g2_ref, w0_ref, a0_ref, kkw_ref, ka_ref, rk_ref, ones_ref):
        pass
    g_out[...] = res["g"]
    bon_out[...] = res["bonus"]
    for _ in _chunk_maps(res["r"], res["v"], res["kk"], res["lw_d"], res["k_d"], res["b_d"],
                         p_out, q_out, rp_out, y0_out):
        pass


def _rwkv_chunks(pz, seq, ctx_len, bsz, mu, w2bd, a2bd, g2, w0, a0, kkw, ka, rk, ones_bd):
    n = pz.shape[0]
    assert ctx_len == ROW_BLOCK and seq % ROW_BLOCK == 0 and n == bsz * (ctx_len + seq)
    sub = ROW_BLOCK // GRID_W
    last = n // GRID_W - 1
    cps = ROW_BLOCK // CHUNK
    nblk = n // ROW_BLOCK
    full = lambda a: pl.BlockSpec(a.shape, lambda i: (0,) * a.ndim)
    row = pl.BlockSpec((ROW_BLOCK, D_RWKV), lambda i: (i, 0))
    row2 = pl.BlockSpec((2, ROW_BLOCK, D_RWKV), lambda i: (0, i, 0))
    mat = pl.BlockSpec((2, cps, HEAD, D_RWKV), lambda i: (0, i, 0, 0))
    s1 = jax.ShapeDtypeStruct((n, D_RWKV), F32)
    s2 = jax.ShapeDtypeStruct((2, n, D_RWKV), F32)
    sm = jax.ShapeDtypeStruct((2, n // CHUNK, HEAD, D_RWKV), F32)
    consts = (mu, w2bd, a2bd, g2, w0, a0, kkw, ka, rk, ones_bd)
    return pl.pallas_call(
        functools.partial(_rwkv_chunks_body, (ctx_len + seq) // ROW_BLOCK),
        grid=(nblk,),
        in_specs=[pl.BlockSpec((ROW_BLOCK, D_Z), lambda i: (i, 0)),
                  pl.BlockSpec((GRID_W, D_Z), lambda i: (jnp.maximum(i * sub - 1, 0), 0)),
                  pl.BlockSpec((GRID_W, D_Z), lambda i: (jnp.minimum((i + 1) * sub, last), 0))]
                 + [full(a) for a in consts],
        out_specs=[row, row, mat, mat, row2, row2],
        out_shape=[s1, s1, sm, sm, s2, s2],
        compiler_params=_params("parallel"),
        name="rwkv_chunks",
    )(pz, pz, pz, *consts)


SCAN_CHUNKS = 4


def _scan_body(pf_ref, qf_ref, rpf_ref, y0f_ref, pb_ref, qb_ref, rpb_ref, y0b_ref, yf_out, yb_out, h_scr):
    @pl.when(pl.program_id(0) == 0)
    def _():
        h_scr[...] = jnp.zeros_like(h_scr)

    cs = CHUNK
    bsz = h_scr.shape[1]
    n_pairs = D_RWKV // LANES
    lane = lax.broadcasted_iota(jnp.int32, (cs, LANES), 1)
    lo_half = lane < HEAD
    zero_bf = jnp.zeros((cs, LANES), BF16)

    def bd(y):
        return jnp.concatenate([jnp.where(lo_half, y, zero_bf), jnp.where(lo_half, zero_bf, y)], axis=0)

    dirs = ((pf_ref, qf_ref, rpf_ref, y0f_ref, yf_out), (pb_ref, qb_ref, rpb_ref, y0b_ref, yb_out))
    chains = [(d, b, p) for d in range(2) for b in range(bsz) for p in range(n_pairs)]
    state = [h_scr[d, b, :, p * LANES:(p + 1) * LANES] for d, b, p in chains]
    for s in range(SCAN_CHUNKS):
        outs = []
        for (d, b, p), hcur in zip(chains, state):
            p_ref, _, rp_ref, _, _ = dirs[d]
            c = s if d == 0 else SCAN_CHUNKS - 1 - s
            ls = slice(p * LANES, (p + 1) * LANES)
            x = jnp.concatenate([p_ref[0, b, c, :, ls], rp_ref[0, b, c * cs:(c + 1) * cs, ls]], axis=0)
            xh, xl = _split2(x)
            hh, hl = _split2(hcur)
            hh, hl = bd(hh), bd(hl)
            outs.append(_dot(xh, hh) + (_dot(xh, hl) + _dot(xl, hh)))
        new_state = []
        for (d, b, p), o in zip(chains, outs):
            _, q_ref, _, y0_ref, y_out = dirs[d]
            c = s if d == 0 else SCAN_CHUNKS - 1 - s
            ls = slice(p * LANES, (p + 1) * LANES)
            new_state.append(o[:cs] + q_ref[0, b, c, :, ls])
            y_out[b, c * cs:(c + 1) * cs, ls] = y0_ref[0, b, c * cs:(c + 1) * cs, ls] + o[cs:]
        state = new_state
    for (d, b, p), hcur in zip(chains, state):
        h_scr[d, b, :, p * LANES:(p + 1) * LANES] = hcur


def _chunk_scan(p, q, rp, y0, seq, ctx_len, bsz):
    n = rp.shape[1]
    rows_b = ctx_len + seq
    ncb = rows_b // CHUNK
    assert ctx_len % (SCAN_CHUNKS * CHUNK) == 0 and seq % (SCAN_CHUNKS * CHUNK) == 0
    steps = ncb // SCAN_CHUNKS
    ctx_steps = ctx_len // (SCAN_CHUNKS * CHUNK)
    p5 = p.reshape(2, bsz, ncb, HEAD, D_RWKV)
    q5 = q.reshape(2, bsz, ncb, HEAD, D_RWKV)
    rp4 = rp.reshape(2, bsz, rows_b, D_RWKV)
    y04 = y0.reshape(2, bsz, rows_b, D_RWKV)
    pos_f = lambda i: i
    pos_b = lambda i: jnp.where(i < ctx_steps, ctx_steps - 1 - i, steps - 1 - (i - ctx_steps))
    rows = SCAN_CHUNKS * CHUNK

    def specs(d, pos):
        mat = pl.BlockSpec((1, bsz, SCAN_CHUNKS, HEAD, D_RWKV), lambda i: (d, 0, pos(i), 0, 0))
        tok = pl.BlockSpec((1, bsz, rows, D_RWKV), lambda i: (d, 0, pos(i), 0))
        return [mat, mat, tok, tok]

    out_f = pl.BlockSpec((bsz, rows, D_RWKV), lambda i: (0, pos_f(i), 0))
    out_b = pl.BlockSpec((bsz, rows, D_RWKV), lambda i: (0, pos_b(i), 0))
    shp = jax.ShapeDtypeStruct((bsz, rows_b, D_RWKV), F32)
    yf, yb = pl.pallas_call(
        _scan_body,
        grid=(steps,),
        in_specs=specs(0, pos_f) + specs(1, pos_b),
        out_specs=[out_f, out_b],
        out_shape=[shp, shp],
        scratch_shapes=[pltpu.VMEM((2, bsz, HEAD, D_RWKV), F32)],
        compiler_params=_params("arbitrary"),
        name="chunk_scan",
    )(p5, q5, rp4, y04, p5, q5, rp4, y04)
    return yf.reshape(n, D_RWKV), yb.reshape(n, D_RWKV)


def _hilo(a):
    if isinstance(a, np.ndarray):
        a = a.astype(np.float32)
        hi = a.astype(BF16)
        return jnp.asarray(hi), jnp.asarray((a - hi.astype(np.float32)).astype(BF16))
    hi = a.astype(BF16)
    return hi, (a - hi.astype(F32)).astype(BF16)


def _channel_tables(length):
    j = np.arange(FOUR_GROUP)
    ang = 2.0 * np.pi * np.outer(j, j) / FOUR_GROUP
    scale = 1.0 / np.sqrt(float(length) * FOUR_GROUP)
    groups = D_FOUR // FOUR_GROUP
    c4 = np.kron(np.eye(groups), np.cos(ang)) * scale
    s4 = np.kron(np.eye(groups), np.sin(ang)) * scale
    return _hilo(c4) + _hilo(s4)


def _four_finish(fr, fi, c4h, c4l, s4h, s4l, gain):
    y = _dot3r(fr, c4h, c4l) + _dot3r(fi, s4h, s4l)
    return _rms(y) * gain


def _four_direct_body(f_ref, mh_ref, ml_ref, c4h, c4l, s4h, s4l, gain_ref, o_ref):
    length = f_ref.shape[1]
    fc = _dot3c(mh_ref[...], ml_ref[...], f_ref[0])
    o_ref[0] = _four_finish(fc[:length], fc[length:], c4h[...], c4l[...], s4h[...], s4l[...], gain_ref[...])


def _fourier_direct(f, gain):
    bsz, length, _ = f.shape
    t = np.arange(length)
    ang = 2.0 * np.pi * (np.outer(t, t) % length) / length
    mh, ml = _hilo(np.concatenate([np.cos(ang), -np.sin(ang)], axis=0))
    consts = (mh, ml) + _channel_tables(length) + (gain,)
    full = lambda a: pl.BlockSpec(a.shape, lambda b: (0,) * a.ndim)
    blk = pl.BlockSpec((1, length, D_FOUR), lambda b: (b, 0, 0))
    return pl.pallas_call(
        _four_direct_body,
        grid=(bsz,),
        in_specs=[blk] + [full(a) for a in consts],
        out_specs=blk,
        out_shape=jax.ShapeDtypeStruct(f.shape, F32),
        compiler_params=_params("parallel"),
        name="fourier_direct",
    )(f, *consts)


def _four_stage1_body(f_ref, mh_ref, ml_ref, o_ref):
    o_ref[0] = _dot3c(mh_ref[...], ml_ref[...], f_ref[0])


def _four_stage2_body(zr_ref, zi_ref, mh_ref, ml_ref, c4h, c4l, s4h, s4l, gain_ref, o_ref):
    group, inner = zr_ref.shape[1], zr_ref.shape[2]
    fc = [_dot3c(mh_ref[j], ml_ref[j], jnp.concatenate([zr_ref[0, j], zi_ref[0, j]], axis=0))
          for j in range(group)]
    o_ref[0] = jnp.concatenate(
        [_four_finish(x[:inner], x[inner:], c4h[...], c4l[...], s4h[...], s4l[...], gain_ref[...]) for x in fc],
        axis=1)


def _fourier_long(f, gain):
    bsz, length, _ = f.shape
    l2 = FOUR_INNER
    l1 = length // l2
    assert l1 * l2 == length and l1 % 16 == 0
    cols = l2 * D_FOUR
    col_tile = 4096
    th = np.arange(l1)
    ang1 = 2.0 * np.pi * (np.outer(th, th) % l1) / l1
    m1h, m1l = _hilo(np.concatenate([np.cos(ang1), -np.sin(ang1)], axis=0))
    z = pl.pallas_call(
        _four_stage1_body,
        grid=(bsz, cols // col_tile),
        in_specs=[pl.BlockSpec((1, l1, col_tile), lambda b, c: (b, 0, c)),
                  pl.BlockSpec(m1h.shape, lambda b, c: (0, 0)),
                  pl.BlockSpec(m1l.shape, lambda b, c: (0, 0))],
        out_specs=pl.BlockSpec((1, 2 * l1, col_tile), lambda b, c: (b, 0, c)),
        out_shape=jax.ShapeDtypeStruct((bsz, 2 * l1, cols), F32),
        compiler_params=_params("parallel", "parallel"),
        name="fourier_stage1",
    )(f.reshape(bsz, l1, cols), m1h, m1l)
    z = z.reshape(bsz, 2 * l1, l2, D_FOUR)

    ma = np.arange(l1)[:, None, None]
    mb = np.arange(l2)[None, :, None]
    tl = np.arange(l2)[None, None, :]
    ang2 = 2.0 * np.pi * (((ma + l1 * mb) * tl) % length) / length
    cos2, sin2 = np.cos(ang2), np.sin(ang2)
    m2 = np.concatenate([np.concatenate([cos2, sin2], axis=2), np.concatenate([-sin2, cos2], axis=2)], axis=1)
    m2h, m2l = _hilo(m2)
    consts = _channel_tables(length) + (gain,)
    full = lambda a: pl.BlockSpec(a.shape, lambda b, m: (0,) * a.ndim)
    grp = FOUR_GROUP_STEP
    tab = pl.BlockSpec((grp, 2 * l2, 2 * l2), lambda b, m: (m, 0, 0))
    out = pl.pallas_call(
        _four_stage2_body,
        grid=(bsz, l1 // grp),
        in_specs=[pl.BlockSpec((1, grp, l2, D_FOUR), lambda b, m: (b, m, 0, 0)),
                  pl.BlockSpec((1, grp, l2, D_FOUR), lambda b, m: (b, l1 // grp + m, 0, 0)),
                  tab, tab] + [full(a) for a in consts],
        out_specs=pl.BlockSpec((1, l2, grp * D_FOUR), lambda b, m: (b, 0, m)),
        out_shape=jax.ShapeDtypeStruct((bsz, l2, l1 * D_FOUR), F32),
        compiler_params=_params("parallel", "parallel"),
        name="fourier_stage2",
    )(z, z, m2h, m2l, *consts)
    return out.reshape(bsz, length, D_FOUR)


def _route(logits):
    lane = lax.broadcasted_iota(jnp.int32, (1, LANES), 1)
    lane_f = lane.astype(F32)
    neg = jnp.float32(-1e30)
    big = jnp.float32(1e9)
    gl = jnp.where(lane < N_GROUPS, logits, neg)
    gmax = jnp.max(gl, axis=-1, keepdims=True)
    pg_top = 1.0 / jnp.sum(jnp.exp(gl - gmax), axis=-1, keepdims=True)
    grp = jnp.min(jnp.where(gl == gmax, lane_f, big), axis=-1, keepdims=True)
    e_lane = lane - N_GROUPS
    in_grp = (e_lane >= 0) & (e_lane < N_EXPERTS) & ((e_lane >> 3).astype(F32) == grp)
    el = jnp.where(in_grp, logits, neg)
    m1 = jnp.max(el, axis=-1, keepdims=True)
    i1 = jnp.min(jnp.where(el == m1, lane_f, big), axis=-1, keepdims=True)
    el2 = jnp.where(lane_f == i1, neg, el)
    m2 = jnp.max(el2, axis=-1, keepdims=True)
    i2 = jnp.min(jnp.where(el2 == m2, lane_f, big), axis=-1, keepdims=True)
    e2 = jnp.exp(m2 - m1)
    den = 1.0 + e2
    return pg_top / den, pg_top * e2 / den, i1 - N_GROUPS, i2 - N_GROUPS


def _post_body(blocks_per_batch, yf_ref, yb_ref, bon_ref, g_ref, cv_ref, cvp_ref, cvn_ref, fx_ref, fc_ref, x_ref,
               wo_ref, gnw_ref, gnb_ref, cw_ref, cg_ref, g1_ref, sc2_ref, sh2_ref, n2_ref, wrh_ref, wrl_ref,
               rb_ref, ones_ref, x_out, h_out, route_out, count_out, count_scr):
    i = pl.program_id(0)

    @pl.when(i == 0)
    def _():
        count_scr[...] = jnp.zeros_like(count_scr)

    ones_pair = ones_ref[...]
    y = yf_ref[...] + yb_ref[...]
    mu = _headsums([y], ones_pair)[0] * (1.0 / HEAD)
    yc = y - mu
    var = _headsums([yc * yc], ones_pair)[0] * (1.0 / HEAD)
    yn = yc * lax.rsqrt(var + GN_EPS) * gnw_ref[...] + gnb_ref[...]
    o_rwkv = (yn + bon_ref[...]) * g_ref[...]

    cv = cv_ref[...]
    tb = cv.shape[0]
    t = lax.broadcasted_iota(jnp.int32, (tb, 1), 0)
    seq_pos = i % blocks_per_batch
    first = (seq_pos <= 1).astype(F32)
    final = jnp.logical_or(seq_pos == 0, seq_pos == blocks_per_batch - 1).astype(F32)
    zc = cv[:, D_CONV:2 * D_CONV] * cv[:, 2 * D_CONV:]
    zp_row = cvp_ref[7:8, D_CONV:2 * D_CONV] * cvp_ref[7:8, 2 * D_CONV:] * (1.0 - first)
    zn_row = cvn_ref[0:1, D_CONV:2 * D_CONV] * cvn_ref[0:1, 2 * D_CONV:] * (1.0 - final)
    prev = jnp.where(t == 0, zp_row, pltpu.roll(zc, 1, axis=0))
    nxt = jnp.where(t == tb - 1, zn_row, pltpu.roll(zc, tb - 1, axis=0))
    cw = cw_ref[...]
    conv = cv[:, :D_CONV] * (cw[0:1] * prev + cw[1:2] * zc + cw[2:3] * nxt)
    conv_o = _rms(conv) * cg_ref[...]

    four_o = jnp.where(seq_pos == 0, fc_ref[...], fx_ref[...])
    mix = (_dot(o_rwkv.astype(BF16), wo_ref[0:D_RWKV, :])
           + _dot(conv_o.astype(BF16), wo_ref[D_RWKV:D_RWKV + D_CONV, :])
           + _dot(four_o.astype(BF16), wo_ref[D_RWKV + D_CONV:, :]))
    x = x_ref[...] + g1_ref[0] * mix
    x_out[...] = x
    h2 = _rms(x) * n2_ref[...] * (1.0 + sc2_ref[0]) + sh2_ref[0]
    h_out[...] = h2

    gate0, gate1, e0, e1 = _route(_dot3r(h2, wrh_ref[...], wrl_ref[...]) + rb_ref[...])
    lane = lax.broadcasted_iota(jnp.int32, (1, LANES), 1)
    lane_f = lane.astype(F32)
    oh0 = (lane_f == e0).astype(F32)
    oh1 = (lane_f == e1).astype(F32)
    ri = lax.broadcasted_iota(jnp.int32, (tb, tb), 0)
    rj = lax.broadcasted_iota(jnp.int32, (tb, tb), 1)
    earlier = (rj < ri).astype(F32).astype(BF16)
    seen = count_scr[...]
    tot0 = jnp.sum(oh0, axis=0, keepdims=True)
    before = _dot(earlier, jnp.concatenate([oh0, oh1], axis=1).astype(BF16))
    before0 = before[:, :LANES] + seen
    before1 = before[:, LANES:] + (seen + tot0)
    rank0 = jnp.sum(oh0 * before0, axis=-1, keepdims=True)
    rank1 = jnp.sum(oh1 * before1, axis=-1, keepdims=True)
    seen = seen + tot0 + jnp.sum(oh1, axis=0, keepdims=True)
    count_scr[...] = seen
    count_out[...] = seen
    route_out[...] = jnp.where(
        lane == 0, gate0, jnp.where(lane == 1, gate1, jnp.where(lane == 2, e0, jnp.where(
            lane == 3, e1, jnp.where(lane == 4, rank0, jnp.where(lane == 5, rank1, 0.0))))))


def _post(yf, yb, bonus, g, pcv, four_x, four_c, xa, seq, ctx_len, ent, wo_bf, gnw, gnb, cw, cg, g1, sc2, sh2, n2,
          wrh, wrl, rb, ones_bd):
    n = xa.shape[0]
    assert ctx_len == ROW_BLOCK
    sub = ROW_BLOCK // 8
    last = n // 8 - 1
    bpb = (ctx_len + seq) // ROW_BLOCK
    full = lambda a: pl.BlockSpec(a.shape, lambda i: (0,) * a.ndim)
    row = lambda w: pl.BlockSpec((ROW_BLOCK, w), lambda i: (i, 0))
    mod = pl.BlockSpec((1, 1, D_MODEL), lambda i: (ent(i), 0, 0))
    return pl.pallas_call(
        functools.partial(_post_body, (ctx_len + seq) // ROW_BLOCK),
        grid=(n // ROW_BLOCK,),
        in_specs=[row(D_RWKV), row(D_RWKV), row(D_RWKV), row(D_RWKV), row(3 * D_CONV),
                  pl.BlockSpec((8, 3 * D_CONV), lambda i: (jnp.maximum(i * sub - 1, 0), 0)),
                  pl.BlockSpec((8, 3 * D_CONV), lambda i: (jnp.minimum((i + 1) * sub, last), 0)),
                  pl.BlockSpec((ROW_BLOCK, D_FOUR), lambda i: (_latent_block(i, bpb), 0)),
                  pl.BlockSpec((ROW_BLOCK, D_FOUR), lambda i: (i // bpb, 0)),
                  row(D_MODEL), full(wo_bf), full(gnw), full(gnb), full(cw), full(cg),
                  mod, mod, mod, full(n2), full(wrh), full(wrl), full(rb), full(ones_bd)],
        out_specs=[row(D_MODEL), row(D_MODEL), row(LANES), pl.BlockSpec((1, LANES), lambda i: (0, 0))],
        out_shape=[jax.ShapeDtypeStruct((n, D_MODEL), F32), jax.ShapeDtypeStruct((n, D_MODEL), F32),
                   jax.ShapeDtypeStruct((n, LANES), F32), jax.ShapeDtypeStruct((1, LANES), F32)],
        scratch_shapes=[pltpu.VMEM((1, LANES), F32)],
        compiler_params=_params("arbitrary"),
        name="post_mix",
    )(yf, yb, bonus, g, pcv, pcv, pcv, four_x, four_c, xa, wo_bf, gnw, gnb, cw, cg, g1, sc2, sh2, n2, wrh, wrl,
      rb, ones_bd)


def _scatter_rows_body(zb_ref, zv_ref, dest_ref, h_ref, xb_ref, zero_buf, sem):
    tb = h_ref.shape[0]

    @pl.when(pl.program_id(0) == 0)
    def _():
        zero_buf[...] = jnp.zeros_like(zero_buf)

        def block_fill(j):
            start = pl.multiple_of(zb_ref[j] * MOE_ROWS, MOE_ROWS)
            return pltpu.make_async_copy(zero_buf, xb_ref.at[pl.ds(start, MOE_ROWS), :], sem)

        def fill_start(j, carry):
            @pl.when(zv_ref[j] != 0)
            def _():
                block_fill(j).start()
            return carry

        def fill_wait(j, carry):
            @pl.when(zv_ref[j] != 0)
            def _():
                block_fill(j).wait()
            return carry

        lax.fori_loop(0, zb_ref.shape[0], fill_start, 0)
        lax.fori_loop(0, zb_ref.shape[0], fill_wait, 0)

    def row_copy(t, k):
        return pltpu.make_async_copy(h_ref.at[pl.ds(t, 1), :],
                                     xb_ref.at[pl.ds(dest_ref[0, 0, 2 * t + k], 1), :], sem)

    def issue(t, carry):
        row_copy(t, 0).start()
        row_copy(t, 1).start()
        return carry

    def drain(t, carry):
        row_copy(t, 0).wait()
        row_copy(t, 1).wait()
        return carry

    for t in range(tb):
        issue(t, 0)
    lax.fori_loop(0, tb, drain, 0, unroll=DMA_UNROLL)


DMA_UNROLL = 8
SCATTER_ROWS = 512


def _scatter_rows(h2, dest, zero_blocks, zero_valid, cap):
    n = h2.shape[0]
    steps = n // SCATTER_ROWS
    grid_spec = pltpu.PrefetchScalarGridSpec(
        num_scalar_prefetch=2,
        grid=(steps,),
        in_specs=[pl.BlockSpec((1, 1, 2 * SCATTER_ROWS), lambda i, zb, zv: (i, 0, 0), memory_space=pltpu.SMEM),
                  pl.BlockSpec((SCATTER_ROWS, D_MODEL), lambda i, zb, zv: (i, 0))],
        out_specs=pl.BlockSpec(memory_space=pl.ANY),
        scratch_shapes=[pltpu.VMEM((MOE_ROWS, D_MODEL), F32), pltpu.SemaphoreType.DMA(())],
    )
    return pl.pallas_call(
        _scatter_rows_body,
        grid_spec=grid_spec,
        out_shape=jax.ShapeDtypeStruct((cap, D_MODEL), F32),
        compiler_params=_params("arbitrary", disable_bounds_checks=True),
        name="moe_scatter",
    )(zero_blocks, zero_valid, dest.reshape(steps, 1, 2 * SCATTER_ROWS), h2)


def _expert_body(layer, be_ref, meta_ref, nu_ref, xb_ref, wg_hbm, wu_hbm, wd_hbm, o_ref,
                 land_g, land_u, land_d, wg_s, wu_s, wd_s, sems):
    i = pl.program_id(0)
    used = i < nu_ref[0]
    meta = meta_ref[i]
    first = (meta & 1) != 0
    final = (meta & 2) != 0
    has_next = (meta & 4) != 0
    slot = (meta >> 3) & 1
    nxt = meta >> 4

    def fetch(e):
        return [pltpu.make_async_copy(w.at[layer, e], land, sems.at[k])
                for k, (w, land) in enumerate(((wg_hbm, land_g), (wu_hbm, land_u), (wd_hbm, land_d)))]

    def cast_to(s):
        wg_s[s] = land_g[...].astype(BF16)
        wu_s[s] = land_u[...].astype(BF16)
        wd_s[s] = land_d[...].astype(BF16)

    @pl.when(i == 0)
    def _():
        for c in fetch(be_ref[0]):
            c.start()
        for c in fetch(be_ref[0]):
            c.wait()
        cast_to(0)

    @pl.when(used & first & has_next)
    def _():
        for c in fetch(nxt):
            c.start()

    @pl.when(used)
    def _():
        xb = xb_ref[...].astype(BF16)
        gate = _dot(xb, wg_s[slot])
        up = _dot(xb, wu_s[slot])
        act = gate * _sigmoid(gate) * up
        o_ref[...] = _dot(act.astype(BF16), wd_s[slot])

    @pl.when(used & final & has_next)
    def _():
        for c in fetch(nxt):
            c.wait()
        cast_to(1 - slot)

    @pl.when(jnp.logical_not(used))
    def _():
        o_ref[...] = jnp.zeros_like(o_ref)


def _experts(xb, block_e, block_meta, n_used, exp_gate, exp_up, exp_down, layer):
    cap = xb.shape[0]
    nb = cap // MOE_ROWS
    blk = lambda i, nu: jnp.minimum(i, nu[0] - 1)
    grid_spec = pltpu.PrefetchScalarGridSpec(
        num_scalar_prefetch=3,
        grid=(nb,),
        in_specs=[pl.BlockSpec((MOE_ROWS, D_MODEL), lambda i, be, meta, nu: (blk(i, nu), 0)),
                  pl.BlockSpec(memory_space=pl.ANY), pl.BlockSpec(memory_space=pl.ANY),
                  pl.BlockSpec(memory_space=pl.ANY)],
        out_specs=pl.BlockSpec((MOE_ROWS, D_MODEL), lambda i, be, meta, nu: (i, 0)),
        scratch_shapes=[pltpu.VMEM((D_MODEL, D_EXPERT), F32), pltpu.VMEM((D_MODEL, D_EXPERT), F32),
                        pltpu.VMEM((D_EXPERT, D_MODEL), F32),
                        pltpu.VMEM((2, D_MODEL, D_EXPERT), BF16), pltpu.VMEM((2, D_MODEL, D_EXPERT), BF16),
                        pltpu.VMEM((2, D_EXPERT, D_MODEL), BF16), pltpu.SemaphoreType.DMA((3,))],
    )
    return pl.pallas_call(
        functools.partial(_expert_body, layer),
        grid_spec=grid_spec,
        out_shape=jax.ShapeDtypeStruct((cap, D_MODEL), F32),
        compiler_params=_params("arbitrary"),
        name="experts",
    )(block_e, block_meta, n_used, xb, exp_gate, exp_up, exp_down)


def _dispatch(route, counts):
    n = route.shape[0]
    eid = route[:, 2:4].astype(jnp.int32)
    rank = route[:, 4:6].astype(jnp.int32)
    counts = counts[0, :N_EXPERTS].astype(jnp.int32)
    pcounts = (counts + MOE_ROWS - 1) // MOE_ROWS * MOE_ROWS
    pend = jnp.cumsum(pcounts)
    pstart = pend - pcounts
    experts = jnp.arange(N_EXPERTS, dtype=jnp.int32)
    dest = jnp.sum(jnp.where(eid[:, :, None] == experts, pstart, 0), axis=-1) + rank
    nb = -(-2 * n // MOE_ROWS) + N_EXPERTS
    block_start = jnp.arange(nb, dtype=jnp.int32) * MOE_ROWS
    block_e = jnp.minimum(jnp.sum((pend[None, :] <= block_start[:, None]).astype(jnp.int32), axis=1),
                          N_EXPERTS - 1)
    n_used = pend[-1] // MOE_ROWS
    j = jnp.arange(nb, dtype=jnp.int32)
    used = j < n_used
    prev_e = jnp.concatenate([block_e[:1] - 1, block_e[:-1]])
    next_blk_e = jnp.concatenate([block_e[1:], block_e[-1:] + 1])
    first = used & (prev_e != block_e)
    final = used & ((j == n_used - 1) | (next_blk_e != block_e))
    later = (experts[None, :] > experts[:, None]) & (pcounts[None, :] > 0)
    next_expert = jnp.min(jnp.where(later, experts[None, :], N_EXPERTS), axis=1)
    nxt = next_expert[block_e]
    slot = (jnp.cumsum(first.astype(jnp.int32)) - 1) & 1
    block_meta = (first.astype(jnp.int32) | (final.astype(jnp.int32) << 1)
                  | ((nxt < N_EXPERTS).astype(jnp.int32) << 2) | (slot << 3)
                  | (jnp.minimum(nxt, N_EXPERTS - 1) << 4)).astype(jnp.int32)
    spare = n_used + experts
    zero_blocks = jnp.concatenate([pend // MOE_ROWS - 1, spare]).astype(jnp.int32)
    zero_valid = jnp.concatenate([pcounts > 0, spare < nb]).astype(jnp.int32)
    return (dest.reshape(n // ROW_BLOCK, 1, 2 * ROW_BLOCK), block_e, block_meta,
            n_used.astype(jnp.int32).reshape(1), zero_blocks, zero_valid, nb * MOE_ROWS)


def _combine_body(final_norm, dest_ref, next_ref, x_ref, route_ref, g2_ref, fg_ref, yb_ref, o_ref, buf, sems):
    i = pl.program_id(0)
    tb = x_ref.shape[0]
    slot = i % 2

    def gather(idx_ref, s, wait, unrolled=False):
        def row_copy(t, k):
            return pltpu.make_async_copy(yb_ref.at[pl.ds(idx_ref[0, 0, 2 * t + k], 1), :],
                                         buf.at[s, k, pl.ds(t, 1), :], sems.at[s])

        def body(t, carry):
            for k in range(2):
                if wait:
                    row_copy(t, k).wait()
                else:
                    row_copy(t, k).start()
            return carry

        if unrolled:
            for t in range(tb):
                body(t, 0)
        else:
            lax.fori_loop(0, tb, body, 0, unroll=DMA_UNROLL)

    @pl.when(i == 0)
    def _():
        gather(dest_ref, slot, False)

    for s in range(2):
        @pl.when(jnp.logical_and(i + 1 < pl.num_programs(0), slot == 1 - s))
        def _():
            gather(next_ref, s, False, unrolled=True)

    gather(dest_ref, slot, True)
    route = route_ref[...]
    x = x_ref[...] + g2_ref[0] * (route[:, 0:1] * buf[slot, 0] + route[:, 1:2] * buf[slot, 1])
    if final_norm:
        x = _rms(x) * fg_ref[...]
    o_ref[...] = x


def _combine(xa, yb, route, dest3, g2, final_g, blk, ent, n_out_blocks, final_norm):
    nxt = lambda i: blk(jnp.minimum(i + 1, n_out_blocks - 1))
    return pl.pallas_call(
        functools.partial(_combine_body, final_norm),
        grid=(n_out_blocks,),
        in_specs=[pl.BlockSpec((1, 1, 2 * ROW_BLOCK), lambda i: (blk(i), 0, 0), memory_space=pltpu.SMEM),
                  pl.BlockSpec((1, 1, 2 * ROW_BLOCK), lambda i: (nxt(i), 0, 0), memory_space=pltpu.SMEM),
                  pl.BlockSpec((ROW_BLOCK, D_MODEL), lambda i: (blk(i), 0)),
                  pl.BlockSpec((ROW_BLOCK, LANES), lambda i: (blk(i), 0)),
                  pl.BlockSpec((1, 1, D_MODEL), lambda i: (ent(blk(i)), 0, 0)),
                  pl.BlockSpec(final_g.shape, lambda i: (0, 0)),
                  pl.BlockSpec(memory_space=pl.ANY)],
        out_specs=pl.BlockSpec((ROW_BLOCK, D_MODEL), lambda i: (i, 0)),
        out_shape=jax.ShapeDtypeStruct((n_out_blocks * ROW_BLOCK, D_MODEL), F32),
        scratch_shapes=[pltpu.VMEM((2, 2, ROW_BLOCK, D_MODEL), F32), pltpu.SemaphoreType.DMA((2,))],
        compiler_params=_params("arbitrary", disable_bounds_checks=True),
        name="moe_combine",
    )(dest3, dest3, xa, route, g2, final_g, yb)


def _block_diag2(w):
    k, n = w.shape[1], w.shape[2]
    z = jnp.zeros((k, n), w.dtype)
    return jnp.concatenate([jnp.concatenate([w[0], z], axis=1), jnp.concatenate([z, w[1]], axis=1)], axis=0)


def kernel(x, c, ctx, c_ctx, ada_w, ada_b, norm1_g, norm2_g, w_in, mu_shift, decay_w0, decay_w2, iclr_a0, iclr_a2, gate_g2, k_k, k_a, r_k, gn_w, gn_b, conv_w, conv_gain, four_gain, w_out, router_g_w, router_g_b, router_e_w, router_e_b, exp_gate, exp_up, exp_down, final_g):
    bsz, seq, d = x.shape
    ctx_len = ctx.shape[1]
    depth = ada_w.shape[0]
    rows_b = ctx_len + seq
    n = bsz * rows_b
    bpb = rows_b // ROW_BLOCK
    lat_bpb = seq // ROW_BLOCK
    ent = lambda i: jnp.where(i % bpb == 0, bsz, i // bpb)

    xa = jnp.concatenate([ctx, x], axis=1).reshape(n, d)
    cvec = jnp.concatenate([c, c_ctx[None, :]], axis=0)
    cvec = cvec * jax.nn.sigmoid(cvec)
    head_id = np.arange(LANES) // HEAD
    ones_bd = jnp.asarray(head_id[:, None] == head_id[None, :], BF16)
    row2 = lambda a: a.reshape(1, -1)

    for l in range(depth):
        last = l == depth - 1
        mod = jnp.dot(cvec, ada_w[l], precision=lax.Precision.HIGHEST) + ada_b[l]
        sh1, sc1, g1, sh2, sc2, g2 = [mod[:, j * d:(j + 1) * d].reshape(bsz + 1, 1, d) for j in range(6)]

        pz, pcv, pfo_x, pfo_c = _inproj(xa, row2(norm1_g[l]), sc1, sh1, w_in[l].astype(BF16), ent, bsz, bpb)
        g, bonus, pm, qm, rp, y0 = _rwkv_chunks(
            pz, seq, ctx_len, bsz, row2(mu_shift[l]), _block_diag2(decay_w2[l]), _block_diag2(iclr_a2[l]),
            gate_g2[l], row2(decay_w0[l]), row2(iclr_a0[l]), row2(k_k[l]), row2(k_a[l]), row2(r_k[l]), ones_bd)
        yf, yb = _chunk_scan(pm, qm, rp, y0, seq, ctx_len, bsz)

        fgain = row2(four_gain[l])
        four_x = _fourier_long(pfo_x.reshape(bsz, seq, D_FOUR), fgain).reshape(bsz * seq, D_FOUR)
        four_c = _fourier_direct(pfo_c.reshape(bsz, ctx_len, D_FOUR), fgain).reshape(bsz * ctx_len, D_FOUR)

        wr = jnp.zeros((d, LANES), F32).at[:, :N_GROUPS].set(router_g_w[l])
        wr = wr.at[:, N_GROUPS:N_GROUPS + N_EXPERTS].set(router_e_w[l])
        rb = jnp.zeros((1, LANES), F32).at[0, :N_GROUPS].set(router_g_b[l])
        rb = rb.at[0, N_GROUPS:N_GROUPS + N_EXPERTS].set(router_e_b[l])
        wrh, wrl = _hilo(wr)
        xa, h2, route, counts = _post(yf, yb, bonus, g, pcv, four_x, four_c, xa, seq, ctx_len, ent,
                                      w_out[l].astype(BF16), row2(gn_w[l]), row2(gn_b[l]), conv_w[l],
                                      row2(conv_gain[l]), g1, sc2, sh2, row2(norm2_g[l]), wrh, wrl, rb, ones_bd)

        dest3, block_e, block_meta, n_used, zero_blocks, zero_valid, cap = _dispatch(route, counts)
        xe = _scatter_rows(h2, dest3, zero_blocks, zero_valid, cap)
        ye = _experts(xe, block_e, block_meta, n_used, exp_gate, exp_up, exp_down, l)
        if last:
            blk = lambda i: (i // lat_bpb) * bpb + 1 + i % lat_bpb
            xa = _combine(xa, ye, route, dest3, g2, row2(final_g), blk, ent, bsz * lat_bpb, True)
        else:
            xa = _combine(xa, ye, route, dest3, g2, row2(final_g), lambda i: i, ent, n // ROW_BLOCK, False)

    return xa.reshape(bsz, seq, d)
```

```python
import functools

import numpy as np
import jax
import jax.numpy as jnp
from jax import lax
from jax.experimental import pallas as pl
from jax.experimental.pallas import tpu as pltpu

F32 = jnp.float32
BF16 = jnp.bfloat16

D_MODEL = 1024
HEAD = 64
D_RWKV = 512
H_RWKV = D_RWKV // HEAD
D_CONV = 256
D_FOUR = 256
FOUR_GROUP = 64
D_Z = 3 * D_RWKV + 2 * 64 + 2 * 64 + 128
D_IN = D_Z + 3 * D_CONV + D_FOUR
GRID_W = 64
N_GROUPS = 4
EXPERTS_PER_GROUP = 8
N_EXPERTS = N_GROUPS * EXPERTS_PER_GROUP
D_EXPERT = D_MODEL // 2
RMS_EPS = 1e-6
GN_EPS = 64e-5

CHUNK = 64
ROW_BLOCK = 256
MOE_ROWS = 256
FOUR_INNER = 128
FOUR_GROUP_STEP = 4
LANES = 128
VMEM_LIMIT = 48 * 1024 * 1024

NN = (((1,), (0,)), ((), ()))
NT = (((1,), (1,)), ((), ()))
TN = (((0,), (0,)), ((), ()))


def _params(*sem, **kw):
    return pltpu.CompilerParams(dimension_semantics=sem, vmem_limit_bytes=VMEM_LIMIT, **kw)


def _split2(x):
    hi = x.astype(BF16)
    lo = (x - hi.astype(F32)).astype(BF16)
    return hi, lo


def _dot(a, b, dims=NN):
    return lax.dot_general(a, b, dims, preferred_element_type=F32)


def _dot1(a, b, dims=NN):
    return _dot(a.astype(BF16), b.astype(BF16), dims)


def _dot3(a, b, dims=NN):
    ah, al = _split2(a)
    bh, bl = _split2(b)
    return _dot(ah, bh, dims) + (_dot(ah, bl, dims) + _dot(al, bh, dims))


def _dot3c(ch, cl, x, dims=NN):
    xh, xl = _split2(x)
    return _dot(ch, xh, dims) + (_dot(cl, xh, dims) + _dot(ch, xl, dims))


def _dot3r(x, ch, cl, dims=NN):
    xh, xl = _split2(x)
    return _dot(xh, ch, dims) + (_dot(xl, ch, dims) + _dot(xh, cl, dims))


def _headsums(xs, ones_pair):
    rows = xs[0].shape[0]
    tiles = xs[0].shape[1] // LANES
    parts = []
    for x in xs:
        for part in _split2(x):
            parts.extend(part[:, t * LANES:(t + 1) * LANES] for t in range(tiles))
    s = _dot(jnp.concatenate(parts, axis=0), ones_pair)
    outs = []
    for j in range(len(xs)):
        base = j * 2 * tiles
        outs.append(jnp.concatenate(
            [s[(base + t) * rows:(base + t + 1) * rows] + s[(base + tiles + t) * rows:(base + tiles + t + 1) * rows]
             for t in range(tiles)], axis=1))
    return outs


def _rms(x, eps=RMS_EPS):
    return x * lax.rsqrt(jnp.mean(x * x, axis=-1, keepdims=True) + eps)


def _sigmoid(x):
    return 1.0 / (1.0 + jnp.exp(-x))


def _softplus(x):
    return jnp.maximum(x, 0.0) + jnp.log(1.0 + jnp.exp(-jnp.abs(x)))


def _inproj_body(blocks_per_batch, x_ref, g_ref, sc_ref, sh_ref, w_ref, z_ref, cv_ref, fx_ref, fc_ref):
    h = _rms(x_ref[...]) * g_ref[...] * (1.0 + sc_ref[0]) + sh_ref[0]
    p = _dot(h.astype(BF16), w_ref[...])
    z_ref[...] = p[:, :D_Z]
    cv_ref[...] = p[:, D_Z:D_Z + 3 * D_CONV]
    is_ctx = pl.program_id(0) % blocks_per_batch == 0

    fo = p[:, D_Z + 3 * D_CONV:]
    fx_ref[...] = fo

    @pl.when(is_ctx)
    def _():
        fc_ref[...] = fo


def _latent_block(i, bpb):
    return (i // bpb) * (bpb - 1) + jnp.maximum(i % bpb - 1, 0)


def _inproj(xa, gain, sc, sh, w_bf, ent, bsz, bpb):
    n = xa.shape[0]
    row = lambda w: pl.BlockSpec((ROW_BLOCK, w), lambda i: (i, 0))
    full = lambda a: pl.BlockSpec(a.shape, lambda i: (0,) * a.ndim)
    mod = pl.BlockSpec((1, 1, D_MODEL), lambda i: (ent(i), 0, 0))
    return pl.pallas_call(
        functools.partial(_inproj_body, bpb),
        grid=(n // ROW_BLOCK,),
        in_specs=[row(D_MODEL), full(gain), mod, mod, full(w_bf)],
        out_specs=[row(D_Z), row(3 * D_CONV),
                   pl.BlockSpec((ROW_BLOCK, D_FOUR), lambda i: (_latent_block(i, bpb), 0)),
                   pl.BlockSpec((ROW_BLOCK, D_FOUR), lambda i: (i // bpb, 0))],
        out_shape=[jax.ShapeDtypeStruct((n, D_Z), F32), jax.ShapeDtypeStruct((n, 3 * D_CONV), F32),
                   jax.ShapeDtypeStruct((n - bsz * ROW_BLOCK, D_FOUR), F32),
                   jax.ShapeDtypeStruct((bsz * ROW_BLOCK, D_FOUR), F32)],
        compiler_params=_params("arbitrary"),
        name="inproj",
    )(xa, gain, sc, sh, w_bf)


def _streams_steps(res, i, blocks_per_batch, zm_ref, zp_ref, zn_ref, mu_ref, w2_ref, a2_ref, g2_ref,
                   w0_ref, a0_ref, kkw_ref, ka_ref, rk_ref, ones_ref):
    seq_pos = i % blocks_per_batch
    ctx_i = (seq_pos == 0).astype(jnp.int32)
    tb = zm_ref.shape[0]
    t = lax.broadcasted_iota(jnp.int32, (tb, 1), 0)
    ctx_v = jnp.zeros((tb, 1), jnp.int32) + ctx_i
    col = t & (GRID_W - 1)
    lmask = (col != 0) | ((ctx_v != 0) & (t != 0))
    rmask = (col != GRID_W - 1) | ((ctx_v != 0) & (t != tb - 1))
    top_v = jnp.zeros((tb, 1), jnp.int32) + (seq_pos == 1).astype(jnp.int32)
    bot_v = jnp.zeros((tb, 1), jnp.int32) + (seq_pos == blocks_per_batch - 1).astype(jnp.int32)
    umask = jnp.logical_not((top_v != 0) & (t < GRID_W))
    dmask = jnp.logical_not((bot_v != 0) & (t >= tb - GRID_W))
    lane_mask = jnp.where(ctx_i != 0, 1, 3)
    tiles = []
    for c0 in range(0, D_Z, LANES):
        cols = slice(c0, c0 + LANES)
        z = zm_ref[:, cols]
        left = jnp.where(lmask, pltpu.roll(z, 1, axis=0), 0.0)
        right = jnp.where(rmask, pltpu.roll(z, tb - 1, axis=0), 0.0)
        up = jnp.where(umask, jnp.concatenate([zp_ref[:, cols], z[:tb - GRID_W]], axis=0), 0.0)
        down = jnp.where(dmask, jnp.concatenate([z[GRID_W:], zn_ref[:, cols]], axis=0), 0.0)
        q = (lax.broadcasted_iota(jnp.int32, (1, LANES), 1) + c0) & lane_mask
        shifted = jnp.where(q == 0, left, jnp.where(q == 1, right, jnp.where(q == 2, up, down)))
        tiles.append(z + (shifted - z) * mu_ref[:, cols])
        yield
    per = D_RWKV // LANES
    r = jnp.concatenate(tiles[0:per], axis=1)
    k = jnp.concatenate(tiles[per:2 * per], axis=1)
    v = jnp.concatenate(tiles[2 * per:3 * per], axis=1)
    lw_in, la_in, lg = tiles[3 * per], tiles[3 * per + 1], tiles[3 * per + 2]

    g = _dot3(_sigmoid(lg), g2_ref[...])
    yield
    w_log = -_softplus(-(w0_ref[...] + _dot3(jnp.tanh(lw_in), w2_ref[...]))) - 0.5
    lw = -jnp.exp(w_log)
    yield
    a = _sigmoid(a0_ref[...] + _dot3(la_in, a2_ref[...]))
    yield
    ka = ka_ref[...]
    a_d = [a[:, d * D_RWKV:(d + 1) * D_RWKV] for d in range(2)]
    k_d = [k * (1.0 + (x - 1.0) * ka) for x in a_d]
    lw_d = [lw[:, d * D_RWKV:(d + 1) * D_RWKV] for d in range(2)]
    kq = k * kkw_ref[...]
    yield
    kq_ss, rk_sum = _headsums([kq * kq, r * (k_d[0] + k_d[1]) * rk_ref[...]], ones_ref[...])
    yield
    kk = kq / jnp.maximum(jnp.sqrt(kq_ss), 1e-12)
    b_d = [kk * x for x in a_d]
    res.update(r=r, v=v, kk=kk, g=g, bonus=rk_sum * v, lw_d=lw_d, k_d=k_d, b_d=b_d)


def _chunk_maps(r, v, kk, lw_d, k_d, b_d, p_out, q_out, rp_out, y0_out):
    cs = CHUNK
    rows = r.shape[0]
    n_chunks = rows // cs
    n_pairs = D_RWKV // LANES

    ri = lax.broadcasted_iota(jnp.int32, (rows, rows), 0)
    rj = lax.broadcasted_iota(jnp.int32, (rows, rows), 1)
    same_chunk = (ri >> 6) == (rj >> 6)
    ti = lax.broadcasted_iota(jnp.int32, (cs, LANES), 0)
    lane = lax.broadcasted_iota(jnp.int32, (cs, LANES), 1)
    tj = lane & (HEAD - 1)
    lo_half = lane < HEAD
    eye = ti == tj
    eye_f = eye.astype(F32)
    zero_bf = jnp.zeros((cs, LANES), BF16)
    strict, incl = [], []
    at, bt, kt, rt, bh, kh, e_tot = [], [], [], [], [], [], []
    for d in range(2):
        sgn = 1 if d == 0 else -1
        order = (ti - tj) * sgn
        strict.append(order > 0)
        incl.append(order >= 0)
        t_mat = jnp.concatenate([(same_chunk & ((ri - rj) * sgn >= 0)).astype(F32).astype(BF16),
                                 same_chunk.astype(F32).astype(BF16)], axis=0)
        lw = lw_d[d]
        l1 = lw.astype(BF16)
        rem = lw - l1.astype(F32)
        l2 = rem.astype(BF16)
        l3 = (rem - l2.astype(F32)).astype(BF16)
        gsum = _dot(t_mat, l1) + (_dot(t_mat, l2) + _dot(t_mat, l3))
        gcum, gtot = gsum[:rows], gsum[rows:]
        e_neg = jnp.exp(-gcum)
        e_rem = jnp.exp(gtot - gcum)
        at.append(-(kk * jnp.exp(gcum - lw)))
        bt.append(b_d[d] * e_neg)
        kt.append(k_d[d] * e_neg)
        rt.append(r * jnp.exp(gcum))
        bh.append(b_d[d] * e_rem)
        kh.append(k_d[d] * e_rem)
        e_tot.append(jnp.exp(gtot))
        yield

    def bd(y):
        y = y.astype(BF16)
        return jnp.concatenate([jnp.where(lo_half, y, zero_bf), jnp.where(lo_half, zero_bf, y)], axis=0)

    def mm(x, y_bd, dims=NN):
        return _dot(x.astype(BF16), y_bd, dims)

    tile = lambda a, c, p: a[c * cs:(c + 1) * cs, p * LANES:(p + 1) * LANES]
    for d in range(2):
        yield from _chunk_stages(d, [(d, c, p) for c in range(n_chunks) for p in range(n_pairs)], tile, bd, mm,
                                 strict, incl, eye, eye_f, lo_half, ti, tj, at, bt, kt, rt, bh, kh, e_tot, v,
                                 p_out, q_out, rp_out, y0_out)


def _chunk_stages(d, chains, tile, bd, mm, strict, incl, eye, eye_f, lo_half, ti, tj, at, bt, kt, rt, bh, kh,
                  e_tot, v, p_out, q_out, rp_out, y0_out):
    cs = CHUNK
    n_pairs = D_RWKV // LANES
    n_chunks = len(chains) // n_pairs
    s = [mm(jnp.concatenate([tile(at[d], c, p), tile(rt[d], c, p)], axis=0),
            jnp.concatenate([bd(tile(bt[d], c, p)), bd(tile(kt[d], c, p))], axis=0), NT) for d, c, p in chains]
    yield
    a_ab = [jnp.where(strict[d], x[:cs, :LANES], 0.0) for x, (d, _, _) in zip(s, chains)]
    a_ak = [jnp.where(strict[d], x[:cs, LANES:], 0.0) for x, (d, _, _) in zip(s, chains)]
    a_rb = [jnp.where(incl[d], x[cs:, :LANES], 0.0) for x, (d, _, _) in zip(s, chains)]
    a_rk = [jnp.where(incl[d], x[cs:, LANES:], 0.0) for x, (d, _, _) in zip(s, chains)]
    a0 = [jnp.where((ti >> 3) == (tj >> 3), x, 0.0) for x in a_ab]
    a2 = [mm(x, bd(x)) for x in a0]
    yield
    a2_bd = [bd(x) for x in a2]
    x0 = [eye_f + x for x in a0]
    x1 = [x + mm(x, y) for x, y in zip(x0, a2_bd)]
    yield
    a4 = [mm(x, y) for x, y in zip(a2, a2_bd)]
    yield
    minv = [x + mm(x, bd(y)) for x, y in zip(x1, a4)]
    yield
    lvl = 3
    while (1 << lvl) < cs:
        off = ((ti >> (lvl + 1)) == (tj >> (lvl + 1))) & ((ti >> lvl) != (tj >> lvl))
        t = [mm(jnp.where(off, x, 0.0), bd(y)) for x, y in zip(a_ab, minv)]
        yield
        minv = [x + mm(x, bd(y)) for x, y in zip(minv, t)]
        yield
        lvl += 1
    v_bd = [bd(tile(v, c, p)) for _, c, p in chains]
    avk = [mm(jnp.concatenate([x, z], axis=0), y) for x, z, y in zip(a_ak, a_rk, v_bd)]
    yield
    av = [x[:cs] for x in avk]
    bot_k = [x[cs:] for x in avk]
    wu = [mm(m, jnp.concatenate([bd(tile(at[d], c, p)), bd(y)], axis=1))
          for m, y, (d, c, p) in zip(minv, av, chains)]
    yield
    top_b = [mm(tile(bh[d], c, p), x.astype(BF16), TN) for x, (d, c, p) in zip(wu, chains)]
    yield
    top_k = [mm(tile(kh[d], c, p), tile(v, c, p).astype(BF16), TN) for d, c, p in chains]
    yield
    bot_b = [mm(x, jnp.concatenate([bd(y[:, :LANES]), bd(y[:, LANES:])], axis=1)) for x, y in zip(a_rb, wu)]
    yield
    sel = lambda x, off: jnp.where(lo_half, x[:cs, off:off + LANES], x[cs:, off:off + LANES])
    for c in range(n_chunks):
        idx = [c * n_pairs + p for p in range(n_pairs)]
        p_out[d, c] = jnp.concatenate(
            [sel(top_b[i], 0) + jnp.where(eye, tile(e_tot[d], c, p), 0.0) for p, i in enumerate(idx)], axis=1)
        q_out[d, c] = jnp.concatenate([sel(top_b[i], LANES) + sel(top_k[i], 0) for i in idx], axis=1)
        rp_out[d, c * cs:(c + 1) * cs, :] = jnp.concatenate(
            [tile(rt[d], c, p) + bot_b[i][:, :LANES] for p, i in enumerate(idx)], axis=1)
        y0_out[d, c * cs:(c + 1) * cs, :] = jnp.concatenate(
            [bot_b[i][:, LANES:] + bot_k[i] for i in idx], axis=1)


def _rwkv_chunks_body(blocks_per_batch, zm_ref, zp_ref, zn_ref, mu_ref, w2_ref, a2_ref, g2_ref, w0_ref,
                      a0_ref, kkw_ref, ka_ref, rk_ref, ones_ref, g_out, bon_out, p_out, q_out, rp_out, y0_out,
                      ):
    res = {}
    for _ in _streams_steps(res, pl.program_id(0), blocks_per_batch, zm_ref, zp_ref, zn_ref, mu_ref, w2_ref, a2_ref,
                            g2_ref, w0_ref, a0_ref, kkw_ref, ka_ref, rk_ref, ones_ref):
        pass
    g_out[...] = res["g"]
    bon_out[...] = res["bonus"]
    for _ in _chunk_maps(res["r"], res["v"], res["kk"], res["lw_d"], res["k_d"], res["b_d"],
                         p_out, q_out, rp_out, y0_out):
        pass


def _rwkv_chunks(pz, seq, ctx_len, bsz, mu, w2bd, a2bd, g2, w0, a0, kkw, ka, rk, ones_bd):
    n = pz.shape[0]
    assert ctx_len == ROW_BLOCK and seq % ROW_BLOCK == 0 and n == bsz * (ctx_len + seq)
    sub = ROW_BLOCK // GRID_W
    last = n // GRID_W - 1
    cps = ROW_BLOCK // CHUNK
    nblk = n // ROW_BLOCK
    full = lambda a: pl.BlockSpec(a.shape, lambda i: (0,) * a.ndim)
    row = pl.BlockSpec((ROW_BLOCK, D_RWKV), lambda i: (i, 0))
    row2 = pl.BlockSpec((2, ROW_BLOCK, D_RWKV), lambda i: (0, i, 0))
    mat = pl.BlockSpec((2, cps, HEAD, D_RWKV), lambda i: (0, i, 0, 0))
    s1 = jax.ShapeDtypeStruct((n, D_RWKV), F32)
    s2 = jax.ShapeDtypeStruct((2, n, D_RWKV), F32)
    sm = jax.ShapeDtypeStruct((2, n // CHUNK, HEAD, D_RWKV), F32)
    consts = (mu, w2bd, a2bd, g2, w0, a0, kkw, ka, rk, ones_bd)
    return pl.pallas_call(
        functools.partial(_rwkv_chunks_body, (ctx_len + seq) // ROW_BLOCK),
        grid=(nblk,),
        in_specs=[pl.BlockSpec((ROW_BLOCK, D_Z), lambda i: (i, 0)),
                  pl.BlockSpec((GRID_W, D_Z), lambda i: (jnp.maximum(i * sub - 1, 0), 0)),
                  pl.BlockSpec((GRID_W, D_Z), lambda i: (jnp.minimum((i + 1) * sub, last), 0))]
                 + [full(a) for a in consts],
        out_specs=[row, row, mat, mat, row2, row2],
        out_shape=[s1, s1, sm, sm, s2, s2],
        compiler_params=_params("parallel"),
        name="rwkv_chunks",
    )(pz, pz, pz, *consts)


SCAN_CHUNKS = 4


def _scan_body(pf_ref, qf_ref, rpf_ref, y0f_ref, pb_ref, qb_ref, rpb_ref, y0b_ref, yf_out, yb_out, h_scr):
    @pl.when(pl.program_id(0) == 0)
    def _():
        h_scr[...] = jnp.zeros_like(h_scr)

    cs = CHUNK
    bsz = h_scr.shape[1]
    n_pairs = D_RWKV // LANES
    lane = lax.broadcasted_iota(jnp.int32, (cs, LANES), 1)
    lo_half = lane < HEAD
    zero_bf = jnp.zeros((cs, LANES), BF16)

    def bd(y):
        return jnp.concatenate([jnp.where(lo_half, y, zero_bf), jnp.where(lo_half, zero_bf, y)], axis=0)

    dirs = ((pf_ref, qf_ref, rpf_ref, y0f_ref, yf_out), (pb_ref, qb_ref, rpb_ref, y0b_ref, yb_out))
    chains = [(d, b, p) for d in range(2) for b in range(bsz) for p in range(n_pairs)]
    state = [h_scr[d, b, :, p * LANES:(p + 1) * LANES] for d, b, p in chains]
    for s in range(SCAN_CHUNKS):
        outs = []
        for (d, b, p), hcur in zip(chains, state):
            p_ref, _, rp_ref, _, _ = dirs[d]
            c = s if d == 0 else SCAN_CHUNKS - 1 - s
            ls = slice(p * LANES, (p + 1) * LANES)
            x = jnp.concatenate([p_ref[0, b, c, :, ls], rp_ref[0, b, c * cs:(c + 1) * cs, ls]], axis=0)
            xh, xl = _split2(x)
            hh, hl = _split2(hcur)
            hh, hl = bd(hh), bd(hl)
            outs.append(_dot(xh, hh) + (_dot(xh, hl) + _dot(xl, hh)))
        new_state = []
        for (d, b, p), o in zip(chains, outs):
            _, q_ref, _, y0_ref, y_out = dirs[d]
            c = s if d == 0 else SCAN_CHUNKS - 1 - s
            ls = slice(p * LANES, (p + 1) * LANES)
            new_state.append(o[:cs] + q_ref[0, b, c, :, ls])
            y_out[b, c * cs:(c + 1) * cs, ls] = y0_ref[0, b, c * cs:(c + 1) * cs, ls] + o[cs:]
        state = new_state
    for (d, b, p), hcur in zip(chains, state):
        h_scr[d, b, :, p * LANES:(p + 1) * LANES] = hcur


def _chunk_scan(p, q, rp, y0, seq, ctx_len, bsz):
    n = rp.shape[1]
    rows_b = ctx_len + seq
    ncb = rows_b // CHUNK
    assert ctx_len % (SCAN_CHUNKS * CHUNK) == 0 and seq % (SCAN_CHUNKS * CHUNK) == 0
    steps = ncb // SCAN_CHUNKS
    ctx_steps = ctx_len // (SCAN_CHUNKS * CHUNK)
    p5 = p.reshape(2, bsz, ncb, HEAD, D_RWKV)
    q5 = q.reshape(2, bsz, ncb, HEAD, D_RWKV)
    rp4 = rp.reshape(2, bsz, rows_b, D_RWKV)
    y04 = y0.reshape(2, bsz, rows_b, D_RWKV)
    pos_f = lambda i: i
    pos_b = lambda i: jnp.where(i < ctx_steps, ctx_steps - 1 - i, steps - 1 - (i - ctx_steps))
    rows = SCAN_CHUNKS * CHUNK

    def specs(d, pos):
        mat = pl.BlockSpec((1, bsz, SCAN_CHUNKS, HEAD, D_RWKV), lambda i: (d, 0, pos(i), 0, 0))
        tok = pl.BlockSpec((1, bsz, rows, D_RWKV), lambda i: (d, 0, pos(i), 0))
        return [mat, mat, tok, tok]

    out_f = pl.BlockSpec((bsz, rows, D_RWKV), lambda i: (0, pos_f(i), 0))
    out_b = pl.BlockSpec((bsz, rows, D_RWKV), lambda i: (0, pos_b(i), 0))
    shp = jax.ShapeDtypeStruct((bsz, rows_b, D_RWKV), F32)
    yf, yb = pl.pallas_call(
        _scan_body,
        grid=(steps,),
        in_specs=specs(0, pos_f) + specs(1, pos_b),
        out_specs=[out_f, out_b],
        out_shape=[shp, shp],
        scratch_shapes=[pltpu.VMEM((2, bsz, HEAD, D_RWKV), F32)],
        compiler_params=_params("arbitrary"),
        name="chunk_scan",
    )(p5, q5, rp4, y04, p5, q5, rp4, y04)
    return yf.reshape(n, D_RWKV), yb.reshape(n, D_RWKV)


def _hilo(a):
    if isinstance(a, np.ndarray):
        a = a.astype(np.float32)
        hi = a.astype(BF16)
        return jnp.asarray(hi), jnp.asarray((a - hi.astype(np.float32)).astype(BF16))
    hi = a.astype(BF16)
    return hi, (a - hi.astype(F32)).astype(BF16)


def _channel_tables(length):
    j = np.arange(FOUR_GROUP)
    ang = 2.0 * np.pi * np.outer(j, j) / FOUR_GROUP
    scale = 1.0 / np.sqrt(float(length) * FOUR_GROUP)
    groups = D_FOUR // FOUR_GROUP
    c4 = np.kron(np.eye(groups), np.cos(ang)) * scale
    s4 = np.kron(np.eye(groups), np.sin(ang)) * scale
    return _hilo(c4) + _hilo(s4)


def _four_finish(fr, fi, c4h, c4l, s4h, s4l, gain):
    y = _dot3r(fr, c4h, c4l) + _dot3r(fi, s4h, s4l)
    return _rms(y) * gain


def _four_direct_body(f_ref, mh_ref, ml_ref, c4h, c4l, s4h, s4l, gain_ref, o_ref):
    length = f_ref.shape[1]
    fc = _dot3c(mh_ref[...], ml_ref[...], f_ref[0])
    o_ref[0] = _four_finish(fc[:length], fc[length:], c4h[...], c4l[...], s4h[...], s4l[...], gain_ref[...])


def _fourier_direct(f, gain):
    bsz, length, _ = f.shape
    t = np.arange(length)
    ang = 2.0 * np.pi * (np.outer(t, t) % length) / length
    mh, ml = _hilo(np.concatenate([np.cos(ang), -np.sin(ang)], axis=0))
    consts = (mh, ml) + _channel_tables(length) + (gain,)
    full = lambda a: pl.BlockSpec(a.shape, lambda b: (0,) * a.ndim)
    blk = pl.BlockSpec((1, length, D_FOUR), lambda b: (b, 0, 0))
    return pl.pallas_call(
        _four_direct_body,
        grid=(bsz,),
        in_specs=[blk] + [full(a) for a in consts],
        out_specs=blk,
        out_shape=jax.ShapeDtypeStruct(f.shape, F32),
        compiler_params=_params("parallel"),
        name="fourier_direct",
    )(f, *consts)


def _four_stage1_body(f_ref, mh_ref, ml_ref, o_ref):
    o_ref[0] = _dot3c(mh_ref[...], ml_ref[...], f_ref[0])


def _four_stage2_body(zr_ref, zi_ref, mh_ref, ml_ref, c4h, c4l, s4h, s4l, gain_ref, o_ref):
    group, inner = zr_ref.shape[1], zr_ref.shape[2]
    fc = [_dot3c(mh_ref[j], ml_ref[j], jnp.concatenate([zr_ref[0, j], zi_ref[0, j]], axis=0))
          for j in range(group)]
    o_ref[0] = jnp.concatenate(
        [_four_finish(x[:inner], x[inner:], c4h[...], c4l[...], s4h[...], s4l[...], gain_ref[...]) for x in fc],
        axis=1)


def _fourier_long(f, gain):
    bsz, length, _ = f.shape
    l2 = FOUR_INNER
    l1 = length // l2
    assert l1 * l2 == length and l1 % 16 == 0
    cols = l2 * D_FOUR
    col_tile = 4096
    th = np.arange(l1)
    ang1 = 2.0 * np.pi * (np.outer(th, th) % l1) / l1
    m1h, m1l = _hilo(np.concatenate([np.cos(ang1), -np.sin(ang1)], axis=0))
    z = pl.pallas_call(
        _four_stage1_body,
        grid=(bsz, cols // col_tile),
        in_specs=[pl.BlockSpec((1, l1, col_tile), lambda b, c: (b, 0, c)),
                  pl.BlockSpec(m1h.shape, lambda b, c: (0, 0)),
                  pl.BlockSpec(m1l.shape, lambda b, c: (0, 0))],
        out_specs=pl.BlockSpec((1, 2 * l1, col_tile), lambda b, c: (b, 0, c)),
        out_shape=jax.ShapeDtypeStruct((bsz, 2 * l1, cols), F32),
        compiler_params=_params("parallel", "parallel"),
        name="fourier_stage1",
    )(f.reshape(bsz, l1, cols), m1h, m1l)
    z = z.reshape(bsz, 2 * l1, l2, D_FOUR)

    ma = np.arange(l1)[:, None, None]
    mb = np.arange(l2)[None, :, None]
    tl = np.arange(l2)[None, None, :]
    ang2 = 2.0 * np.pi * (((ma + l1 * mb) * tl) % length) / length
    cos2, sin2 = np.cos(ang2), np.sin(ang2)
    m2 = np.concatenate([np.concatenate([cos2, sin2], axis=2), np.concatenate([-sin2, cos2], axis=2)], axis=1)
    m2h, m2l = _hilo(m2)
    consts = _channel_tables(length) + (gain,)
    full = lambda a: pl.BlockSpec(a.shape, lambda b, m: (0,) * a.ndim)
    grp = FOUR_GROUP_STEP
    tab = pl.BlockSpec((grp, 2 * l2, 2 * l2), lambda b, m: (m, 0, 0))
    out = pl.pallas_call(
        _four_stage2_body,
        grid=(bsz, l1 // grp),
        in_specs=[pl.BlockSpec((1, grp, l2, D_FOUR), lambda b, m: (b, m, 0, 0)),
                  pl.BlockSpec((1, grp, l2, D_FOUR), lambda b, m: (b, l1 // grp + m, 0, 0)),
                  tab, tab] + [full(a) for a in consts],
        out_specs=pl.BlockSpec((1, l2, grp * D_FOUR), lambda b, m: (b, 0, m)),
        out_shape=jax.ShapeDtypeStruct((bsz, l2, l1 * D_FOUR), F32),
        compiler_params=_params("parallel", "parallel"),
        name="fourier_stage2",
    )(z, z, m2h, m2l, *consts)
    return out.reshape(bsz, length, D_FOUR)


def _route(logits):
    lane = lax.broadcasted_iota(jnp.int32, (1, LANES), 1)
    lane_f = lane.astype(F32)
    neg = jnp.float32(-1e30)
    big = jnp.float32(1e9)
    gl = jnp.where(lane < N_GROUPS, logits, neg)
    gmax = jnp.max(gl, axis=-1, keepdims=True)
    pg_top = 1.0 / jnp.sum(jnp.exp(gl - gmax), axis=-1, keepdims=True)
    grp = jnp.min(jnp.where(gl == gmax, lane_f, big), axis=-1, keepdims=True)
    e_lane = lane - N_GROUPS
    in_grp = (e_lane >= 0) & (e_lane < N_EXPERTS) & ((e_lane >> 3).astype(F32) == grp)
    el = jnp.where(in_grp, logits, neg)
    m1 = jnp.max(el, axis=-1, keepdims=True)
    i1 = jnp.min(jnp.where(el == m1, lane_f, big), axis=-1, keepdims=True)
    el2 = jnp.where(lane_f == i1, neg, el)
    m2 = jnp.max(el2, axis=-1, keepdims=True)
    i2 = jnp.min(jnp.where(el2 == m2, lane_f, big), axis=-1, keepdims=True)
    e2 = jnp.exp(m2 - m1)
    den = 1.0 + e2
    return pg_top / den, pg_top * e2 / den, i1 - N_GROUPS, i2 - N_GROUPS


def _post_body(blocks_per_batch, yf_ref, yb_ref, bon_ref, g_ref, cv_ref, cvp_ref, cvn_ref, fx_ref, fc_ref, x_ref,
               wo_ref, gnw_ref, gnb_ref, cw_ref, cg_ref, g1_ref, sc2_ref, sh2_ref, n2_ref, wrh_ref, wrl_ref,
               rb_ref, ones_ref, x_out, h_out, route_out, count_out, count_scr):
    i = pl.program_id(0)

    @pl.when(i == 0)
    def _():
        count_scr[...] = jnp.zeros_like(count_scr)

    ones_pair = ones_ref[...]
    y = yf_ref[...] + yb_ref[...]
    mu = _headsums([y], ones_pair)[0] * (1.0 / HEAD)
    yc = y - mu
    var = _headsums([yc * yc], ones_pair)[0] * (1.0 / HEAD)
    yn = yc * lax.rsqrt(var + GN_EPS) * gnw_ref[...] + gnb_ref[...]
    o_rwkv = (yn + bon_ref[...]) * g_ref[...]

    cv = cv_ref[...]
    tb = cv.shape[0]
    t = lax.broadcasted_iota(jnp.int32, (tb, 1), 0)
    seq_pos = i % blocks_per_batch
    first = (seq_pos <= 1).astype(F32)
    final = jnp.logical_or(seq_pos == 0, seq_pos == blocks_per_batch - 1).astype(F32)
    zc = cv[:, D_CONV:2 * D_CONV] * cv[:, 2 * D_CONV:]
    zp_row = cvp_ref[7:8, D_CONV:2 * D_CONV] * cvp_ref[7:8, 2 * D_CONV:] * (1.0 - first)
    zn_row = cvn_ref[0:1, D_CONV:2 * D_CONV] * cvn_ref[0:1, 2 * D_CONV:] * (1.0 - final)
    prev = jnp.where(t == 0, zp_row, pltpu.roll(zc, 1, axis=0))
    nxt = jnp.where(t == tb - 1, zn_row, pltpu.roll(zc, tb - 1, axis=0))
    cw = cw_ref[...]
    conv = cv[:, :D_CONV] * (cw[0:1] * prev + cw[1:2] * zc + cw[2:3] * nxt)
    conv_o = _rms(conv) * cg_ref[...]

    four_o = jnp.where(seq_pos == 0, fc_ref[...], fx_ref[...])
    mix = (_dot(o_rwkv.astype(BF16), wo_ref[0:D_RWKV, :])
           + _dot(conv_o.astype(BF16), wo_ref[D_RWKV:D_RWKV + D_CONV, :])
           + _dot(four_o.astype(BF16), wo_ref[D_RWKV + D_CONV:, :]))
    x = x_ref[...] + g1_ref[0] * mix
    x_out[...] = x
    h2 = _rms(x) * n2_ref[...] * (1.0 + sc2_ref[0]) + sh2_ref[0]
    h_out[...] = h2

    gate0, gate1, e0, e1 = _route(_dot3r(h2, wrh_ref[...], wrl_ref[...]) + rb_ref[...])
    lane = lax.broadcasted_iota(jnp.int32, (1, LANES), 1)
    lane_f = lane.astype(F32)
    oh0 = (lane_f == e0).astype(F32)
    oh1 = (lane_f == e1).astype(F32)
    ri = lax.broadcasted_iota(jnp.int32, (tb, tb), 0)
    rj = lax.broadcasted_iota(jnp.int32, (tb, tb), 1)
    earlier = (rj < ri).astype(F32).astype(BF16)
    seen = count_scr[...]
    tot0 = jnp.sum(oh0, axis=0, keepdims=True)
    before = _dot(earlier, jnp.concatenate([oh0, oh1], axis=1).astype(BF16))
    before0 = before[:, :LANES] + seen
    before1 = before[:, LANES:] + (seen + tot0)
    rank0 = jnp.sum(oh0 * before0, axis=-1, keepdims=True)
    rank1 = jnp.sum(oh1 * before1, axis=-1, keepdims=True)
    seen = seen + tot0 + jnp.sum(oh1, axis=0, keepdims=True)
    count_scr[...] = seen
    count_out[...] = seen
    route_out[...] = jnp.where(
        lane == 0, gate0, jnp.where(lane == 1, gate1, jnp.where(lane == 2, e0, jnp.where(
            lane == 3, e1, jnp.where(lane == 4, rank0, jnp.where(lane == 5, rank1, 0.0))))))


def _post(yf, yb, bonus, g, pcv, four_x, four_c, xa, seq, ctx_len, ent, wo_bf, gnw, gnb, cw, cg, g1, sc2, sh2, n2,
          wrh, wrl, rb, ones_bd):
    n = xa.shape[0]
    assert ctx_len == ROW_BLOCK
    sub = ROW_BLOCK // 8
    last = n // 8 - 1
    bpb = (ctx_len + seq) // ROW_BLOCK
    full = lambda a: pl.BlockSpec(a.shape, lambda i: (0,) * a.ndim)
    row = lambda w: pl.BlockSpec((ROW_BLOCK, w), lambda i: (i, 0))
    mod = pl.BlockSpec((1, 1, D_MODEL), lambda i: (ent(i), 0, 0))
    return pl.pallas_call(
        functools.partial(_post_body, (ctx_len + seq) // ROW_BLOCK),
        grid=(n // ROW_BLOCK,),
        in_specs=[row(D_RWKV), row(D_RWKV), row(D_RWKV), row(D_RWKV), row(3 * D_CONV),
                  pl.BlockSpec((8, 3 * D_CONV), lambda i: (jnp.maximum(i * sub - 1, 0), 0)),
                  pl.BlockSpec((8, 3 * D_CONV), lambda i: (jnp.minimum((i + 1) * sub, last), 0)),
                  pl.BlockSpec((ROW_BLOCK, D_FOUR), lambda i: (_latent_block(i, bpb), 0)),
                  pl.BlockSpec((ROW_BLOCK, D_FOUR), lambda i: (i // bpb, 0)),
                  row(D_MODEL), full(wo_bf), full(gnw), full(gnb), full(cw), full(cg),
                  mod, mod, mod, full(n2), full(wrh), full(wrl), full(rb), full(ones_bd)],
        out_specs=[row(D_MODEL), row(D_MODEL), row(LANES), pl.BlockSpec((1, LANES), lambda i: (0, 0))],
        out_shape=[jax.ShapeDtypeStruct((n, D_MODEL), F32), jax.ShapeDtypeStruct((n, D_MODEL), F32),
                   jax.ShapeDtypeStruct((n, LANES), F32), jax.ShapeDtypeStruct((1, LANES), F32)],
        scratch_shapes=[pltpu.VMEM((1, LANES), F32)],
        compiler_params=_params("arbitrary"),
        name="post_mix",
    )(yf, yb, bonus, g, pcv, pcv, pcv, four_x, four_c, xa, wo_bf, gnw, gnb, cw, cg, g1, sc2, sh2, n2, wrh, wrl,
      rb, ones_bd)


def _scatter_rows_body(zb_ref, zv_ref, dest_ref, h_ref, xb_ref, zero_buf, sem):
    tb = h_ref.shape[0]

    @pl.when(pl.program_id(0) == 0)
    def _():
        zero_buf[...] = jnp.zeros_like(zero_buf)

        def block_fill(j):
            start = pl.multiple_of(zb_ref[j] * MOE_ROWS, MOE_ROWS)
            return pltpu.make_async_copy(zero_buf, xb_ref.at[pl.ds(start, MOE_ROWS), :], sem)

        def fill_start(j, carry):
            @pl.when(zv_ref[j] != 0)
            def _():
                block_fill(j).start()
            return carry

        def fill_wait(j, carry):
            @pl.when(zv_ref[j] != 0)
            def _():
                block_fill(j).wait()
            return carry

        lax.fori_loop(0, zb_ref.shape[0], fill_start, 0)
        lax.fori_loop(0, zb_ref.shape[0], fill_wait, 0)

    def row_copy(t, k):
        return pltpu.make_async_copy(h_ref.at[pl.ds(t, 1), :],
                                     xb_ref.at[pl.ds(dest_ref[0, 0, k * tb + t], 1), :], sem)

    def issue(t, carry):
        row_copy(t, 0).start()
        row_copy(t, 1).start()
        return carry

    def drain(t, carry):
        row_copy(t, 0).wait()
        row_copy(t, 1).wait()
        return carry

    for t in range(tb):
        issue(t, 0)
    lax.fori_loop(0, tb, drain, 0, unroll=DMA_UNROLL)


DMA_UNROLL = 8
SCATTER_ROWS = 512


def _scatter_rows(h2, dest, zero_blocks, zero_valid, cap):
    n = h2.shape[0]
    steps = n // SCATTER_ROWS
    grid_spec = pltpu.PrefetchScalarGridSpec(
        num_scalar_prefetch=2,
        grid=(steps,),
        in_specs=[pl.BlockSpec((1, 1, 2 * SCATTER_ROWS), lambda i, zb, zv: (i, 0, 0), memory_space=pltpu.SMEM),
                  pl.BlockSpec((SCATTER_ROWS, D_MODEL), lambda i, zb, zv: (i, 0))],
        out_specs=pl.BlockSpec(memory_space=pl.ANY),
        scratch_shapes=[pltpu.VMEM((MOE_ROWS, D_MODEL), F32), pltpu.SemaphoreType.DMA(())],
    )
    return pl.pallas_call(
        _scatter_rows_body,
        grid_spec=grid_spec,
        out_shape=jax.ShapeDtypeStruct((cap, D_MODEL), F32),
        compiler_params=_params("arbitrary", disable_bounds_checks=True),
        name="moe_scatter",
    )(zero_blocks, zero_valid, _block_rows(dest, SCATTER_ROWS), h2)


def _expert_body(layer, be_ref, meta_ref, nu_ref, xb_ref, wg_hbm, wu_hbm, wd_hbm, o_ref,
                 land_g, land_u, land_d, wg_s, wu_s, wd_s, sems):
    i = pl.program_id(0)
    used = i < nu_ref[0]
    meta = meta_ref[i]
    first = (meta & 1) != 0
    final = (meta & 2) != 0
    has_next = (meta & 4) != 0
    slot = (meta >> 3) & 1
    nxt = meta >> 4

    def fetch(e):
        return [pltpu.make_async_copy(w.at[layer, e], land, sems.at[k])
                for k, (w, land) in enumerate(((wg_hbm, land_g), (wu_hbm, land_u), (wd_hbm, land_d)))]

    def cast_to(s):
        wg_s[s] = land_g[...].astype(BF16)
        wu_s[s] = land_u[...].astype(BF16)
        wd_s[s] = land_d[...].astype(BF16)

    @pl.when(i == 0)
    def _():
        for c in fetch(be_ref[0]):
            c.start()
        for c in fetch(be_ref[0]):
            c.wait()
        cast_to(0)

    @pl.when(used & first & has_next)
    def _():
        for c in fetch(nxt):
            c.start()

    @pl.when(used)
    def _():
        xb = xb_ref[...].astype(BF16)
        gate = _dot(xb, wg_s[slot])
        up = _dot(xb, wu_s[slot])
        act = gate * _sigmoid(gate) * up
        o_ref[...] = _dot(act.astype(BF16), wd_s[slot])

    @pl.when(used & final & has_next)
    def _():
        for c in fetch(nxt):
            c.wait()
        cast_to(1 - slot)

    @pl.when(jnp.logical_not(used))
    def _():
        o_ref[...] = jnp.zeros_like(o_ref)


def _experts(xb, block_e, block_meta, n_used, exp_gate, exp_up, exp_down, layer):
    cap = xb.shape[0]
    nb = cap // MOE_ROWS
    blk = lambda i, nu: jnp.minimum(i, nu[0] - 1)
    grid_spec = pltpu.PrefetchScalarGridSpec(
        num_scalar_prefetch=3,
        grid=(nb,),
        in_specs=[pl.BlockSpec((MOE_ROWS, D_MODEL), lambda i, be, meta, nu: (blk(i, nu), 0)),
                  pl.BlockSpec(memory_space=pl.ANY), pl.BlockSpec(memory_space=pl.ANY),
                  pl.BlockSpec(memory_space=pl.ANY)],
        out_specs=pl.BlockSpec((MOE_ROWS, D_MODEL), lambda i, be, meta, nu: (i, 0)),
        scratch_shapes=[pltpu.VMEM((D_MODEL, D_EXPERT), F32), pltpu.VMEM((D_MODEL, D_EXPERT), F32),
                        pltpu.VMEM((D_EXPERT, D_MODEL), F32),
                        pltpu.VMEM((2, D_MODEL, D_EXPERT), BF16), pltpu.VMEM((2, D_MODEL, D_EXPERT), BF16),
                        pltpu.VMEM((2, D_EXPERT, D_MODEL), BF16), pltpu.SemaphoreType.DMA((3,))],
    )
    return pl.pallas_call(
        functools.partial(_expert_body, layer),
        grid_spec=grid_spec,
        out_shape=jax.ShapeDtypeStruct((cap, D_MODEL), F32),
        compiler_params=_params("arbitrary"),
        name="experts",
    )(block_e, block_meta, n_used, xb, exp_gate, exp_up, exp_down)


def _block_rows(dest, rows):
    n = dest.shape[1]
    return jnp.stack([dest[0].reshape(n // rows, rows), dest[1].reshape(n // rows, rows)],
                     axis=1).reshape(n // rows, 1, 2 * rows)


def _dispatch(route, counts):
    n = route.shape[0]
    counts = counts[0, :N_EXPERTS].astype(jnp.int32)
    pcounts = (counts + MOE_ROWS - 1) // MOE_ROWS * MOE_ROWS
    pend = jnp.cumsum(pcounts)
    pstart = pend - pcounts
    experts = jnp.arange(N_EXPERTS, dtype=jnp.int32)
    dest = jnp.stack([
        jnp.sum(jnp.where(route[:, 2 + k].astype(jnp.int32)[:, None] == experts, pstart, 0), axis=-1)
        + route[:, 4 + k].astype(jnp.int32) for k in range(2)])
    nb = -(-2 * n // MOE_ROWS) + N_EXPERTS
    block_start = jnp.arange(nb, dtype=jnp.int32) * MOE_ROWS
    block_e = jnp.minimum(jnp.sum((pend[None, :] <= block_start[:, None]).astype(jnp.int32), axis=1),
                          N_EXPERTS - 1)
    n_used = pend[-1] // MOE_ROWS
    j = jnp.arange(nb, dtype=jnp.int32)
    used = j < n_used
    prev_e = jnp.concatenate([block_e[:1] - 1, block_e[:-1]])
    next_blk_e = jnp.concatenate([block_e[1:], block_e[-1:] + 1])
    first = used & (prev_e != block_e)
    final = used & ((j == n_used - 1) | (next_blk_e != block_e))
    later = (experts[None, :] > experts[:, None]) & (pcounts[None, :] > 0)
    next_expert = jnp.min(jnp.where(later, experts[None, :], N_EXPERTS), axis=1)
    nxt = jnp.sum(jnp.where(block_e[:, None] == experts[None, :], next_expert[None, :], 0), axis=1)
    slot = (jnp.cumsum(first.astype(jnp.int32)) - 1) & 1
    block_meta = (first.astype(jnp.int32) | (final.astype(jnp.int32) << 1)
                  | ((nxt < N_EXPERTS).astype(jnp.int32) << 2) | (slot << 3)
                  | (jnp.minimum(nxt, N_EXPERTS - 1) << 4)).astype(jnp.int32)
    spare = n_used + experts
    zero_blocks = jnp.concatenate([pend // MOE_ROWS - 1, spare]).astype(jnp.int32)
    zero_valid = jnp.concatenate([pcounts > 0, spare < nb]).astype(jnp.int32)
    return dest, block_e, block_meta, n_used.astype(jnp.int32).reshape(1), zero_blocks, zero_valid, nb * MOE_ROWS


def _combine_body(final_norm, dest_ref, next_ref, x_ref, route_ref, g2_ref, fg_ref, yb_ref, o_ref, buf, sems):
    i = pl.program_id(0)
    tb = x_ref.shape[0]
    slot = i % 2

    def gather(idx_ref, s, wait, unrolled=False):
        def row_copy(t, k):
            return pltpu.make_async_copy(yb_ref.at[pl.ds(idx_ref[0, 0, k * tb + t], 1), :],
                                         buf.at[s, k, pl.ds(t, 1), :], sems.at[s])

        def body(t, carry):
            for k in range(2):
                if wait:
                    row_copy(t, k).wait()
                else:
                    row_copy(t, k).start()
            return carry

        if unrolled:
            for t in range(tb):
                body(t, 0)
        else:
            lax.fori_loop(0, tb, body, 0, unroll=DMA_UNROLL)

    @pl.when(i == 0)
    def _():
        gather(dest_ref, slot, False)

    for s in range(2):
        @pl.when(jnp.logical_and(i + 1 < pl.num_programs(0), slot == 1 - s))
        def _():
            gather(next_ref, s, False, unrolled=True)

    gather(dest_ref, slot, True)
    route = route_ref[...]
    x = x_ref[...] + g2_ref[0] * (route[:, 0:1] * buf[slot, 0] + route[:, 1:2] * buf[slot, 1])
    if final_norm:
        x = _rms(x) * fg_ref[...]
    o_ref[...] = x


def _combine(xa, yb, route, dest3, g2, final_g, blk, ent, n_out_blocks, final_norm):
    nxt = lambda i: blk(jnp.minimum(i + 1, n_out_blocks - 1))
    return pl.pallas_call(
        functools.partial(_combine_body, final_norm),
        grid=(n_out_blocks,),
        in_specs=[pl.BlockSpec((1, 1, 2 * ROW_BLOCK), lambda i: (blk(i), 0, 0), memory_space=pltpu.SMEM),
                  pl.BlockSpec((1, 1, 2 * ROW_BLOCK), lambda i: (nxt(i), 0, 0), memory_space=pltpu.SMEM),
                  pl.BlockSpec((ROW_BLOCK, D_MODEL), lambda i: (blk(i), 0)),
                  pl.BlockSpec((ROW_BLOCK, LANES), lambda i: (blk(i), 0)),
                  pl.BlockSpec((1, 1, D_MODEL), lambda i: (ent(blk(i)), 0, 0)),
                  pl.BlockSpec(final_g.shape, lambda i: (0, 0)),
                  pl.BlockSpec(memory_space=pl.ANY)],
        out_specs=pl.BlockSpec((ROW_BLOCK, D_MODEL), lambda i: (i, 0)),
        out_shape=jax.ShapeDtypeStruct((n_out_blocks * ROW_BLOCK, D_MODEL), F32),
        scratch_shapes=[pltpu.VMEM((2, 2, ROW_BLOCK, D_MODEL), F32), pltpu.SemaphoreType.DMA((2,))],
        compiler_params=_params("arbitrary", disable_bounds_checks=True),
        name="moe_combine",
    )(dest3, dest3, xa, route, g2, final_g, yb)


def _block_diag2(w):
    k, n = w.shape[1], w.shape[2]
    z = jnp.zeros((k, n), w.dtype)
    return jnp.concatenate([jnp.concatenate([w[0], z], axis=1), jnp.concatenate([z, w[1]], axis=1)], axis=0)


def kernel(x, c, ctx, c_ctx, ada_w, ada_b, norm1_g, norm2_g, w_in, mu_shift, decay_w0, decay_w2, iclr_a0, iclr_a2, gate_g2, k_k, k_a, r_k, gn_w, gn_b, conv_w, conv_gain, four_gain, w_out, router_g_w, router_g_b, router_e_w, router_e_b, exp_gate, exp_up, exp_down, final_g):
    bsz, seq, d = x.shape
    ctx_len = ctx.shape[1]
    depth = ada_w.shape[0]
    rows_b = ctx_len + seq
    n = bsz * rows_b
    bpb = rows_b // ROW_BLOCK
    lat_bpb = seq // ROW_BLOCK
    ent = lambda i: jnp.where(i % bpb == 0, bsz, i // bpb)

    xa = jnp.concatenate([ctx, x], axis=1).reshape(n, d)
    cvec = jnp.concatenate([c, c_ctx[None, :]], axis=0)
    cvec = cvec * jax.nn.sigmoid(cvec)
    head_id = np.arange(LANES) // HEAD
    ones_bd = jnp.asarray(head_id[:, None] == head_id[None, :], BF16)
    row2 = lambda a: a.reshape(1, -1)

    for l in range(depth):
        last = l == depth - 1
        mod = jnp.dot(cvec, ada_w[l], precision=lax.Precision.HIGHEST) + ada_b[l]
        sh1, sc1, g1, sh2, sc2, g2 = [mod[:, j * d:(j + 1) * d].reshape(bsz + 1, 1, d) for j in range(6)]

        pz, pcv, pfo_x, pfo_c = _inproj(xa, row2(norm1_g[l]), sc1, sh1, w_in[l].astype(BF16), ent, bsz, bpb)
        g, bonus, pm, qm, rp, y0 = _rwkv_chunks(
            pz, seq, ctx_len, bsz, row2(mu_shift[l]), _block_diag2(decay_w2[l]), _block_diag2(iclr_a2[l]),
            gate_g2[l], row2(decay_w0[l]), row2(iclr_a0[l]), row2(k_k[l]), row2(k_a[l]), row2(r_k[l]), ones_bd)
        yf, yb = _chunk_scan(pm, qm, rp, y0, seq, ctx_len, bsz)

        fgain = row2(four_gain[l])
        four_x = _fourier_long(pfo_x.reshape(bsz, seq, D_FOUR), fgain).reshape(bsz * seq, D_FOUR)
        four_c = _fourier_direct(pfo_c.reshape(bsz, ctx_len, D_FOUR), fgain).reshape(bsz * ctx_len, D_FOUR)

        wr = jnp.zeros((d, LANES), F32).at[:, :N_GROUPS].set(router_g_w[l])
        wr = wr.at[:, N_GROUPS:N_GROUPS + N_EXPERTS].set(router_e_w[l])
        rb = jnp.zeros((1, LANES), F32).at[0, :N_GROUPS].set(router_g_b[l])
        rb = rb.at[0, N_GROUPS:N_GROUPS + N_EXPERTS].set(router_e_b[l])
        wrh, wrl = _hilo(wr)
        xa, h2, route, counts = _post(yf, yb, bonus, g, pcv, four_x, four_c, xa, seq, ctx_len, ent,
                                      w_out[l].astype(BF16), row2(gn_w[l]), row2(gn_b[l]), conv_w[l],
                                      row2(conv_gain[l]), g1, sc2, sh2, row2(norm2_g[l]), wrh, wrl, rb, ones_bd)

        dest, block_e, block_meta, n_used, zero_blocks, zero_valid, cap = _dispatch(route, counts)
        xe = _scatter_rows(h2, dest, zero_blocks, zero_valid, cap)
        ye = _experts(xe, block_e, block_meta, n_used, exp_gate, exp_up, exp_down, l)
        dest3 = _block_rows(dest, ROW_BLOCK)
        if last:
            blk = lambda i: (i // lat_bpb) * bpb + 1 + i % lat_bpb
            xa = _combine(xa, ye, route, dest3, g2, row2(final_g), blk, ent, bsz * lat_bpb, True)
        else:
            xa = _combine(xa, ye, route, dest3, g2, row2(final_g), lambda i: i, ent, n // ROW_BLOCK, False)

    return xa.reshape(bsz, seq, d)
```

```python
import functools

import numpy as np
import jax
import jax.numpy as jnp
from jax import lax
from jax.experimental import pallas as pl
from jax.experimental.pallas import tpu as pltpu

F32 = jnp.float32
BF16 = jnp.bfloat16

D_MODEL = 1024
HEAD = 64
D_RWKV = 512
H_RWKV = D_RWKV // HEAD
D_CONV = 256
D_FOUR = 256
FOUR_GROUP = 64
D_Z = 3 * D_RWKV + 2 * 64 + 2 * 64 + 128
D_IN = D_Z + 3 * D_CONV + D_FOUR
GRID_W = 64
N_GROUPS = 4
EXPERTS_PER_GROUP = 8
N_EXPERTS = N_GROUPS * EXPERTS_PER_GROUP
D_EXPERT = D_MODEL // 2
RMS_EPS = 1e-6
GN_EPS = 64e-5

CHUNK = 64
ROW_BLOCK = 256
MOE_ROWS = 256
FOUR_INNER = 128
FOUR_GROUP_STEP = 4
LANES = 128
VMEM_LIMIT = 48 * 1024 * 1024

NN = (((1,), (0,)), ((), ()))
NT = (((1,), (1,)), ((), ()))
TN = (((0,), (0,)), ((), ()))


def _params(*sem, **kw):
    return pltpu.CompilerParams(dimension_semantics=sem, vmem_limit_bytes=VMEM_LIMIT, **kw)


def _split2(x):
    hi = x.astype(BF16)
    lo = (x - hi.astype(F32)).astype(BF16)
    return hi, lo


def _dot(a, b, dims=NN):
    return lax.dot_general(a, b, dims, preferred_element_type=F32)


def _dot1(a, b, dims=NN):
    return _dot(a.astype(BF16), b.astype(BF16), dims)


def _dot3(a, b, dims=NN):
    ah, al = _split2(a)
    bh, bl = _split2(b)
    return _dot(ah, bh, dims) + (_dot(ah, bl, dims) + _dot(al, bh, dims))


def _dot3c(ch, cl, x, dims=NN):
    xh, xl = _split2(x)
    return _dot(ch, xh, dims) + (_dot(cl, xh, dims) + _dot(ch, xl, dims))


def _dot3r(x, ch, cl, dims=NN):
    xh, xl = _split2(x)
    return _dot(xh, ch, dims) + (_dot(xl, ch, dims) + _dot(xh, cl, dims))


def _headsums(xs, ones_pair):
    rows = xs[0].shape[0]
    tiles = xs[0].shape[1] // LANES
    parts = []
    for x in xs:
        for part in _split2(x):
            parts.extend(part[:, t * LANES:(t + 1) * LANES] for t in range(tiles))
    s = _dot(jnp.concatenate(parts, axis=0), ones_pair)
    outs = []
    for j in range(len(xs)):
        base = j * 2 * tiles
        outs.append(jnp.concatenate(
            [s[(base + t) * rows:(base + t + 1) * rows] + s[(base + tiles + t) * rows:(base + tiles + t + 1) * rows]
             for t in range(tiles)], axis=1))
    return outs


def _rms(x, eps=RMS_EPS):
    return x * lax.rsqrt(jnp.mean(x * x, axis=-1, keepdims=True) + eps)


def _sigmoid(x):
    return 1.0 / (1.0 + jnp.exp(-x))


def _softplus(x):
    return jnp.maximum(x, 0.0) + jnp.log(1.0 + jnp.exp(-jnp.abs(x)))


def _token_specs(xs, bpb):
    if len(xs) == 1:
        return [pl.BlockSpec((ROW_BLOCK, D_MODEL), lambda i: (i, 0))]
    return [pl.BlockSpec((ROW_BLOCK, D_MODEL), lambda i: (_latent_block(i, bpb), 0)),
            pl.BlockSpec((ROW_BLOCK, D_MODEL), lambda i: (i // bpb, 0))]


def _token_rows(x_refs, is_ctx):
    if len(x_refs) == 1:
        return x_refs[0][...]
    return jnp.where(is_ctx, x_refs[1][...], x_refs[0][...])


def _inproj_body(n_src, blocks_per_batch, *refs):
    x_refs = refs[:n_src]
    g_ref, sc_ref, sh_ref, w_ref, z_ref, cv_ref, fx_ref, fc_ref = refs[n_src:]
    is_ctx = pl.program_id(0) % blocks_per_batch == 0
    h = _rms(_token_rows(x_refs, is_ctx)) * g_ref[...] * (1.0 + sc_ref[0]) + sh_ref[0]
    p = _dot(h.astype(BF16), w_ref[...])
    z_ref[...] = p[:, :D_Z]
    cv_ref[...] = p[:, D_Z:D_Z + 3 * D_CONV]

    fo = p[:, D_Z + 3 * D_CONV:]
    fx_ref[...] = fo

    @pl.when(is_ctx)
    def _():
        fc_ref[...] = fo


def _latent_block(i, bpb):
    return (i // bpb) * (bpb - 1) + jnp.maximum(i % bpb - 1, 0)


def _inproj(xs, gain, sc, sh, w_bf, ent, bsz, bpb):
    n = bsz * bpb * ROW_BLOCK
    row = lambda w: pl.BlockSpec((ROW_BLOCK, w), lambda i: (i, 0))
    full = lambda a: pl.BlockSpec(a.shape, lambda i: (0,) * a.ndim)
    mod = pl.BlockSpec((1, 1, D_MODEL), lambda i: (ent(i), 0, 0))
    return pl.pallas_call(
        functools.partial(_inproj_body, len(xs), bpb),
        grid=(n // ROW_BLOCK,),
        in_specs=_token_specs(xs, bpb) + [full(gain), mod, mod, full(w_bf)],
        out_specs=[row(D_Z), row(3 * D_CONV),
                   pl.BlockSpec((ROW_BLOCK, D_FOUR), lambda i: (_latent_block(i, bpb), 0)),
                   pl.BlockSpec((ROW_BLOCK, D_FOUR), lambda i: (i // bpb, 0))],
        out_shape=[jax.ShapeDtypeStruct((n, D_Z), F32), jax.ShapeDtypeStruct((n, 3 * D_CONV), F32),
                   jax.ShapeDtypeStruct((n - bsz * ROW_BLOCK, D_FOUR), F32),
                   jax.ShapeDtypeStruct((bsz * ROW_BLOCK, D_FOUR), F32)],
        compiler_params=_params("arbitrary"),
        name="inproj",
    )(*xs, gain, sc, sh, w_bf)


def _streams_steps(res, i, blocks_per_batch, zm_ref, zp_ref, zn_ref, mu_ref, w2_ref, a2_ref, g2_ref,
                   w0_ref, a0_ref, kkw_ref, ka_ref, rk_ref, ones_ref):
    seq_pos = i % blocks_per_batch
    ctx_i = (seq_pos == 0).astype(jnp.int32)
    tb = zm_ref.shape[0]
    t = lax.broadcasted_iota(jnp.int32, (tb, 1), 0)
    ctx_v = jnp.zeros((tb, 1), jnp.int32) + ctx_i
    col = t & (GRID_W - 1)
    lmask = (col != 0) | ((ctx_v != 0) & (t != 0))
    rmask = (col != GRID_W - 1) | ((ctx_v != 0) & (t != tb - 1))
    top_v = jnp.zeros((tb, 1), jnp.int32) + (seq_pos == 1).astype(jnp.int32)
    bot_v = jnp.zeros((tb, 1), jnp.int32) + (seq_pos == blocks_per_batch - 1).astype(jnp.int32)
    umask = jnp.logical_not((top_v != 0) & (t < GRID_W))
    dmask = jnp.logical_not((bot_v != 0) & (t >= tb - GRID_W))
    lane_mask = jnp.where(ctx_i != 0, 1, 3)
    tiles = []
    for c0 in range(0, D_Z, LANES):
        cols = slice(c0, c0 + LANES)
        z = zm_ref[:, cols]
        left = jnp.where(lmask, pltpu.roll(z, 1, axis=0), 0.0)
        right = jnp.where(rmask, pltpu.roll(z, tb - 1, axis=0), 0.0)
        up = jnp.where(umask, jnp.concatenate([zp_ref[:, cols], z[:tb - GRID_W]], axis=0), 0.0)
        down = jnp.where(dmask, jnp.concatenate([z[GRID_W:], zn_ref[:, cols]], axis=0), 0.0)
        q = (lax.broadcasted_iota(jnp.int32, (1, LANES), 1) + c0) & lane_mask
        shifted = jnp.where(q == 0, left, jnp.where(q == 1, right, jnp.where(q == 2, up, down)))
        tiles.append(z + (shifted - z) * mu_ref[:, cols])
        yield
    per = D_RWKV // LANES
    r = jnp.concatenate(tiles[0:per], axis=1)
    k = jnp.concatenate(tiles[per:2 * per], axis=1)
    v = jnp.concatenate(tiles[2 * per:3 * per], axis=1)
    lw_in, la_in, lg = tiles[3 * per], tiles[3 * per + 1], tiles[3 * per + 2]

    g = _dot3(_sigmoid(lg), g2_ref[...])
    yield
    w_log = -_softplus(-(w0_ref[...] + _dot3(jnp.tanh(lw_in), w2_ref[...]))) - 0.5
    lw = -jnp.exp(w_log)
    yield
    a = _sigmoid(a0_ref[...] + _dot3(la_in, a2_ref[...]))
    yield
    ka = ka_ref[...]
    a_d = [a[:, d * D_RWKV:(d + 1) * D_RWKV] for d in range(2)]
    k_d = [k * (1.0 + (x - 1.0) * ka) for x in a_d]
    lw_d = [lw[:, d * D_RWKV:(d + 1) * D_RWKV] for d in range(2)]
    kq = k * kkw_ref[...]
    yield
    kq_ss, rk_sum = _headsums([kq * kq, r * (k_d[0] + k_d[1]) * rk_ref[...]], ones_ref[...])
    yield
    kk = kq / jnp.maximum(jnp.sqrt(kq_ss), 1e-12)
    b_d = [kk * x for x in a_d]
    res.update(r=r, v=v, kk=kk, g=g, bonus=rk_sum * v, lw_d=lw_d, k_d=k_d, b_d=b_d)


def _chunk_maps(r, v, kk, lw_d, k_d, b_d, p_out, q_out, rp_out, y0_out):
    cs = CHUNK
    rows = r.shape[0]
    n_chunks = rows // cs
    n_pairs = D_RWKV // LANES

    ri = lax.broadcasted_iota(jnp.int32, (rows, rows), 0)
    rj = lax.broadcasted_iota(jnp.int32, (rows, rows), 1)
    same_chunk = (ri >> 6) == (rj >> 6)
    ti = lax.broadcasted_iota(jnp.int32, (cs, LANES), 0)
    lane = lax.broadcasted_iota(jnp.int32, (cs, LANES), 1)
    tj = lane & (HEAD - 1)
    lo_half = lane < HEAD
    eye = ti == tj
    eye_f = eye.astype(F32)
    zero_bf = jnp.zeros((cs, LANES), BF16)
    strict, incl = [], []
    at, bt, kt, rt, bh, kh, e_tot = [], [], [], [], [], [], []
    for d in range(2):
        sgn = 1 if d == 0 else -1
        order = (ti - tj) * sgn
        strict.append(order > 0)
        incl.append(order >= 0)
        t_mat = jnp.concatenate([(same_chunk & ((ri - rj) * sgn >= 0)).astype(F32).astype(BF16),
                                 same_chunk.astype(F32).astype(BF16)], axis=0)
        lw = lw_d[d]
        l1 = lw.astype(BF16)
        rem = lw - l1.astype(F32)
        l2 = rem.astype(BF16)
        l3 = (rem - l2.astype(F32)).astype(BF16)
        gsum = _dot(t_mat, l1) + (_dot(t_mat, l2) + _dot(t_mat, l3))
        gcum, gtot = gsum[:rows], gsum[rows:]
        e_neg = jnp.exp(-gcum)
        e_rem = jnp.exp(gtot - gcum)
        at.append(-(kk * jnp.exp(gcum - lw)))
        bt.append(b_d[d] * e_neg)
        kt.append(k_d[d] * e_neg)
        rt.append(r * jnp.exp(gcum))
        bh.append(b_d[d] * e_rem)
        kh.append(k_d[d] * e_rem)
        e_tot.append(jnp.exp(gtot))
        yield

    def bd(y):
        y = y.astype(BF16)
        return jnp.concatenate([jnp.where(lo_half, y, zero_bf), jnp.where(lo_half, zero_bf, y)], axis=0)

    def mm(x, y_bd, dims=NN):
        return _dot(x.astype(BF16), y_bd, dims)

    tile = lambda a, c, p: a[c * cs:(c + 1) * cs, p * LANES:(p + 1) * LANES]
    for d in range(2):
        yield from _chunk_stages(d, [(d, c, p) for c in range(n_chunks) for p in range(n_pairs)], tile, bd, mm,
                                 strict, incl, eye, eye_f, lo_half, ti, tj, at, bt, kt, rt, bh, kh, e_tot, v,
                                 p_out, q_out, rp_out, y0_out)


def _chunk_stages(d, chains, tile, bd, mm, strict, incl, eye, eye_f, lo_half, ti, tj, at, bt, kt, rt, bh, kh,
                  e_tot, v, p_out, q_out, rp_out, y0_out):
    cs = CHUNK
    n_pairs = D_RWKV // LANES
    n_chunks = len(chains) // n_pairs
    s = [mm(jnp.concatenate([tile(at[d], c, p), tile(rt[d], c, p)], axis=0),
            jnp.concatenate([bd(tile(bt[d], c, p)), bd(tile(kt[d], c, p))], axis=0), NT) for d, c, p in chains]
    yield
    a_ab = [jnp.where(strict[d], x[:cs, :LANES], 0.0) for x, (d, _, _) in zip(s, chains)]
    a_ak = [jnp.where(strict[d], x[:cs, LANES:], 0.0) for x, (d, _, _) in zip(s, chains)]
    a_rb = [jnp.where(incl[d], x[cs:, :LANES], 0.0) for x, (d, _, _) in zip(s, chains)]
    a_rk = [jnp.where(incl[d], x[cs:, LANES:], 0.0) for x, (d, _, _) in zip(s, chains)]
    a0 = [jnp.where((ti >> 3) == (tj >> 3), x, 0.0) for x in a_ab]
    a2 = [mm(x, bd(x)) for x in a0]
    yield
    a2_bd = [bd(x) for x in a2]
    x0 = [eye_f + x for x in a0]
    x1 = [x + mm(x, y) for x, y in zip(x0, a2_bd)]
    yield
    a4 = [mm(x, y) for x, y in zip(a2, a2_bd)]
    yield
    minv = [x + mm(x, bd(y)) for x, y in zip(x1, a4)]
    yield
    lvl = 3
    while (1 << lvl) < cs:
        off = ((ti >> (lvl + 1)) == (tj >> (lvl + 1))) & ((ti >> lvl) != (tj >> lvl))
        t = [mm(jnp.where(off, x, 0.0), bd(y)) for x, y in zip(a_ab, minv)]
        yield
        minv = [x + mm(x, bd(y)) for x, y in zip(minv, t)]
        yield
        lvl += 1
    v_bd = [bd(tile(v, c, p)) for _, c, p in chains]
    avk = [mm(jnp.concatenate([x, z], axis=0), y) for x, z, y in zip(a_ak, a_rk, v_bd)]
    yield
    av = [x[:cs] for x in avk]
    bot_k = [x[cs:] for x in avk]
    wu = [mm(m, jnp.concatenate([bd(tile(at[d], c, p)), bd(y)], axis=1))
          for m, y, (d, c, p) in zip(minv, av, chains)]
    yield
    top_b = [mm(tile(bh[d], c, p), x.astype(BF16), TN) for x, (d, c, p) in zip(wu, chains)]
    yield
    top_k = [mm(tile(kh[d], c, p), tile(v, c, p).astype(BF16), TN) for d, c, p in chains]
    yield
    bot_b = [mm(x, jnp.concatenate([bd(y[:, :LANES]), bd(y[:, LANES:])], axis=1)) for x, y in zip(a_rb, wu)]
    yield
    sel = lambda x, off: jnp.where(lo_half, x[:cs, off:off + LANES], x[cs:, off:off + LANES])
    for c in range(n_chunks):
        idx = [c * n_pairs + p for p in range(n_pairs)]
        p_out[d, c] = jnp.concatenate(
            [sel(top_b[i], 0) + jnp.where(eye, tile(e_tot[d], c, p), 0.0) for p, i in enumerate(idx)], axis=1)
        q_out[d, c] = jnp.concatenate([sel(top_b[i], LANES) + sel(top_k[i], 0) for i in idx], axis=1)
        rp_out[d, c * cs:(c + 1) * cs, :] = jnp.concatenate(
            [tile(rt[d], c, p) + bot_b[i][:, :LANES] for p, i in enumerate(idx)], axis=1)
        y0_out[d, c * cs:(c + 1) * cs, :] = jnp.concatenate(
            [bot_b[i][:, LANES:] + bot_k[i] for i in idx], axis=1)


def _rwkv_chunks_body(blocks_per_batch, zm_ref, zp_ref, zn_ref, mu_ref, w2_ref, a2_ref, g2_ref, w0_ref,
                      a0_ref, kkw_ref, ka_ref, rk_ref, ones_ref, g_out, bon_out, p_out, q_out, rp_out, y0_out,
                      ):
    res = {}
    for _ in _streams_steps(res, pl.program_id(0), blocks_per_batch, zm_ref, zp_ref, zn_ref, mu_ref, w2_ref, a2_ref,
                            g2_ref, w0_ref, a0_ref, kkw_ref, ka_ref, rk_ref, ones_ref):
        pass
    g_out[...] = res["g"]
    bon_out[...] = res["bonus"]
    for _ in _chunk_maps(res["r"], res["v"], res["kk"], res["lw_d"], res["k_d"], res["b_d"],
                         p_out, q_out, rp_out, y0_out):
        pass


def _rwkv_chunks(pz, seq, ctx_len, bsz, mu, w2bd, a2bd, g2, w0, a0, kkw, ka, rk, ones_bd):
    n = pz.shape[0]
    assert ctx_len == ROW_BLOCK and seq % ROW_BLOCK == 0 and n == bsz * (ctx_len + seq)
    sub = ROW_BLOCK // GRID_W
    last = n // GRID_W - 1
    cps = ROW_BLOCK // CHUNK
    nblk = n // ROW_BLOCK
    full = lambda a: pl.BlockSpec(a.shape, lambda i: (0,) * a.ndim)
    row = pl.BlockSpec((ROW_BLOCK, D_RWKV), lambda i: (i, 0))
    row2 = pl.BlockSpec((2, ROW_BLOCK, D_RWKV), lambda i: (0, i, 0))
    mat = pl.BlockSpec((2, cps, HEAD, D_RWKV), lambda i: (0, i, 0, 0))
    s1 = jax.ShapeDtypeStruct((n, D_RWKV), F32)
    s2 = jax.ShapeDtypeStruct((2, n, D_RWKV), F32)
    sm = jax.ShapeDtypeStruct((2, n // CHUNK, HEAD, D_RWKV), F32)
    consts = (mu, w2bd, a2bd, g2, w0, a0, kkw, ka, rk, ones_bd)
    return pl.pallas_call(
        functools.partial(_rwkv_chunks_body, (ctx_len + seq) // ROW_BLOCK),
        grid=(nblk,),
        in_specs=[pl.BlockSpec((ROW_BLOCK, D_Z), lambda i: (i, 0)),
                  pl.BlockSpec((GRID_W, D_Z), lambda i: (jnp.maximum(i * sub - 1, 0), 0)),
                  pl.BlockSpec((GRID_W, D_Z), lambda i: (jnp.minimum((i + 1) * sub, last), 0))]
                 + [full(a) for a in consts],
        out_specs=[row, row, mat, mat, row2, row2],
        out_shape=[s1, s1, sm, sm, s2, s2],
        compiler_params=_params("parallel"),
        name="rwkv_chunks",
    )(pz, pz, pz, *consts)


SCAN_CHUNKS = 4


def _scan_body(pf_ref, qf_ref, rpf_ref, y0f_ref, pb_ref, qb_ref, rpb_ref, y0b_ref, yf_out, yb_out, h_scr):
    @pl.when(pl.program_id(0) == 0)
    def _():
        h_scr[...] = jnp.zeros_like(h_scr)

    cs = CHUNK
    bsz = h_scr.shape[1]
    n_pairs = D_RWKV // LANES
    lane = lax.broadcasted_iota(jnp.int32, (cs, LANES), 1)
    lo_half = lane < HEAD
    zero_bf = jnp.zeros((cs, LANES), BF16)

    def bd(y):
        return jnp.concatenate([jnp.where(lo_half, y, zero_bf), jnp.where(lo_half, zero_bf, y)], axis=0)

    dirs = ((pf_ref, qf_ref, rpf_ref, y0f_ref, yf_out), (pb_ref, qb_ref, rpb_ref, y0b_ref, yb_out))
    chains = [(d, b, p) for d in range(2) for b in range(bsz) for p in range(n_pairs)]
    state = [h_scr[d, b, :, p * LANES:(p + 1) * LANES] for d, b, p in chains]
    for s in range(SCAN_CHUNKS):
        outs = []
        for (d, b, p), hcur in zip(chains, state):
            p_ref, _, rp_ref, _, _ = dirs[d]
            c = s if d == 0 else SCAN_CHUNKS - 1 - s
            ls = slice(p * LANES, (p + 1) * LANES)
            x = jnp.concatenate([p_ref[0, b, c, :, ls], rp_ref[0, b, c * cs:(c + 1) * cs, ls]], axis=0)
            xh, xl = _split2(x)
            hh, hl = _split2(hcur)
            hh, hl = bd(hh), bd(hl)
            outs.append(_dot(xh, hh) + (_dot(xh, hl) + _dot(xl, hh)))
        new_state = []
        for (d, b, p), o in zip(chains, outs):
            _, q_ref, _, y0_ref, y_out = dirs[d]
            c = s if d == 0 else SCAN_CHUNKS - 1 - s
            ls = slice(p * LANES, (p + 1) * LANES)
            new_state.append(o[:cs] + q_ref[0, b, c, :, ls])
            y_out[b, c * cs:(c + 1) * cs, ls] = y0_ref[0, b, c * cs:(c + 1) * cs, ls] + o[cs:]
        state = new_state
    for (d, b, p), hcur in zip(chains, state):
        h_scr[d, b, :, p * LANES:(p + 1) * LANES] = hcur


def _chunk_scan(p, q, rp, y0, seq, ctx_len, bsz):
    n = rp.shape[1]
    rows_b = ctx_len + seq
    ncb = rows_b // CHUNK
    assert ctx_len % (SCAN_CHUNKS * CHUNK) == 0 and seq % (SCAN_CHUNKS * CHUNK) == 0
    steps = ncb // SCAN_CHUNKS
    ctx_steps = ctx_len // (SCAN_CHUNKS * CHUNK)
    p5 = p.reshape(2, bsz, ncb, HEAD, D_RWKV)
    q5 = q.reshape(2, bsz, ncb, HEAD, D_RWKV)
    rp4 = rp.reshape(2, bsz, rows_b, D_RWKV)
    y04 = y0.reshape(2, bsz, rows_b, D_RWKV)
    pos_f = lambda i: i
    pos_b = lambda i: jnp.where(i < ctx_steps, ctx_steps - 1 - i, steps - 1 - (i - ctx_steps))
    rows = SCAN_CHUNKS * CHUNK

    def specs(d, pos):
        mat = pl.BlockSpec((1, bsz, SCAN_CHUNKS, HEAD, D_RWKV), lambda i: (d, 0, pos(i), 0, 0))
        tok = pl.BlockSpec((1, bsz, rows, D_RWKV), lambda i: (d, 0, pos(i), 0))
        return [mat, mat, tok, tok]

    out_f = pl.BlockSpec((bsz, rows, D_RWKV), lambda i: (0, pos_f(i), 0))
    out_b = pl.BlockSpec((bsz, rows, D_RWKV), lambda i: (0, pos_b(i), 0))
    shp = jax.ShapeDtypeStruct((bsz, rows_b, D_RWKV), F32)
    yf, yb = pl.pallas_call(
        _scan_body,
        grid=(steps,),
        in_specs=specs(0, pos_f) + specs(1, pos_b),
        out_specs=[out_f, out_b],
        out_shape=[shp, shp],
        scratch_shapes=[pltpu.VMEM((2, bsz, HEAD, D_RWKV), F32)],
        compiler_params=_params("arbitrary"),
        name="chunk_scan",
    )(p5, q5, rp4, y04, p5, q5, rp4, y04)
    return yf.reshape(n, D_RWKV), yb.reshape(n, D_RWKV)


def _hilo(a):
    if isinstance(a, np.ndarray):
        a = a.astype(np.float32)
        hi = a.astype(BF16)
        return jnp.asarray(hi), jnp.asarray((a - hi.astype(np.float32)).astype(BF16))
    hi = a.astype(BF16)
    return hi, (a - hi.astype(F32)).astype(BF16)


def _channel_tables(length):
    j = np.arange(FOUR_GROUP)
    ang = 2.0 * np.pi * np.outer(j, j) / FOUR_GROUP
    scale = 1.0 / np.sqrt(float(length) * FOUR_GROUP)
    groups = D_FOUR // FOUR_GROUP
    c4 = np.kron(np.eye(groups), np.cos(ang)) * scale
    s4 = np.kron(np.eye(groups), np.sin(ang)) * scale
    return _hilo(c4) + _hilo(s4)


def _four_finish(fr, fi, c4h, c4l, s4h, s4l, gain):
    y = _dot3r(fr, c4h, c4l) + _dot3r(fi, s4h, s4l)
    return _rms(y) * gain


def _four_direct_body(f_ref, mh_ref, ml_ref, c4h, c4l, s4h, s4l, gain_ref, o_ref):
    length = f_ref.shape[1]
    fc = _dot3c(mh_ref[...], ml_ref[...], f_ref[0])
    o_ref[0] = _four_finish(fc[:length], fc[length:], c4h[...], c4l[...], s4h[...], s4l[...], gain_ref[...])


def _fourier_direct(f, gain):
    bsz, length, _ = f.shape
    t = np.arange(length)
    ang = 2.0 * np.pi * (np.outer(t, t) % length) / length
    mh, ml = _hilo(np.concatenate([np.cos(ang), -np.sin(ang)], axis=0))
    consts = (mh, ml) + _channel_tables(length) + (gain,)
    full = lambda a: pl.BlockSpec(a.shape, lambda b: (0,) * a.ndim)
    blk = pl.BlockSpec((1, length, D_FOUR), lambda b: (b, 0, 0))
    return pl.pallas_call(
        _four_direct_body,
        grid=(bsz,),
        in_specs=[blk] + [full(a) for a in consts],
        out_specs=blk,
        out_shape=jax.ShapeDtypeStruct(f.shape, F32),
        compiler_params=_params("parallel"),
        name="fourier_direct",
    )(f, *consts)


def _four_stage1_body(f_ref, mh_ref, ml_ref, o_ref):
    o_ref[0] = _dot3c(mh_ref[...], ml_ref[...], f_ref[0])


def _four_stage2_body(zr_ref, zi_ref, mh_ref, ml_ref, c4h, c4l, s4h, s4l, gain_ref, o_ref):
    group, inner = zr_ref.shape[1], zr_ref.shape[2]
    fc = [_dot3c(mh_ref[j], ml_ref[j], jnp.concatenate([zr_ref[0, j], zi_ref[0, j]], axis=0))
          for j in range(group)]
    o_ref[0] = jnp.concatenate(
        [_four_finish(x[:inner], x[inner:], c4h[...], c4l[...], s4h[...], s4l[...], gain_ref[...]) for x in fc],
        axis=1)


def _fourier_long(f, gain):
    bsz, length, _ = f.shape
    l2 = FOUR_INNER
    l1 = length // l2
    assert l1 * l2 == length and l1 % 16 == 0
    cols = l2 * D_FOUR
    col_tile = 4096
    th = np.arange(l1)
    ang1 = 2.0 * np.pi * (np.outer(th, th) % l1) / l1
    m1h, m1l = _hilo(np.concatenate([np.cos(ang1), -np.sin(ang1)], axis=0))
    z = pl.pallas_call(
        _four_stage1_body,
        grid=(bsz, cols // col_tile),
        in_specs=[pl.BlockSpec((1, l1, col_tile), lambda b, c: (b, 0, c)),
                  pl.BlockSpec(m1h.shape, lambda b, c: (0, 0)),
                  pl.BlockSpec(m1l.shape, lambda b, c: (0, 0))],
        out_specs=pl.BlockSpec((1, 2 * l1, col_tile), lambda b, c: (b, 0, c)),
        out_shape=jax.ShapeDtypeStruct((bsz, 2 * l1, cols), F32),
        compiler_params=_params("parallel", "parallel"),
        name="fourier_stage1",
    )(f.reshape(bsz, l1, cols), m1h, m1l)
    z = z.reshape(bsz, 2 * l1, l2, D_FOUR)

    ma = np.arange(l1)[:, None, None]
    mb = np.arange(l2)[None, :, None]
    tl = np.arange(l2)[None, None, :]
    ang2 = 2.0 * np.pi * (((ma + l1 * mb) * tl) % length) / length
    cos2, sin2 = np.cos(ang2), np.sin(ang2)
    m2 = np.concatenate([np.concatenate([cos2, sin2], axis=2), np.concatenate([-sin2, cos2], axis=2)], axis=1)
    m2h, m2l = _hilo(m2)
    consts = _channel_tables(length) + (gain,)
    full = lambda a: pl.BlockSpec(a.shape, lambda b, m: (0,) * a.ndim)
    grp = FOUR_GROUP_STEP
    tab = pl.BlockSpec((grp, 2 * l2, 2 * l2), lambda b, m: (m, 0, 0))
    out = pl.pallas_call(
        _four_stage2_body,
        grid=(bsz, l1 // grp),
        in_specs=[pl.BlockSpec((1, grp, l2, D_FOUR), lambda b, m: (b, m, 0, 0)),
                  pl.BlockSpec((1, grp, l2, D_FOUR), lambda b, m: (b, l1 // grp + m, 0, 0)),
                  tab, tab] + [full(a) for a in consts],
        out_specs=pl.BlockSpec((1, l2, grp * D_FOUR), lambda b, m: (b, 0, m)),
        out_shape=jax.ShapeDtypeStruct((bsz, l2, l1 * D_FOUR), F32),
        compiler_params=_params("parallel", "parallel"),
        name="fourier_stage2",
    )(z, z, m2h, m2l, *consts)
    return out.reshape(bsz, length, D_FOUR)


def _route(logits):
    lane = lax.broadcasted_iota(jnp.int32, (1, LANES), 1)
    lane_f = lane.astype(F32)
    neg = jnp.float32(-1e30)
    big = jnp.float32(1e9)
    gl = jnp.where(lane < N_GROUPS, logits, neg)
    gmax = jnp.max(gl, axis=-1, keepdims=True)
    pg_top = 1.0 / jnp.sum(jnp.exp(gl - gmax), axis=-1, keepdims=True)
    grp = jnp.min(jnp.where(gl == gmax, lane_f, big), axis=-1, keepdims=True)
    e_lane = lane - N_GROUPS
    in_grp = (e_lane >= 0) & (e_lane < N_EXPERTS) & ((e_lane >> 3).astype(F32) == grp)
    el = jnp.where(in_grp, logits, neg)
    m1 = jnp.max(el, axis=-1, keepdims=True)
    i1 = jnp.min(jnp.where(el == m1, lane_f, big), axis=-1, keepdims=True)
    el2 = jnp.where(lane_f == i1, neg, el)
    m2 = jnp.max(el2, axis=-1, keepdims=True)
    i2 = jnp.min(jnp.where(el2 == m2, lane_f, big), axis=-1, keepdims=True)
    e2 = jnp.exp(m2 - m1)
    den = 1.0 + e2
    return pg_top / den, pg_top * e2 / den, i1 - N_GROUPS, i2 - N_GROUPS


def _post_body(n_src, blocks_per_batch, *refs):
    x_refs = refs[:n_src]
    (yf_ref, yb_ref, bon_ref, g_ref, cv_ref, cvp_ref, cvn_ref, fx_ref, fc_ref, wo_ref, gnw_ref, gnb_ref, cw_ref,
     cg_ref, g1_ref, sc2_ref, sh2_ref, n2_ref, wrh_ref, wrl_ref, rb_ref, ones_ref, x_out, h_out, route_out,
     count_out, count_scr) = refs[n_src:]
    i = pl.program_id(0)

    @pl.when(i == 0)
    def _():
        count_scr[...] = jnp.zeros_like(count_scr)

    ones_pair = ones_ref[...]
    y = yf_ref[...] + yb_ref[...]
    mu = _headsums([y], ones_pair)[0] * (1.0 / HEAD)
    yc = y - mu
    var = _headsums([yc * yc], ones_pair)[0] * (1.0 / HEAD)
    yn = yc * lax.rsqrt(var + GN_EPS) * gnw_ref[...] + gnb_ref[...]
    o_rwkv = (yn + bon_ref[...]) * g_ref[...]

    cv = cv_ref[...]
    tb = cv.shape[0]
    t = lax.broadcasted_iota(jnp.int32, (tb, 1), 0)
    seq_pos = i % blocks_per_batch
    first = (seq_pos <= 1).astype(F32)
    final = jnp.logical_or(seq_pos == 0, seq_pos == blocks_per_batch - 1).astype(F32)
    zc = cv[:, D_CONV:2 * D_CONV] * cv[:, 2 * D_CONV:]
    zp_row = cvp_ref[7:8, D_CONV:2 * D_CONV] * cvp_ref[7:8, 2 * D_CONV:] * (1.0 - first)
    zn_row = cvn_ref[0:1, D_CONV:2 * D_CONV] * cvn_ref[0:1, 2 * D_CONV:] * (1.0 - final)
    prev = jnp.where(t == 0, zp_row, pltpu.roll(zc, 1, axis=0))
    nxt = jnp.where(t == tb - 1, zn_row, pltpu.roll(zc, tb - 1, axis=0))
    cw = cw_ref[...]
    conv = cv[:, :D_CONV] * (cw[0:1] * prev + cw[1:2] * zc + cw[2:3] * nxt)
    conv_o = _rms(conv) * cg_ref[...]

    four_o = jnp.where(seq_pos == 0, fc_ref[...], fx_ref[...])
    mix = (_dot(o_rwkv.astype(BF16), wo_ref[0:D_RWKV, :])
           + _dot(conv_o.astype(BF16), wo_ref[D_RWKV:D_RWKV + D_CONV, :])
           + _dot(four_o.astype(BF16), wo_ref[D_RWKV + D_CONV:, :]))
    x = _token_rows(x_refs, seq_pos == 0) + g1_ref[0] * mix
    x_out[...] = x
    h2 = _rms(x) * n2_ref[...] * (1.0 + sc2_ref[0]) + sh2_ref[0]
    h_out[...] = h2

    gate0, gate1, e0, e1 = _route(_dot3r(h2, wrh_ref[...], wrl_ref[...]) + rb_ref[...])
    lane = lax.broadcasted_iota(jnp.int32, (1, LANES), 1)
    lane_f = lane.astype(F32)
    oh0 = (lane_f == e0).astype(F32)
    oh1 = (lane_f == e1).astype(F32)
    ri = lax.broadcasted_iota(jnp.int32, (tb, tb), 0)
    rj = lax.broadcasted_iota(jnp.int32, (tb, tb), 1)
    earlier = (rj < ri).astype(F32).astype(BF16)
    seen = count_scr[...]
    tot0 = jnp.sum(oh0, axis=0, keepdims=True)
    before = _dot(earlier, jnp.concatenate([oh0, oh1], axis=1).astype(BF16))
    before0 = before[:, :LANES] + seen
    before1 = before[:, LANES:] + (seen + tot0)
    rank0 = jnp.sum(oh0 * before0, axis=-1, keepdims=True)
    rank1 = jnp.sum(oh1 * before1, axis=-1, keepdims=True)
    seen = seen + tot0 + jnp.sum(oh1, axis=0, keepdims=True)
    count_scr[...] = seen
    count_out[...] = seen
    route_out[...] = jnp.where(
        lane == 0, gate0, jnp.where(lane == 1, gate1, jnp.where(lane == 2, e0, jnp.where(
            lane == 3, e1, jnp.where(lane == 4, rank0, jnp.where(lane == 5, rank1, 0.0))))))


def _post(yf, yb, bonus, g, pcv, four_x, four_c, xs, seq, ctx_len, ent, wo_bf, gnw, gnb, cw, cg, g1, sc2, sh2, n2,
          wrh, wrl, rb, ones_bd):
    n = yf.shape[0]
    assert ctx_len == ROW_BLOCK
    sub = ROW_BLOCK // 8
    last = n // 8 - 1
    bpb = (ctx_len + seq) // ROW_BLOCK
    full = lambda a: pl.BlockSpec(a.shape, lambda i: (0,) * a.ndim)
    row = lambda w: pl.BlockSpec((ROW_BLOCK, w), lambda i: (i, 0))
    mod = pl.BlockSpec((1, 1, D_MODEL), lambda i: (ent(i), 0, 0))
    return pl.pallas_call(
        functools.partial(_post_body, len(xs), bpb),
        grid=(n // ROW_BLOCK,),
        in_specs=_token_specs(xs, bpb) + [
                  row(D_RWKV), row(D_RWKV), row(D_RWKV), row(D_RWKV), row(3 * D_CONV),
                  pl.BlockSpec((8, 3 * D_CONV), lambda i: (jnp.maximum(i * sub - 1, 0), 0)),
                  pl.BlockSpec((8, 3 * D_CONV), lambda i: (jnp.minimum((i + 1) * sub, last), 0)),
                  pl.BlockSpec((ROW_BLOCK, D_FOUR), lambda i: (_latent_block(i, bpb), 0)),
                  pl.BlockSpec((ROW_BLOCK, D_FOUR), lambda i: (i // bpb, 0)),
                  full(wo_bf), full(gnw), full(gnb), full(cw), full(cg),
                  mod, mod, mod, full(n2), full(wrh), full(wrl), full(rb), full(ones_bd)],
        out_specs=[row(D_MODEL), row(D_MODEL), row(LANES), pl.BlockSpec((1, LANES), lambda i: (0, 0))],
        out_shape=[jax.ShapeDtypeStruct((n, D_MODEL), F32), jax.ShapeDtypeStruct((n, D_MODEL), F32),
                   jax.ShapeDtypeStruct((n, LANES), F32), jax.ShapeDtypeStruct((1, LANES), F32)],
        scratch_shapes=[pltpu.VMEM((1, LANES), F32)],
        compiler_params=_params("arbitrary"),
        name="post_mix",
    )(*xs, yf, yb, bonus, g, pcv, pcv, pcv, four_x, four_c, wo_bf, gnw, gnb, cw, cg, g1, sc2, sh2, n2, wrh, wrl,
      rb, ones_bd)


def _scatter_rows_body(zb_ref, zv_ref, dest_ref, h_ref, xb_ref, zero_buf, sem):
    tb = h_ref.shape[0]

    @pl.when(pl.program_id(0) == 0)
    def _():
        zero_buf[...] = jnp.zeros_like(zero_buf)

        def block_fill(j):
            start = pl.multiple_of(zb_ref[j] * MOE_ROWS, MOE_ROWS)
            return pltpu.make_async_copy(zero_buf, xb_ref.at[pl.ds(start, MOE_ROWS), :], sem)

        def fill_start(j, carry):
            @pl.when(zv_ref[j] != 0)
            def _():
                block_fill(j).start()
            return carry

        def fill_wait(j, carry):
            @pl.when(zv_ref[j] != 0)
            def _():
                block_fill(j).wait()
            return carry

        lax.fori_loop(0, zb_ref.shape[0], fill_start, 0)
        lax.fori_loop(0, zb_ref.shape[0], fill_wait, 0)

    def row_copy(t, k):
        return pltpu.make_async_copy(h_ref.at[pl.ds(t, 1), :],
                                     xb_ref.at[pl.ds(dest_ref[0, 0, k * tb + t], 1), :], sem)

    def issue(t, carry):
        row_copy(t, 0).start()
        row_copy(t, 1).start()
        return carry

    def drain(t, carry):
        row_copy(t, 0).wait()
        row_copy(t, 1).wait()
        return carry

    for t in range(tb):
        issue(t, 0)
    lax.fori_loop(0, tb, drain, 0, unroll=DMA_UNROLL)


DMA_UNROLL = 8
SCATTER_ROWS = 512


def _scatter_rows(h2, dest, zero_blocks, zero_valid, cap):
    n = h2.shape[0]
    steps = n // SCATTER_ROWS
    grid_spec = pltpu.PrefetchScalarGridSpec(
        num_scalar_prefetch=2,
        grid=(steps,),
        in_specs=[pl.BlockSpec((1, 1, 2 * SCATTER_ROWS), lambda i, zb, zv: (i, 0, 0), memory_space=pltpu.SMEM),
                  pl.BlockSpec((SCATTER_ROWS, D_MODEL), lambda i, zb, zv: (i, 0))],
        out_specs=pl.BlockSpec(memory_space=pl.ANY),
        scratch_shapes=[pltpu.VMEM((MOE_ROWS, D_MODEL), F32), pltpu.SemaphoreType.DMA(())],
    )
    return pl.pallas_call(
        _scatter_rows_body,
        grid_spec=grid_spec,
        out_shape=jax.ShapeDtypeStruct((cap, D_MODEL), F32),
        compiler_params=_params("arbitrary", disable_bounds_checks=True),
        name="moe_scatter",
    )(zero_blocks, zero_valid, _block_rows(dest, SCATTER_ROWS), h2)


def _expert_body(layer, be_ref, meta_ref, nu_ref, xb_ref, wg_hbm, wu_hbm, wd_hbm, o_ref,
                 land_g, land_u, land_d, wg_s, wu_s, wd_s, sems):
    i = pl.program_id(0)
    used = i < nu_ref[0]
    meta = meta_ref[i]
    first = (meta & 1) != 0
    final = (meta & 2) != 0
    has_next = (meta & 4) != 0
    slot = (meta >> 3) & 1
    nxt = meta >> 4

    def fetch(e):
        return [pltpu.make_async_copy(w.at[layer, e], land, sems.at[k])
                for k, (w, land) in enumerate(((wg_hbm, land_g), (wu_hbm, land_u), (wd_hbm, land_d)))]

    def cast_to(s):
        wg_s[s] = land_g[...].astype(BF16)
        wu_s[s] = land_u[...].astype(BF16)
        wd_s[s] = land_d[...].astype(BF16)

    @pl.when(i == 0)
    def _():
        for c in fetch(be_ref[0]):
            c.start()
        for c in fetch(be_ref[0]):
            c.wait()
        cast_to(0)

    @pl.when(used & first & has_next)
    def _():
        for c in fetch(nxt):
            c.start()

    @pl.when(used)
    def _():
        xb = xb_ref[...].astype(BF16)
        gate = _dot(xb, wg_s[slot])
        up = _dot(xb, wu_s[slot])
        act = gate * _sigmoid(gate) * up
        o_ref[...] = _dot(act.astype(BF16), wd_s[slot])

    @pl.when(used & final & has_next)
    def _():
        for c in fetch(nxt):
            c.wait()
        cast_to(1 - slot)

    @pl.when(jnp.logical_not(used))
    def _():
        o_ref[...] = jnp.zeros_like(o_ref)


def _experts(xb, block_e, block_meta, n_used, exp_gate, exp_up, exp_down, layer):
    cap = xb.shape[0]
    nb = cap // MOE_ROWS
    blk = lambda i, nu: jnp.minimum(i, nu[0] - 1)
    grid_spec = pltpu.PrefetchScalarGridSpec(
        num_scalar_prefetch=3,
        grid=(nb,),
        in_specs=[pl.BlockSpec((MOE_ROWS, D_MODEL), lambda i, be, meta, nu: (blk(i, nu), 0)),
                  pl.BlockSpec(memory_space=pl.ANY), pl.BlockSpec(memory_space=pl.ANY),
                  pl.BlockSpec(memory_space=pl.ANY)],
        out_specs=pl.BlockSpec((MOE_ROWS, D_MODEL), lambda i, be, meta, nu: (i, 0)),
        scratch_shapes=[pltpu.VMEM((D_MODEL, D_EXPERT), F32), pltpu.VMEM((D_MODEL, D_EXPERT), F32),
                        pltpu.VMEM((D_EXPERT, D_MODEL), F32),
                        pltpu.VMEM((2, D_MODEL, D_EXPERT), BF16), pltpu.VMEM((2, D_MODEL, D_EXPERT), BF16),
                        pltpu.VMEM((2, D_EXPERT, D_MODEL), BF16), pltpu.SemaphoreType.DMA((3,))],
    )
    return pl.pallas_call(
        functools.partial(_expert_body, layer),
        grid_spec=grid_spec,
        out_shape=jax.ShapeDtypeStruct((cap, D_MODEL), F32),
        compiler_params=_params("arbitrary"),
        name="experts",
    )(block_e, block_meta, n_used, xb, exp_gate, exp_up, exp_down)


def _block_rows(dest, rows):
    n = dest.shape[1]
    return jnp.stack([dest[0].reshape(n // rows, rows), dest[1].reshape(n // rows, rows)],
                     axis=1).reshape(n // rows, 1, 2 * rows)


def _dispatch(route, counts):
    n = route.shape[0]
    counts = counts[0, :N_EXPERTS].astype(jnp.int32)
    pcounts = (counts + MOE_ROWS - 1) // MOE_ROWS * MOE_ROWS
    pend = jnp.cumsum(pcounts)
    pstart = pend - pcounts
    experts = jnp.arange(N_EXPERTS, dtype=jnp.int32)
    dest = jnp.stack([
        jnp.sum(jnp.where(route[:, 2 + k].astype(jnp.int32)[None, :] == experts[:, None], pstart[:, None], 0),
                axis=0) + route[:, 4 + k].astype(jnp.int32) for k in range(2)])
    nb = -(-2 * n // MOE_ROWS) + N_EXPERTS
    block_start = jnp.arange(nb, dtype=jnp.int32) * MOE_ROWS
    block_e = jnp.minimum(jnp.sum((pend[None, :] <= block_start[:, None]).astype(jnp.int32), axis=1),
                          N_EXPERTS - 1)
    n_used = pend[-1] // MOE_ROWS
    j = jnp.arange(nb, dtype=jnp.int32)
    used = j < n_used
    prev_e = jnp.concatenate([block_e[:1] - 1, block_e[:-1]])
    next_blk_e = jnp.concatenate([block_e[1:], block_e[-1:] + 1])
    first = used & (prev_e != block_e)
    final = used & ((j == n_used - 1) | (next_blk_e != block_e))
    later = (experts[None, :] > experts[:, None]) & (pcounts[None, :] > 0)
    next_expert = jnp.min(jnp.where(later, experts[None, :], N_EXPERTS), axis=1)
    nxt = jnp.sum(jnp.where(block_e[:, None] == experts[None, :], next_expert[None, :], 0), axis=1)
    slot = (jnp.cumsum(first.astype(jnp.int32)) - 1) & 1
    block_meta = (first.astype(jnp.int32) | (final.astype(jnp.int32) << 1)
                  | ((nxt < N_EXPERTS).astype(jnp.int32) << 2) | (slot << 3)
                  | (jnp.minimum(nxt, N_EXPERTS - 1) << 4)).astype(jnp.int32)
    spare = n_used + experts
    zero_blocks = jnp.concatenate([pend // MOE_ROWS - 1, spare]).astype(jnp.int32)
    zero_valid = jnp.concatenate([pcounts > 0, spare < nb]).astype(jnp.int32)
    return dest, block_e, block_meta, n_used.astype(jnp.int32).reshape(1), zero_blocks, zero_valid, nb * MOE_ROWS


def _combine_body(final_norm, dest_ref, next_ref, x_ref, route_ref, g2_ref, fg_ref, yb_ref, o_ref, buf, sems):
    i = pl.program_id(0)
    tb = x_ref.shape[0]
    slot = i % 2

    def gather(idx_ref, s, wait, unrolled=False):
        def row_copy(t, k):
            return pltpu.make_async_copy(yb_ref.at[pl.ds(idx_ref[0, 0, k * tb + t], 1), :],
                                         buf.at[s, k, pl.ds(t, 1), :], sems.at[s])

        def body(t, carry):
            for k in range(2):
                if wait:
                    row_copy(t, k).wait()
                else:
                    row_copy(t, k).start()
            return carry

        if unrolled:
            for t in range(tb):
                body(t, 0)
        else:
            lax.fori_loop(0, tb, body, 0, unroll=DMA_UNROLL)

    @pl.when(i == 0)
    def _():
        gather(dest_ref, slot, False)

    for s in range(2):
        @pl.when(jnp.logical_and(i + 1 < pl.num_programs(0), slot == 1 - s))
        def _():
            gather(next_ref, s, False, unrolled=True)

    gather(dest_ref, slot, True)
    route = route_ref[...]
    x = x_ref[...] + g2_ref[0] * (route[:, 0:1] * buf[slot, 0] + route[:, 1:2] * buf[slot, 1])
    if final_norm:
        x = _rms(x) * fg_ref[...]
    o_ref[...] = x


def _combine(xa, yb, route, dest3, g2, final_g, blk, ent, n_out_blocks, final_norm):
    nxt = lambda i: blk(jnp.minimum(i + 1, n_out_blocks - 1))
    return pl.pallas_call(
        functools.partial(_combine_body, final_norm),
        grid=(n_out_blocks,),
        in_specs=[pl.BlockSpec((1, 1, 2 * ROW_BLOCK), lambda i: (blk(i), 0, 0), memory_space=pltpu.SMEM),
                  pl.BlockSpec((1, 1, 2 * ROW_BLOCK), lambda i: (nxt(i), 0, 0), memory_space=pltpu.SMEM),
                  pl.BlockSpec((ROW_BLOCK, D_MODEL), lambda i: (blk(i), 0)),
                  pl.BlockSpec((ROW_BLOCK, LANES), lambda i: (blk(i), 0)),
                  pl.BlockSpec((1, 1, D_MODEL), lambda i: (ent(blk(i)), 0, 0)),
                  pl.BlockSpec(final_g.shape, lambda i: (0, 0)),
                  pl.BlockSpec(memory_space=pl.ANY)],
        out_specs=pl.BlockSpec((ROW_BLOCK, D_MODEL), lambda i: (i, 0)),
        out_shape=jax.ShapeDtypeStruct((n_out_blocks * ROW_BLOCK, D_MODEL), F32),
        scratch_shapes=[pltpu.VMEM((2, 2, ROW_BLOCK, D_MODEL), F32), pltpu.SemaphoreType.DMA((2,))],
        compiler_params=_params("arbitrary", disable_bounds_checks=True),
        name="moe_combine",
    )(dest3, dest3, xa, route, g2, final_g, yb)


def _block_diag2(w):
    k, n = w.shape[1], w.shape[2]
    z = jnp.zeros((k, n), w.dtype)
    return jnp.concatenate([jnp.concatenate([w[0], z], axis=1), jnp.concatenate([z, w[1]], axis=1)], axis=0)


def kernel(x, c, ctx, c_ctx, ada_w, ada_b, norm1_g, norm2_g, w_in, mu_shift, decay_w0, decay_w2, iclr_a0, iclr_a2, gate_g2, k_k, k_a, r_k, gn_w, gn_b, conv_w, conv_gain, four_gain, w_out, router_g_w, router_g_b, router_e_w, router_e_b, exp_gate, exp_up, exp_down, final_g):
    bsz, seq, d = x.shape
    ctx_len = ctx.shape[1]
    depth = ada_w.shape[0]
    rows_b = ctx_len + seq
    n = bsz * rows_b
    bpb = rows_b // ROW_BLOCK
    lat_bpb = seq // ROW_BLOCK
    ent = lambda i: jnp.where(i % bpb == 0, bsz, i // bpb)

    xs = (x.reshape(bsz * seq, d), ctx.reshape(bsz * ctx_len, d))
    cvec = jnp.concatenate([c, c_ctx[None, :]], axis=0)
    cvec = cvec * jax.nn.sigmoid(cvec)
    head_id = np.arange(LANES) // HEAD
    ones_bd = jnp.asarray(head_id[:, None] == head_id[None, :], BF16)
    row2 = lambda a: a.reshape(1, -1)

    for l in range(depth):
        last = l == depth - 1
        mod = jnp.dot(cvec, ada_w[l], precision=lax.Precision.HIGHEST) + ada_b[l]
        sh1, sc1, g1, sh2, sc2, g2 = [mod[:, j * d:(j + 1) * d].reshape(bsz + 1, 1, d) for j in range(6)]

        pz, pcv, pfo_x, pfo_c = _inproj(xs, row2(norm1_g[l]), sc1, sh1, w_in[l].astype(BF16), ent, bsz, bpb)
        g, bonus, pm, qm, rp, y0 = _rwkv_chunks(
            pz, seq, ctx_len, bsz, row2(mu_shift[l]), _block_diag2(decay_w2[l]), _block_diag2(iclr_a2[l]),
            gate_g2[l], row2(decay_w0[l]), row2(iclr_a0[l]), row2(k_k[l]), row2(k_a[l]), row2(r_k[l]), ones_bd)
        yf, yb = _chunk_scan(pm, qm, rp, y0, seq, ctx_len, bsz)

        fgain = row2(four_gain[l])
        four_x = _fourier_long(pfo_x.reshape(bsz, seq, D_FOUR), fgain).reshape(bsz * seq, D_FOUR)
        four_c = _fourier_direct(pfo_c.reshape(bsz, ctx_len, D_FOUR), fgain).reshape(bsz * ctx_len, D_FOUR)

        wr = jnp.zeros((d, LANES), F32).at[:, :N_GROUPS].set(router_g_w[l])
        wr = wr.at[:, N_GROUPS:N_GROUPS + N_EXPERTS].set(router_e_w[l])
        rb = jnp.zeros((1, LANES), F32).at[0, :N_GROUPS].set(router_g_b[l])
        rb = rb.at[0, N_GROUPS:N_GROUPS + N_EXPERTS].set(router_e_b[l])
        wrh, wrl = _hilo(wr)
        xa, h2, route, counts = _post(yf, yb, bonus, g, pcv, four_x, four_c, xs, seq, ctx_len, ent,
                                      w_out[l].astype(BF16), row2(gn_w[l]), row2(gn_b[l]), conv_w[l],
                                      row2(conv_gain[l]), g1, sc2, sh2, row2(norm2_g[l]), wrh, wrl, rb, ones_bd)

        dest, block_e, block_meta, n_used, zero_blocks, zero_valid, cap = _dispatch(route, counts)
        xe = _scatter_rows(h2, dest, zero_blocks, zero_valid, cap)
        ye = _experts(xe, block_e, block_meta, n_used, exp_gate, exp_up, exp_down, l)
        dest3 = _block_rows(dest, ROW_BLOCK)
        if last:
            blk = lambda i: (i // lat_bpb) * bpb + 1 + i % lat_bpb
            xa = _combine(xa, ye, route, dest3, g2, row2(final_g), blk, ent, bsz * lat_bpb, True)
        else:
            xa = _combine(xa, ye, route, dest3, g2, row2(final_g), lambda i: i, ent, n // ROW_BLOCK, False)
            xs = (xa,)

    return xa.reshape(bsz, seq, d)
```

```python
import functools

import numpy as np
import jax
import jax.numpy as jnp
from jax import lax
from jax.experimental import pallas as pl
from jax.experimental.pallas import tpu as pltpu

F32 = jnp.float32
BF16 = jnp.bfloat16

D_MODEL = 1024
HEAD = 64
D_RWKV = 512
H_RWKV = D_RWKV // HEAD
D_CONV = 256
D_FOUR = 256
FOUR_GROUP = 64
D_Z = 3 * D_RWKV + 2 * 64 + 2 * 64 + 128
D_IN = D_Z + 3 * D_CONV + D_FOUR
GRID_W = 64
N_GROUPS = 4
EXPERTS_PER_GROUP = 8
N_EXPERTS = N_GROUPS * EXPERTS_PER_GROUP
D_EXPERT = D_MODEL // 2
RMS_EPS = 1e-6
GN_EPS = 64e-5

CHUNK = 64
ROW_BLOCK = 256
MOE_ROWS = 256
FOUR_INNER = 128
FOUR_GROUP_STEP = 4
LANES = 128
VMEM_LIMIT = 48 * 1024 * 1024

NN = (((1,), (0,)), ((), ()))
NT = (((1,), (1,)), ((), ()))
TN = (((0,), (0,)), ((), ()))


def _params(*sem, **kw):
    return pltpu.CompilerParams(dimension_semantics=sem, vmem_limit_bytes=VMEM_LIMIT, **kw)


def _split2(x):
    hi = x.astype(BF16)
    lo = (x - hi.astype(F32)).astype(BF16)
    return hi, lo


def _dot(a, b, dims=NN):
    return lax.dot_general(a, b, dims, preferred_element_type=F32)


def _dot1(a, b, dims=NN):
    return _dot(a.astype(BF16), b.astype(BF16), dims)


def _dot3(a, b, dims=NN):
    ah, al = _split2(a)
    bh, bl = _split2(b)
    return _dot(ah, bh, dims) + (_dot(ah, bl, dims) + _dot(al, bh, dims))


def _dot3c(ch, cl, x, dims=NN):
    xh, xl = _split2(x)
    return _dot(ch, xh, dims) + (_dot(cl, xh, dims) + _dot(ch, xl, dims))


def _dot3r(x, ch, cl, dims=NN):
    xh, xl = _split2(x)
    return _dot(xh, ch, dims) + (_dot(xl, ch, dims) + _dot(xh, cl, dims))


def _headsums(xs, ones_pair):
    rows = xs[0].shape[0]
    tiles = xs[0].shape[1] // LANES
    parts = []
    for x in xs:
        for part in _split2(x):
            parts.extend(part[:, t * LANES:(t + 1) * LANES] for t in range(tiles))
    s = _dot(jnp.concatenate(parts, axis=0), ones_pair)
    outs = []
    for j in range(len(xs)):
        base = j * 2 * tiles
        outs.append(jnp.concatenate(
            [s[(base + t) * rows:(base + t + 1) * rows] + s[(base + tiles + t) * rows:(base + tiles + t + 1) * rows]
             for t in range(tiles)], axis=1))
    return outs


def _rms(x, eps=RMS_EPS):
    return x * lax.rsqrt(jnp.mean(x * x, axis=-1, keepdims=True) + eps)


def _sigmoid(x):
    return 1.0 / (1.0 + jnp.exp(-x))


def _softplus(x):
    return jnp.maximum(x, 0.0) + jnp.log(1.0 + jnp.exp(-jnp.abs(x)))


def _token_specs(xs, bpb):
    if len(xs) == 1:
        return [pl.BlockSpec((ROW_BLOCK, D_MODEL), lambda i: (i, 0))]
    return [pl.BlockSpec((ROW_BLOCK, D_MODEL), lambda i: (_latent_block(i, bpb), 0)),
            pl.BlockSpec((ROW_BLOCK, D_MODEL), lambda i: (i // bpb, 0))]


def _token_rows(x_refs, is_ctx):
    if len(x_refs) == 1:
        return x_refs[0][...]
    return jnp.where(is_ctx, x_refs[1][...], x_refs[0][...])


def _inproj_body(n_src, blocks_per_batch, *refs):
    x_refs = refs[:n_src]
    g_ref, sc_ref, sh_ref, w_ref, z_ref, cv_ref, fx_ref, fc_ref = refs[n_src:]
    is_ctx = pl.program_id(0) % blocks_per_batch == 0
    h = _rms(_token_rows(x_refs, is_ctx)) * g_ref[...] * (1.0 + sc_ref[0]) + sh_ref[0]
    p = _dot(h.astype(BF16), w_ref[...])
    z_ref[...] = p[:, :D_Z]
    cv_ref[...] = p[:, D_Z:D_Z + 3 * D_CONV]

    fo = p[:, D_Z + 3 * D_CONV:]
    fx_ref[...] = fo

    @pl.when(is_ctx)
    def _():
        fc_ref[...] = fo


def _latent_block(i, bpb):
    return (i // bpb) * (bpb - 1) + jnp.maximum(i % bpb - 1, 0)


def _inproj(xs, gain, sc, sh, w_bf, ent, bsz, bpb):
    n = bsz * bpb * ROW_BLOCK
    row = lambda w: pl.BlockSpec((ROW_BLOCK, w), lambda i: (i, 0))
    full = lambda a: pl.BlockSpec(a.shape, lambda i: (0,) * a.ndim)
    mod = pl.BlockSpec((1, 1, D_MODEL), lambda i: (ent(i), 0, 0))
    return pl.pallas_call(
        functools.partial(_inproj_body, len(xs), bpb),
        grid=(n // ROW_BLOCK,),
        in_specs=_token_specs(xs, bpb) + [full(gain), mod, mod, full(w_bf)],
        out_specs=[row(D_Z), row(3 * D_CONV),
                   pl.BlockSpec((ROW_BLOCK, D_FOUR), lambda i: (_latent_block(i, bpb), 0)),
                   pl.BlockSpec((ROW_BLOCK, D_FOUR), lambda i: (i // bpb, 0))],
        out_shape=[jax.ShapeDtypeStruct((n, D_Z), F32), jax.ShapeDtypeStruct((n, 3 * D_CONV), F32),
                   jax.ShapeDtypeStruct((n - bsz * ROW_BLOCK, D_FOUR), F32),
                   jax.ShapeDtypeStruct((bsz * ROW_BLOCK, D_FOUR), F32)],
        compiler_params=_params("arbitrary"),
        name="inproj",
    )(*xs, gain, sc, sh, w_bf)


def _streams_steps(res, i, blocks_per_batch, zm_ref, zp_ref, zn_ref, mu_ref, w2_ref, a2_ref, g2_ref,
                   w0_ref, a0_ref, kkw_ref, ka_ref, rk_ref, ones_ref):
    seq_pos = i % blocks_per_batch
    ctx_i = (seq_pos == 0).astype(jnp.int32)
    tb = zm_ref.shape[0]
    t = lax.broadcasted_iota(jnp.int32, (tb, 1), 0)
    ctx_v = jnp.zeros((tb, 1), jnp.int32) + ctx_i
    col = t & (GRID_W - 1)
    lmask = (col != 0) | ((ctx_v != 0) & (t != 0))
    rmask = (col != GRID_W - 1) | ((ctx_v != 0) & (t != tb - 1))
    top_v = jnp.zeros((tb, 1), jnp.int32) + (seq_pos == 1).astype(jnp.int32)
    bot_v = jnp.zeros((tb, 1), jnp.int32) + (seq_pos == blocks_per_batch - 1).astype(jnp.int32)
    umask = jnp.logical_not((top_v != 0) & (t < GRID_W))
    dmask = jnp.logical_not((bot_v != 0) & (t >= tb - GRID_W))
    lane_mask = jnp.where(ctx_i != 0, 1, 3)
    tiles = []
    for c0 in range(0, D_Z, LANES):
        cols = slice(c0, c0 + LANES)
        z = zm_ref[:, cols]
        left = jnp.where(lmask, pltpu.roll(z, 1, axis=0), 0.0)
        right = jnp.where(rmask, pltpu.roll(z, tb - 1, axis=0), 0.0)
        up = jnp.where(umask, jnp.concatenate([zp_ref[:, cols], z[:tb - GRID_W]], axis=0), 0.0)
        down = jnp.where(dmask, jnp.concatenate([z[GRID_W:], zn_ref[:, cols]], axis=0), 0.0)
        q = (lax.broadcasted_iota(jnp.int32, (1, LANES), 1) + c0) & lane_mask
        shifted = jnp.where(q == 0, left, jnp.where(q == 1, right, jnp.where(q == 2, up, down)))
        tiles.append(z + (shifted - z) * mu_ref[:, cols])
        yield
    per = D_RWKV // LANES
    r = jnp.concatenate(tiles[0:per], axis=1)
    k = jnp.concatenate(tiles[per:2 * per], axis=1)
    v = jnp.concatenate(tiles[2 * per:3 * per], axis=1)
    lw_in, la_in, lg = tiles[3 * per], tiles[3 * per + 1], tiles[3 * per + 2]

    g = _dot3(_sigmoid(lg), g2_ref[...])
    yield
    w_log = -_softplus(-(w0_ref[...] + _dot3(jnp.tanh(lw_in), w2_ref[...]))) - 0.5
    lw = -jnp.exp(w_log)
    yield
    a = _sigmoid(a0_ref[...] + _dot3(la_in, a2_ref[...]))
    yield
    ka = ka_ref[...]
    a_d = [a[:, d * D_RWKV:(d + 1) * D_RWKV] for d in range(2)]
    k_d = [k * (1.0 + (x - 1.0) * ka) for x in a_d]
    lw_d = [lw[:, d * D_RWKV:(d + 1) * D_RWKV] for d in range(2)]
    kq = k * kkw_ref[...]
    yield
    kq_ss, rk_sum = _headsums([kq * kq, r * (k_d[0] + k_d[1]) * rk_ref[...]], ones_ref[...])
    yield
    kk = kq / jnp.maximum(jnp.sqrt(kq_ss), 1e-12)
    b_d = [kk * x for x in a_d]
    res.update(r=r, v=v, kk=kk, g=g, bonus=rk_sum * v, lw_d=lw_d, k_d=k_d, b_d=b_d)


def _chunk_maps(r, v, kk, lw_d, k_d, b_d, p_out, q_out, rp_out, y0_out):
    cs = CHUNK
    rows = r.shape[0]
    n_chunks = rows // cs
    n_pairs = D_RWKV // LANES

    ri = lax.broadcasted_iota(jnp.int32, (rows, rows), 0)
    rj = lax.broadcasted_iota(jnp.int32, (rows, rows), 1)
    same_chunk = (ri >> 6) == (rj >> 6)
    ti = lax.broadcasted_iota(jnp.int32, (cs, LANES), 0)
    lane = lax.broadcasted_iota(jnp.int32, (cs, LANES), 1)
    tj = lane & (HEAD - 1)
    lo_half = lane < HEAD
    eye = ti == tj
    eye_f = eye.astype(F32)
    zero_bf = jnp.zeros((cs, LANES), BF16)
    strict, incl = [], []
    at, bt, kt, rt, bh, kh, e_tot = [], [], [], [], [], [], []
    for d in range(2):
        sgn = 1 if d == 0 else -1
        order = (ti - tj) * sgn
        strict.append(order > 0)
        incl.append(order >= 0)
        t_mat = jnp.concatenate([(same_chunk & ((ri - rj) * sgn >= 0)).astype(F32).astype(BF16),
                                 same_chunk.astype(F32).astype(BF16)], axis=0)
        lw = lw_d[d]
        l1 = lw.astype(BF16)
        rem = lw - l1.astype(F32)
        l2 = rem.astype(BF16)
        l3 = (rem - l2.astype(F32)).astype(BF16)
        gsum = _dot(t_mat, l1) + (_dot(t_mat, l2) + _dot(t_mat, l3))
        gcum, gtot = gsum[:rows], gsum[rows:]
        e_neg = jnp.exp(-gcum)
        e_rem = jnp.exp(gtot - gcum)
        at.append(-(kk * jnp.exp(gcum - lw)))
        bt.append(b_d[d] * e_neg)
        kt.append(k_d[d] * e_neg)
        rt.append(r * jnp.exp(gcum))
        bh.append(b_d[d] * e_rem)
        kh.append(k_d[d] * e_rem)
        e_tot.append(jnp.exp(gtot))
        yield

    def bd(y):
        y = y.astype(BF16)
        return jnp.concatenate([jnp.where(lo_half, y, zero_bf), jnp.where(lo_half, zero_bf, y)], axis=0)

    def mm(x, y_bd, dims=NN):
        return _dot(x.astype(BF16), y_bd, dims)

    tile = lambda a, c, p: a[c * cs:(c + 1) * cs, p * LANES:(p + 1) * LANES]
    for d in range(2):
        yield from _chunk_stages(d, [(d, c, p) for c in range(n_chunks) for p in range(n_pairs)], tile, bd, mm,
                                 strict, incl, eye, eye_f, lo_half, ti, tj, at, bt, kt, rt, bh, kh, e_tot, v,
                                 p_out, q_out, rp_out, y0_out)


def _chunk_stages(d, chains, tile, bd, mm, strict, incl, eye, eye_f, lo_half, ti, tj, at, bt, kt, rt, bh, kh,
                  e_tot, v, p_out, q_out, rp_out, y0_out):
    cs = CHUNK
    n_pairs = D_RWKV // LANES
    n_chunks = len(chains) // n_pairs
    s = [mm(jnp.concatenate([tile(at[d], c, p), tile(rt[d], c, p)], axis=0),
            jnp.concatenate([bd(tile(bt[d], c, p)), bd(tile(kt[d], c, p))], axis=0), NT) for d, c, p in chains]
    yield
    a_ab = [jnp.where(strict[d], x[:cs, :LANES], 0.0) for x, (d, _, _) in zip(s, chains)]
    a_ak = [jnp.where(strict[d], x[:cs, LANES:], 0.0) for x, (d, _, _) in zip(s, chains)]
    a_rb = [jnp.where(incl[d], x[cs:, :LANES], 0.0) for x, (d, _, _) in zip(s, chains)]
    a_rk = [jnp.where(incl[d], x[cs:, LANES:], 0.0) for x, (d, _, _) in zip(s, chains)]
    a0 = [jnp.where((ti >> 3) == (tj >> 3), x, 0.0) for x in a_ab]
    a2 = [mm(x, bd(x)) for x in a0]
    yield
    a2_bd = [bd(x) for x in a2]
    x0 = [eye_f + x for x in a0]
    x1 = [x + mm(x, y) for x, y in zip(x0, a2_bd)]
    yield
    a4 = [mm(x, y) for x, y in zip(a2, a2_bd)]
    yield
    minv = [x + mm(x, bd(y)) for x, y in zip(x1, a4)]
    yield
    lvl = 3
    while (1 << lvl) < cs:
        off = ((ti >> (lvl + 1)) == (tj >> (lvl + 1))) & ((ti >> lvl) != (tj >> lvl))
        t = [mm(jnp.where(off, x, 0.0), bd(y)) for x, y in zip(a_ab, minv)]
        yield
        minv = [x + mm(x, bd(y)) for x, y in zip(minv, t)]
        yield
        lvl += 1
    v_bd = [bd(tile(v, c, p)) for _, c, p in chains]
    avk = [mm(jnp.concatenate([x, z], axis=0), y) for x, z, y in zip(a_ak, a_rk, v_bd)]
    yield
    av = [x[:cs] for x in avk]
    bot_k = [x[cs:] for x in avk]
    wu = [mm(m, jnp.concatenate([bd(tile(at[d], c, p)), bd(y)], axis=1))
          for m, y, (d, c, p) in zip(minv, av, chains)]
    yield
    top_b = [mm(tile(bh[d], c, p), x.astype(BF16), TN) for x, (d, c, p) in zip(wu, chains)]
    yield
    top_k = [mm(tile(kh[d], c, p), tile(v, c, p).astype(BF16), TN) for d, c, p in chains]
    yield
    bot_b = [mm(x, jnp.concatenate([bd(y[:, :LANES]), bd(y[:, LANES:])], axis=1)) for x, y in zip(a_rb, wu)]
    yield
    sel = lambda x, off: jnp.where(lo_half, x[:cs, off:off + LANES], x[cs:, off:off + LANES])
    for c in range(n_chunks):
        idx = [c * n_pairs + p for p in range(n_pairs)]
        p_out[d, c] = jnp.concatenate(
            [sel(top_b[i], 0) + jnp.where(eye, tile(e_tot[d], c, p), 0.0) for p, i in enumerate(idx)], axis=1)
        q_out[d, c] = jnp.concatenate([sel(top_b[i], LANES) + sel(top_k[i], 0) for i in idx], axis=1)
        rp_out[d, c * cs:(c + 1) * cs, :] = jnp.concatenate(
            [tile(rt[d], c, p) + bot_b[i][:, :LANES] for p, i in enumerate(idx)], axis=1)
        y0_out[d, c * cs:(c + 1) * cs, :] = jnp.concatenate(
            [bot_b[i][:, LANES:] + bot_k[i] for i in idx], axis=1)


def _rwkv_chunks_body(blocks_per_batch, zm_ref, zp_ref, zn_ref, mu_ref, w2_ref, a2_ref, g2_ref, w0_ref,
                      a0_ref, kkw_ref, ka_ref, rk_ref, ones_ref, g_out, bon_out, p_out, q_out, rp_out, y0_out,
                      ):
    res = {}
    for _ in _streams_steps(res, pl.program_id(0), blocks_per_batch, zm_ref, zp_ref, zn_ref, mu_ref, w2_ref, a2_ref,
                            g2_ref, w0_ref, a0_ref, kkw_ref, ka_ref, rk_ref, ones_ref):
        pass
    g_out[...] = res["g"]
    bon_out[...] = res["bonus"]
    for _ in _chunk_maps(res["r"], res["v"], res["kk"], res["lw_d"], res["k_d"], res["b_d"],
                         p_out, q_out, rp_out, y0_out):
        pass


def _rwkv_chunks(pz, seq, ctx_len, bsz, mu, w2bd, a2bd, g2, w0, a0, kkw, ka, rk, ones_bd):
    n = pz.shape[0]
    assert ctx_len == ROW_BLOCK and seq % ROW_BLOCK == 0 and n == bsz * (ctx_len + seq)
    sub = ROW_BLOCK // GRID_W
    last = n // GRID_W - 1
    cps = ROW_BLOCK // CHUNK
    nblk = n // ROW_BLOCK
    full = lambda a: pl.BlockSpec(a.shape, lambda i: (0,) * a.ndim)
    row = pl.BlockSpec((ROW_BLOCK, D_RWKV), lambda i: (i, 0))
    row2 = pl.BlockSpec((2, ROW_BLOCK, D_RWKV), lambda i: (0, i, 0))
    mat = pl.BlockSpec((2, cps, HEAD, D_RWKV), lambda i: (0, i, 0, 0))
    s1 = jax.ShapeDtypeStruct((n, D_RWKV), F32)
    s2 = jax.ShapeDtypeStruct((2, n, D_RWKV), F32)
    sm = jax.ShapeDtypeStruct((2, n // CHUNK, HEAD, D_RWKV), F32)
    consts = (mu, w2bd, a2bd, g2, w0, a0, kkw, ka, rk, ones_bd)
    return pl.pallas_call(
        functools.partial(_rwkv_chunks_body, (ctx_len + seq) // ROW_BLOCK),
        grid=(nblk,),
        in_specs=[pl.BlockSpec((ROW_BLOCK, D_Z), lambda i: (i, 0)),
                  pl.BlockSpec((GRID_W, D_Z), lambda i: (jnp.maximum(i * sub - 1, 0), 0)),
                  pl.BlockSpec((GRID_W, D_Z), lambda i: (jnp.minimum((i + 1) * sub, last), 0))]
                 + [full(a) for a in consts],
        out_specs=[row, row, mat, mat, row2, row2],
        out_shape=[s1, s1, sm, sm, s2, s2],
        compiler_params=_params("parallel"),
        name="rwkv_chunks",
    )(pz, pz, pz, *consts)


SCAN_CHUNKS = 4


def _scan_body(pf_ref, qf_ref, rpf_ref, y0f_ref, pb_ref, qb_ref, rpb_ref, y0b_ref, yf_out, yb_out, h_scr):
    @pl.when(pl.program_id(0) == 0)
    def _():
        h_scr[...] = jnp.zeros_like(h_scr)

    cs = CHUNK
    bsz = h_scr.shape[1]
    n_pairs = D_RWKV // LANES
    lane = lax.broadcasted_iota(jnp.int32, (cs, LANES), 1)
    lo_half = lane < HEAD
    zero_bf = jnp.zeros((cs, LANES), BF16)

    def bd(y):
        return jnp.concatenate([jnp.where(lo_half, y, zero_bf), jnp.where(lo_half, zero_bf, y)], axis=0)

    dirs = ((pf_ref, qf_ref, rpf_ref, y0f_ref, yf_out), (pb_ref, qb_ref, rpb_ref, y0b_ref, yb_out))
    chains = [(d, b, p) for d in range(2) for b in range(bsz) for p in range(n_pairs)]
    state = [h_scr[d, b, :, p * LANES:(p + 1) * LANES] for d, b, p in chains]
    for s in range(SCAN_CHUNKS):
        outs = []
        for (d, b, p), hcur in zip(chains, state):
            p_ref, _, rp_ref, _, _ = dirs[d]
            c = s if d == 0 else SCAN_CHUNKS - 1 - s
            ls = slice(p * LANES, (p + 1) * LANES)
            x = jnp.concatenate([p_ref[0, b, c, :, ls], rp_ref[0, b, c * cs:(c + 1) * cs, ls]], axis=0)
            xh, xl = _split2(x)
            hh, hl = _split2(hcur)
            hh, hl = bd(hh), bd(hl)
            outs.append(_dot(xh, hh) + (_dot(xh, hl) + _dot(xl, hh)))
        new_state = []
        for (d, b, p), o in zip(chains, outs):
            _, q_ref, _, y0_ref, y_out = dirs[d]
            c = s if d == 0 else SCAN_CHUNKS - 1 - s
            ls = slice(p * LANES, (p + 1) * LANES)
            new_state.append(o[:cs] + q_ref[0, b, c, :, ls])
            y_out[b, c * cs:(c + 1) * cs, ls] = y0_ref[0, b, c * cs:(c + 1) * cs, ls] + o[cs:]
        state = new_state
    for (d, b, p), hcur in zip(chains, state):
        h_scr[d, b, :, p * LANES:(p + 1) * LANES] = hcur


def _chunk_scan(p, q, rp, y0, seq, ctx_len, bsz):
    n = rp.shape[1]
    rows_b = ctx_len + seq
    ncb = rows_b // CHUNK
    assert ctx_len % (SCAN_CHUNKS * CHUNK) == 0 and seq % (SCAN_CHUNKS * CHUNK) == 0
    steps = ncb // SCAN_CHUNKS
    ctx_steps = ctx_len // (SCAN_CHUNKS * CHUNK)
    p5 = p.reshape(2, bsz, ncb, HEAD, D_RWKV)
    q5 = q.reshape(2, bsz, ncb, HEAD, D_RWKV)
    rp4 = rp.reshape(2, bsz, rows_b, D_RWKV)
    y04 = y0.reshape(2, bsz, rows_b, D_RWKV)
    pos_f = lambda i: i
    pos_b = lambda i: jnp.where(i < ctx_steps, ctx_steps - 1 - i, steps - 1 - (i - ctx_steps))
    rows = SCAN_CHUNKS * CHUNK

    def specs(d, pos):
        mat = pl.BlockSpec((1, bsz, SCAN_CHUNKS, HEAD, D_RWKV), lambda i: (d, 0, pos(i), 0, 0))
        tok = pl.BlockSpec((1, bsz, rows, D_RWKV), lambda i: (d, 0, pos(i), 0))
        return [mat, mat, tok, tok]

    out_f = pl.BlockSpec((bsz, rows, D_RWKV), lambda i: (0, pos_f(i), 0))
    out_b = pl.BlockSpec((bsz, rows, D_RWKV), lambda i: (0, pos_b(i), 0))
    shp = jax.ShapeDtypeStruct((bsz, rows_b, D_RWKV), F32)
    yf, yb = pl.pallas_call(
        _scan_body,
        grid=(steps,),
        in_specs=specs(0, pos_f) + specs(1, pos_b),
        out_specs=[out_f, out_b],
        out_shape=[shp, shp],
        scratch_shapes=[pltpu.VMEM((2, bsz, HEAD, D_RWKV), F32)],
        compiler_params=_params("arbitrary"),
        name="chunk_scan",
    )(p5, q5, rp4, y04, p5, q5, rp4, y04)
    return yf.reshape(n, D_RWKV), yb.reshape(n, D_RWKV)


def _hilo(a):
    if isinstance(a, np.ndarray):
        a = a.astype(np.float32)
        hi = a.astype(BF16)
        return jnp.asarray(hi), jnp.asarray((a - hi.astype(np.float32)).astype(BF16))
    hi = a.astype(BF16)
    return hi, (a - hi.astype(F32)).astype(BF16)


def _channel_tables(length):
    j = np.arange(FOUR_GROUP)
    ang = 2.0 * np.pi * np.outer(j, j) / FOUR_GROUP
    scale = 1.0 / np.sqrt(float(length) * FOUR_GROUP)
    groups = D_FOUR // FOUR_GROUP
    c4 = np.kron(np.eye(groups), np.cos(ang)) * scale
    s4 = np.kron(np.eye(groups), np.sin(ang)) * scale
    return _hilo(c4) + _hilo(s4)


def _four_finish(fr, fi, c4h, c4l, s4h, s4l, gain):
    y = _dot3r(fr, c4h, c4l) + _dot3r(fi, s4h, s4l)
    return _rms(y) * gain


def _four_direct_body(f_ref, mh_ref, ml_ref, c4h, c4l, s4h, s4l, gain_ref, o_ref):
    length = f_ref.shape[1]
    fc = _dot3c(mh_ref[...], ml_ref[...], f_ref[0])
    o_ref[0] = _four_finish(fc[:length], fc[length:], c4h[...], c4l[...], s4h[...], s4l[...], gain_ref[...])


def _fourier_direct(f, gain):
    bsz, length, _ = f.shape
    t = np.arange(length)
    ang = 2.0 * np.pi * (np.outer(t, t) % length) / length
    mh, ml = _hilo(np.concatenate([np.cos(ang), -np.sin(ang)], axis=0))
    consts = (mh, ml) + _channel_tables(length) + (gain,)
    full = lambda a: pl.BlockSpec(a.shape, lambda b: (0,) * a.ndim)
    blk = pl.BlockSpec((1, length, D_FOUR), lambda b: (b, 0, 0))
    return pl.pallas_call(
        _four_direct_body,
        grid=(bsz,),
        in_specs=[blk] + [full(a) for a in consts],
        out_specs=blk,
        out_shape=jax.ShapeDtypeStruct(f.shape, F32),
        compiler_params=_params("parallel"),
        name="fourier_direct",
    )(f, *consts)


def _four_stage1_body(f_ref, mh_ref, ml_ref, o_ref):
    o_ref[0] = _dot3c(mh_ref[...], ml_ref[...], f_ref[0])


def _four_stage2_body(zr_ref, zi_ref, mh_ref, ml_ref, c4h, c4l, s4h, s4l, gain_ref, o_ref):
    group, inner = zr_ref.shape[1], zr_ref.shape[2]
    fc = [_dot3c(mh_ref[j], ml_ref[j], jnp.concatenate([zr_ref[0, j], zi_ref[0, j]], axis=0))
          for j in range(group)]
    o_ref[0] = jnp.concatenate(
        [_four_finish(x[:inner], x[inner:], c4h[...], c4l[...], s4h[...], s4l[...], gain_ref[...]) for x in fc],
        axis=1)


def _fourier_long(f, gain):
    bsz, length, _ = f.shape
    l2 = FOUR_INNER
    l1 = length // l2
    assert l1 * l2 == length and l1 % 16 == 0
    cols = l2 * D_FOUR
    col_tile = 4096
    th = np.arange(l1)
    ang1 = 2.0 * np.pi * (np.outer(th, th) % l1) / l1
    m1h, m1l = _hilo(np.concatenate([np.cos(ang1), -np.sin(ang1)], axis=0))
    z = pl.pallas_call(
        _four_stage1_body,
        grid=(bsz, cols // col_tile),
        in_specs=[pl.BlockSpec((1, l1, col_tile), lambda b, c: (b, 0, c)),
                  pl.BlockSpec(m1h.shape, lambda b, c: (0, 0)),
                  pl.BlockSpec(m1l.shape, lambda b, c: (0, 0))],
        out_specs=pl.BlockSpec((1, 2 * l1, col_tile), lambda b, c: (b, 0, c)),
        out_shape=jax.ShapeDtypeStruct((bsz, 2 * l1, cols), F32),
        compiler_params=_params("parallel", "parallel"),
        name="fourier_stage1",
    )(f.reshape(bsz, l1, cols), m1h, m1l)
    z = z.reshape(bsz, 2 * l1, l2, D_FOUR)

    ma = np.arange(l1)[:, None, None]
    mb = np.arange(l2)[None, :, None]
    tl = np.arange(l2)[None, None, :]
    ang2 = 2.0 * np.pi * (((ma + l1 * mb) * tl) % length) / length
    cos2, sin2 = np.cos(ang2), np.sin(ang2)
    m2 = np.concatenate([np.concatenate([cos2, sin2], axis=2), np.concatenate([-sin2, cos2], axis=2)], axis=1)
    m2h, m2l = _hilo(m2)
    consts = _channel_tables(length) + (gain,)
    full = lambda a: pl.BlockSpec(a.shape, lambda b, m: (0,) * a.ndim)
    grp = FOUR_GROUP_STEP
    tab = pl.BlockSpec((grp, 2 * l2, 2 * l2), lambda b, m: (m, 0, 0))
    out = pl.pallas_call(
        _four_stage2_body,
        grid=(bsz, l1 // grp),
        in_specs=[pl.BlockSpec((1, grp, l2, D_FOUR), lambda b, m: (b, m, 0, 0)),
                  pl.BlockSpec((1, grp, l2, D_FOUR), lambda b, m: (b, l1 // grp + m, 0, 0)),
                  tab, tab] + [full(a) for a in consts],
        out_specs=pl.BlockSpec((1, l2, grp * D_FOUR), lambda b, m: (b, 0, m)),
        out_shape=jax.ShapeDtypeStruct((bsz, l2, l1 * D_FOUR), F32),
        compiler_params=_params("parallel", "parallel"),
        name="fourier_stage2",
    )(z, z, m2h, m2l, *consts)
    return out.reshape(bsz, length, D_FOUR)


def _route(logits):
    lane = lax.broadcasted_iota(jnp.int32, (1, LANES), 1)
    lane_f = lane.astype(F32)
    neg = jnp.float32(-1e30)
    big = jnp.float32(1e9)
    gl = jnp.where(lane < N_GROUPS, logits, neg)
    gmax = jnp.max(gl, axis=-1, keepdims=True)
    pg_top = 1.0 / jnp.sum(jnp.exp(gl - gmax), axis=-1, keepdims=True)
    grp = jnp.min(jnp.where(gl == gmax, lane_f, big), axis=-1, keepdims=True)
    e_lane = lane - N_GROUPS
    in_grp = (e_lane >= 0) & (e_lane < N_EXPERTS) & ((e_lane >> 3).astype(F32) == grp)
    el = jnp.where(in_grp, logits, neg)
    m1 = jnp.max(el, axis=-1, keepdims=True)
    i1 = jnp.min(jnp.where(el == m1, lane_f, big), axis=-1, keepdims=True)
    el2 = jnp.where(lane_f == i1, neg, el)
    m2 = jnp.max(el2, axis=-1, keepdims=True)
    i2 = jnp.min(jnp.where(el2 == m2, lane_f, big), axis=-1, keepdims=True)
    e2 = jnp.exp(m2 - m1)
    den = 1.0 + e2
    return pg_top / den, pg_top * e2 / den, i1 - N_GROUPS, i2 - N_GROUPS


def _post_body(n_src, blocks_per_batch, *refs):
    x_refs = refs[:n_src]
    (yf_ref, yb_ref, bon_ref, g_ref, cv_ref, cvp_ref, cvn_ref, fx_ref, fc_ref, wo_ref, gnw_ref, gnb_ref, cw_ref,
     cg_ref, g1_ref, sc2_ref, sh2_ref, n2_ref, wrh_ref, wrl_ref, rb_ref, ones_ref, x_out, h_out, route_out,
     route_t_out, count_out, count_scr) = refs[n_src:]
    i = pl.program_id(0)

    @pl.when(i == 0)
    def _():
        count_scr[...] = jnp.zeros_like(count_scr)

    ones_pair = ones_ref[...]
    y = yf_ref[...] + yb_ref[...]
    mu = _headsums([y], ones_pair)[0] * (1.0 / HEAD)
    yc = y - mu
    var = _headsums([yc * yc], ones_pair)[0] * (1.0 / HEAD)
    yn = yc * lax.rsqrt(var + GN_EPS) * gnw_ref[...] + gnb_ref[...]
    o_rwkv = (yn + bon_ref[...]) * g_ref[...]

    cv = cv_ref[...]
    tb = cv.shape[0]
    t = lax.broadcasted_iota(jnp.int32, (tb, 1), 0)
    seq_pos = i % blocks_per_batch
    first = (seq_pos <= 1).astype(F32)
    final = jnp.logical_or(seq_pos == 0, seq_pos == blocks_per_batch - 1).astype(F32)
    zc = cv[:, D_CONV:2 * D_CONV] * cv[:, 2 * D_CONV:]
    zp_row = cvp_ref[7:8, D_CONV:2 * D_CONV] * cvp_ref[7:8, 2 * D_CONV:] * (1.0 - first)
    zn_row = cvn_ref[0:1, D_CONV:2 * D_CONV] * cvn_ref[0:1, 2 * D_CONV:] * (1.0 - final)
    prev = jnp.where(t == 0, zp_row, pltpu.roll(zc, 1, axis=0))
    nxt = jnp.where(t == tb - 1, zn_row, pltpu.roll(zc, tb - 1, axis=0))
    cw = cw_ref[...]
    conv = cv[:, :D_CONV] * (cw[0:1] * prev + cw[1:2] * zc + cw[2:3] * nxt)
    conv_o = _rms(conv) * cg_ref[...]

    four_o = jnp.where(seq_pos == 0, fc_ref[...], fx_ref[...])
    mix = (_dot(o_rwkv.astype(BF16), wo_ref[0:D_RWKV, :])
           + _dot(conv_o.astype(BF16), wo_ref[D_RWKV:D_RWKV + D_CONV, :])
           + _dot(four_o.astype(BF16), wo_ref[D_RWKV + D_CONV:, :]))
    x = _token_rows(x_refs, seq_pos == 0) + g1_ref[0] * mix
    x_out[...] = x
    h2 = _rms(x) * n2_ref[...] * (1.0 + sc2_ref[0]) + sh2_ref[0]
    h_out[...] = h2

    gate0, gate1, e0, e1 = _route(_dot3r(h2, wrh_ref[...], wrl_ref[...]) + rb_ref[...])
    lane = lax.broadcasted_iota(jnp.int32, (1, LANES), 1)
    lane_f = lane.astype(F32)
    oh0 = (lane_f == e0).astype(F32)
    oh1 = (lane_f == e1).astype(F32)
    ri = lax.broadcasted_iota(jnp.int32, (tb, tb), 0)
    rj = lax.broadcasted_iota(jnp.int32, (tb, tb), 1)
    earlier = (rj < ri).astype(F32).astype(BF16)
    seen = count_scr[...]
    tot0 = jnp.sum(oh0, axis=0, keepdims=True)
    before = _dot(earlier, jnp.concatenate([oh0, oh1], axis=1).astype(BF16))
    before0 = before[:, :LANES] + seen
    before1 = before[:, LANES:] + (seen + tot0)
    rank0 = jnp.sum(oh0 * before0, axis=-1, keepdims=True)
    rank1 = jnp.sum(oh1 * before1, axis=-1, keepdims=True)
    seen = seen + tot0 + jnp.sum(oh1, axis=0, keepdims=True)
    count_scr[...] = seen
    count_out[...] = seen
    route = jnp.where(
        lane == 0, gate0, jnp.where(lane == 1, gate1, jnp.where(lane == 2, e0, jnp.where(
            lane == 3, e1, jnp.where(lane == 4, rank0, jnp.where(lane == 5, rank1, 0.0))))))
    route_out[...] = route
    route_t_out[...] = jnp.transpose(route)[:8]


def _post(yf, yb, bonus, g, pcv, four_x, four_c, xs, seq, ctx_len, ent, wo_bf, gnw, gnb, cw, cg, g1, sc2, sh2, n2,
          wrh, wrl, rb, ones_bd):
    n = yf.shape[0]
    assert ctx_len == ROW_BLOCK
    sub = ROW_BLOCK // 8
    last = n // 8 - 1
    bpb = (ctx_len + seq) // ROW_BLOCK
    full = lambda a: pl.BlockSpec(a.shape, lambda i: (0,) * a.ndim)
    row = lambda w: pl.BlockSpec((ROW_BLOCK, w), lambda i: (i, 0))
    mod = pl.BlockSpec((1, 1, D_MODEL), lambda i: (ent(i), 0, 0))
    return pl.pallas_call(
        functools.partial(_post_body, len(xs), bpb),
        grid=(n // ROW_BLOCK,),
        in_specs=_token_specs(xs, bpb) + [
                  row(D_RWKV), row(D_RWKV), row(D_RWKV), row(D_RWKV), row(3 * D_CONV),
                  pl.BlockSpec((8, 3 * D_CONV), lambda i: (jnp.maximum(i * sub - 1, 0), 0)),
                  pl.BlockSpec((8, 3 * D_CONV), lambda i: (jnp.minimum((i + 1) * sub, last), 0)),
                  pl.BlockSpec((ROW_BLOCK, D_FOUR), lambda i: (_latent_block(i, bpb), 0)),
                  pl.BlockSpec((ROW_BLOCK, D_FOUR), lambda i: (i // bpb, 0)),
                  full(wo_bf), full(gnw), full(gnb), full(cw), full(cg),
                  mod, mod, mod, full(n2), full(wrh), full(wrl), full(rb), full(ones_bd)],
        out_specs=[row(D_MODEL), row(D_MODEL), row(LANES), pl.BlockSpec((8, ROW_BLOCK), lambda i: (0, i)),
                   pl.BlockSpec((1, LANES), lambda i: (0, 0))],
        out_shape=[jax.ShapeDtypeStruct((n, D_MODEL), F32), jax.ShapeDtypeStruct((n, D_MODEL), F32),
                   jax.ShapeDtypeStruct((n, LANES), F32), jax.ShapeDtypeStruct((8, n), F32),
                   jax.ShapeDtypeStruct((1, LANES), F32)],
        scratch_shapes=[pltpu.VMEM((1, LANES), F32)],
        compiler_params=_params("arbitrary"),
        name="post_mix",
    )(*xs, yf, yb, bonus, g, pcv, pcv, pcv, four_x, four_c, wo_bf, gnw, gnb, cw, cg, g1, sc2, sh2, n2, wrh, wrl,
      rb, ones_bd)


def _scatter_rows_body(zb_ref, zv_ref, dest_ref, h_ref, xb_ref, zero_buf, sem):
    tb = h_ref.shape[0]

    @pl.when(pl.program_id(0) == 0)
    def _():
        zero_buf[...] = jnp.zeros_like(zero_buf)

        def block_fill(j):
            start = pl.multiple_of(zb_ref[j] * MOE_ROWS, MOE_ROWS)
            return pltpu.make_async_copy(zero_buf, xb_ref.at[pl.ds(start, MOE_ROWS), :], sem)

        def fill_start(j, carry):
            @pl.when(zv_ref[j] != 0)
            def _():
                block_fill(j).start()
            return carry

        def fill_wait(j, carry):
            @pl.when(zv_ref[j] != 0)
            def _():
                block_fill(j).wait()
            return carry

        lax.fori_loop(0, zb_ref.shape[0], fill_start, 0)
        lax.fori_loop(0, zb_ref.shape[0], fill_wait, 0)

    def row_copy(t, k):
        return pltpu.make_async_copy(h_ref.at[pl.ds(t, 1), :],
                                     xb_ref.at[pl.ds(dest_ref[0, 0, k * tb + t], 1), :], sem)

    def issue(t, carry):
        row_copy(t, 0).start()
        row_copy(t, 1).start()
        return carry

    def drain(t, carry):
        row_copy(t, 0).wait()
        row_copy(t, 1).wait()
        return carry

    for t in range(tb):
        issue(t, 0)
    lax.fori_loop(0, tb, drain, 0, unroll=DMA_UNROLL)


DMA_UNROLL = 8
SCATTER_ROWS = 512


def _scatter_rows(h2, dest, zero_blocks, zero_valid, cap):
    n = h2.shape[0]
    steps = n // SCATTER_ROWS
    grid_spec = pltpu.PrefetchScalarGridSpec(
        num_scalar_prefetch=2,
        grid=(steps,),
        in_specs=[pl.BlockSpec((1, 1, 2 * SCATTER_ROWS), lambda i, zb, zv: (i, 0, 0), memory_space=pltpu.SMEM),
                  pl.BlockSpec((SCATTER_ROWS, D_MODEL), lambda i, zb, zv: (i, 0))],
        out_specs=pl.BlockSpec(memory_space=pl.ANY),
        scratch_shapes=[pltpu.VMEM((MOE_ROWS, D_MODEL), F32), pltpu.SemaphoreType.DMA(())],
    )
    return pl.pallas_call(
        _scatter_rows_body,
        grid_spec=grid_spec,
        out_shape=jax.ShapeDtypeStruct((cap, D_MODEL), F32),
        compiler_params=_params("arbitrary", disable_bounds_checks=True),
        name="moe_scatter",
    )(zero_blocks, zero_valid, _block_rows(dest, SCATTER_ROWS), h2)


def _expert_body(layer, be_ref, meta_ref, nu_ref, xb_ref, wg_hbm, wu_hbm, wd_hbm, o_ref,
                 land_g, land_u, land_d, wg_s, wu_s, wd_s, sems):
    i = pl.program_id(0)
    used = i < nu_ref[0]
    meta = meta_ref[i]
    first = (meta & 1) != 0
    final = (meta & 2) != 0
    has_next = (meta & 4) != 0
    slot = (meta >> 3) & 1
    nxt = meta >> 4

    def fetch(e):
        return [pltpu.make_async_copy(w.at[layer, e], land, sems.at[k])
                for k, (w, land) in enumerate(((wg_hbm, land_g), (wu_hbm, land_u), (wd_hbm, land_d)))]

    def cast_to(s):
        wg_s[s] = land_g[...].astype(BF16)
        wu_s[s] = land_u[...].astype(BF16)
        wd_s[s] = land_d[...].astype(BF16)

    @pl.when(i == 0)
    def _():
        for c in fetch(be_ref[0]):
            c.start()
        for c in fetch(be_ref[0]):
            c.wait()
        cast_to(0)

    @pl.when(used & first & has_next)
    def _():
        for c in fetch(nxt):
            c.start()

    @pl.when(used)
    def _():
        xb = xb_ref[...].astype(BF16)
        gate = _dot(xb, wg_s[slot])
        up = _dot(xb, wu_s[slot])
        act = gate * _sigmoid(gate) * up
        o_ref[...] = _dot(act.astype(BF16), wd_s[slot])

    @pl.when(used & final & has_next)
    def _():
        for c in fetch(nxt):
            c.wait()
        cast_to(1 - slot)

    @pl.when(jnp.logical_not(used))
    def _():
        o_ref[...] = jnp.zeros_like(o_ref)


def _experts(xb, block_e, block_meta, n_used, exp_gate, exp_up, exp_down, layer):
    cap = xb.shape[0]
    nb = cap // MOE_ROWS
    blk = lambda i, nu: jnp.minimum(i, nu[0] - 1)
    grid_spec = pltpu.PrefetchScalarGridSpec(
        num_scalar_prefetch=3,
        grid=(nb,),
        in_specs=[pl.BlockSpec((MOE_ROWS, D_MODEL), lambda i, be, meta, nu: (blk(i, nu), 0)),
                  pl.BlockSpec(memory_space=pl.ANY), pl.BlockSpec(memory_space=pl.ANY),
                  pl.BlockSpec(memory_space=pl.ANY)],
        out_specs=pl.BlockSpec((MOE_ROWS, D_MODEL), lambda i, be, meta, nu: (i, 0)),
        scratch_shapes=[pltpu.VMEM((D_MODEL, D_EXPERT), F32), pltpu.VMEM((D_MODEL, D_EXPERT), F32),
                        pltpu.VMEM((D_EXPERT, D_MODEL), F32),
                        pltpu.VMEM((2, D_MODEL, D_EXPERT), BF16), pltpu.VMEM((2, D_MODEL, D_EXPERT), BF16),
                        pltpu.VMEM((2, D_EXPERT, D_MODEL), BF16), pltpu.SemaphoreType.DMA((3,))],
    )
    return pl.pallas_call(
        functools.partial(_expert_body, layer),
        grid_spec=grid_spec,
        out_shape=jax.ShapeDtypeStruct((cap, D_MODEL), F32),
        compiler_params=_params("arbitrary"),
        name="experts",
    )(block_e, block_meta, n_used, xb, exp_gate, exp_up, exp_down)


def _block_rows(dest, rows):
    n = dest.shape[1]
    return jnp.stack([dest[0].reshape(n // rows, rows), dest[1].reshape(n // rows, rows)],
                     axis=1).reshape(n // rows, 1, 2 * rows)


def _dispatch(route_t, counts):
    n = route_t.shape[1]
    counts = counts[0, :N_EXPERTS].astype(jnp.int32)
    pcounts = (counts + MOE_ROWS - 1) // MOE_ROWS * MOE_ROWS
    pend = jnp.cumsum(pcounts)
    pstart = pend - pcounts
    experts = jnp.arange(N_EXPERTS, dtype=jnp.int32)
    dest = jnp.stack([
        jnp.sum(jnp.where(route_t[2 + k].astype(jnp.int32)[None, :] == experts[:, None], pstart[:, None], 0),
                axis=0) + route_t[4 + k].astype(jnp.int32) for k in range(2)])
    nb = -(-2 * n // MOE_ROWS) + N_EXPERTS
    block_start = jnp.arange(nb, dtype=jnp.int32) * MOE_ROWS
    block_e = jnp.minimum(jnp.sum((pend[None, :] <= block_start[:, None]).astype(jnp.int32), axis=1),
                          N_EXPERTS - 1)
    n_used = pend[-1] // MOE_ROWS
    j = jnp.arange(nb, dtype=jnp.int32)
    used = j < n_used
    prev_e = jnp.concatenate([block_e[:1] - 1, block_e[:-1]])
    next_blk_e = jnp.concatenate([block_e[1:], block_e[-1:] + 1])
    first = used & (prev_e != block_e)
    final = used & ((j == n_used - 1) | (next_blk_e != block_e))
    later = (experts[None, :] > experts[:, None]) & (pcounts[None, :] > 0)
    next_expert = jnp.min(jnp.where(later, experts[None, :], N_EXPERTS), axis=1)
    nxt = jnp.sum(jnp.where(block_e[:, None] == experts[None, :], next_expert[None, :], 0), axis=1)
    slot = (jnp.cumsum(first.astype(jnp.int32)) - 1) & 1
    block_meta = (first.astype(jnp.int32) | (final.astype(jnp.int32) << 1)
                  | ((nxt < N_EXPERTS).astype(jnp.int32) << 2) | (slot << 3)
                  | (jnp.minimum(nxt, N_EXPERTS - 1) << 4)).astype(jnp.int32)
    spare = n_used + experts
    zero_blocks = jnp.concatenate([pend // MOE_ROWS - 1, spare]).astype(jnp.int32)
    zero_valid = jnp.concatenate([pcounts > 0, spare < nb]).astype(jnp.int32)
    return dest, block_e, block_meta, n_used.astype(jnp.int32).reshape(1), zero_blocks, zero_valid, nb * MOE_ROWS


def _combine_body(final_norm, dest_ref, next_ref, x_ref, route_ref, g2_ref, fg_ref, yb_ref, o_ref, buf, sems):
    i = pl.program_id(0)
    tb = x_ref.shape[0]
    slot = i % 2

    def gather(idx_ref, s, wait, unrolled=False):
        def row_copy(t, k):
            return pltpu.make_async_copy(yb_ref.at[pl.ds(idx_ref[0, 0, k * tb + t], 1), :],
                                         buf.at[s, k, pl.ds(t, 1), :], sems.at[s])

        def body(t, carry):
            for k in range(2):
                if wait:
                    row_copy(t, k).wait()
                else:
                    row_copy(t, k).start()
            return carry

        if unrolled:
            for t in range(tb):
                body(t, 0)
        else:
            lax.fori_loop(0, tb, body, 0, unroll=DMA_UNROLL)

    @pl.when(i == 0)
    def _():
        gather(dest_ref, slot, False)

    for s in range(2):
        @pl.when(jnp.logical_and(i + 1 < pl.num_programs(0), slot == 1 - s))
        def _():
            gather(next_ref, s, False, unrolled=True)

    gather(dest_ref, slot, True)
    route = route_ref[...]
    x = x_ref[...] + g2_ref[0] * (route[:, 0:1] * buf[slot, 0] + route[:, 1:2] * buf[slot, 1])
    if final_norm:
        x = _rms(x) * fg_ref[...]
    o_ref[...] = x


def _combine(xa, yb, route, dest3, g2, final_g, blk, ent, n_out_blocks, final_norm):
    nxt = lambda i: blk(jnp.minimum(i + 1, n_out_blocks - 1))
    return pl.pallas_call(
        functools.partial(_combine_body, final_norm),
        grid=(n_out_blocks,),
        in_specs=[pl.BlockSpec((1, 1, 2 * ROW_BLOCK), lambda i: (blk(i), 0, 0), memory_space=pltpu.SMEM),
                  pl.BlockSpec((1, 1, 2 * ROW_BLOCK), lambda i: (nxt(i), 0, 0), memory_space=pltpu.SMEM),
                  pl.BlockSpec((ROW_BLOCK, D_MODEL), lambda i: (blk(i), 0)),
                  pl.BlockSpec((ROW_BLOCK, LANES), lambda i: (blk(i), 0)),
                  pl.BlockSpec((1, 1, D_MODEL), lambda i: (ent(blk(i)), 0, 0)),
                  pl.BlockSpec(final_g.shape, lambda i: (0, 0)),
                  pl.BlockSpec(memory_space=pl.ANY)],
        out_specs=pl.BlockSpec((ROW_BLOCK, D_MODEL), lambda i: (i, 0)),
        out_shape=jax.ShapeDtypeStruct((n_out_blocks * ROW_BLOCK, D_MODEL), F32),
        scratch_shapes=[pltpu.VMEM((2, 2, ROW_BLOCK, D_MODEL), F32), pltpu.SemaphoreType.DMA((2,))],
        compiler_params=_params("arbitrary", disable_bounds_checks=True),
        name="moe_combine",
    )(dest3, dest3, xa, route, g2, final_g, yb)


def _block_diag2(w):
    k, n = w.shape[1], w.shape[2]
    z = jnp.zeros((k, n), w.dtype)
    return jnp.concatenate([jnp.concatenate([w[0], z], axis=1), jnp.concatenate([z, w[1]], axis=1)], axis=0)


def kernel(x, c, ctx, c_ctx, ada_w, ada_b, norm1_g, norm2_g, w_in, mu_shift, decay_w0, decay_w2, iclr_a0, iclr_a2, gate_g2, k_k, k_a, r_k, gn_w, gn_b, conv_w, conv_gain, four_gain, w_out, router_g_w, router_g_b, router_e_w, router_e_b, exp_gate, exp_up, exp_down, final_g):
    bsz, seq, d = x.shape
    ctx_len = ctx.shape[1]
    depth = ada_w.shape[0]
    rows_b = ctx_len + seq
    n = bsz * rows_b
    bpb = rows_b // ROW_BLOCK
    lat_bpb = seq // ROW_BLOCK
    ent = lambda i: jnp.where(i % bpb == 0, bsz, i // bpb)

    xs = (x.reshape(bsz * seq, d), ctx.reshape(bsz * ctx_len, d))
    cvec = jnp.concatenate([c, c_ctx[None, :]], axis=0)
    cvec = cvec * jax.nn.sigmoid(cvec)
    head_id = np.arange(LANES) // HEAD
    ones_bd = jnp.asarray(head_id[:, None] == head_id[None, :], BF16)
    row2 = lambda a: a.reshape(1, -1)

    for l in range(depth):
        last = l == depth - 1
        mod = jnp.dot(cvec, ada_w[l], precision=lax.Precision.HIGHEST) + ada_b[l]
        sh1, sc1, g1, sh2, sc2, g2 = [mod[:, j * d:(j + 1) * d].reshape(bsz + 1, 1, d) for j in range(6)]

        pz, pcv, pfo_x, pfo_c = _inproj(xs, row2(norm1_g[l]), sc1, sh1, w_in[l].astype(BF16), ent, bsz, bpb)
        g, bonus, pm, qm, rp, y0 = _rwkv_chunks(
            pz, seq, ctx_len, bsz, row2(mu_shift[l]), _block_diag2(decay_w2[l]), _block_diag2(iclr_a2[l]),
            gate_g2[l], row2(decay_w0[l]), row2(iclr_a0[l]), row2(k_k[l]), row2(k_a[l]), row2(r_k[l]), ones_bd)
        yf, yb = _chunk_scan(pm, qm, rp, y0, seq, ctx_len, bsz)

        fgain = row2(four_gain[l])
        four_x = _fourier_long(pfo_x.reshape(bsz, seq, D_FOUR), fgain).reshape(bsz * seq, D_FOUR)
        four_c = _fourier_direct(pfo_c.reshape(bsz, ctx_len, D_FOUR), fgain).reshape(bsz * ctx_len, D_FOUR)

        wr = jnp.zeros((d, LANES), F32).at[:, :N_GROUPS].set(router_g_w[l])
        wr = wr.at[:, N_GROUPS:N_GROUPS + N_EXPERTS].set(router_e_w[l])
        rb = jnp.zeros((1, LANES), F32).at[0, :N_GROUPS].set(router_g_b[l])
        rb = rb.at[0, N_GROUPS:N_GROUPS + N_EXPERTS].set(router_e_b[l])
        wrh, wrl = _hilo(wr)
        xa, h2, route, route_t, counts = _post(yf, yb, bonus, g, pcv, four_x, four_c, xs, seq, ctx_len, ent,
                                      w_out[l].astype(BF16), row2(gn_w[l]), row2(gn_b[l]), conv_w[l],
                                      row2(conv_gain[l]), g1, sc2, sh2, row2(norm2_g[l]), wrh, wrl, rb, ones_bd)

        dest, block_e, block_meta, n_used, zero_blocks, zero_valid, cap = _dispatch(route_t, counts)
        xe = _scatter_rows(h2, dest, zero_blocks, zero_valid, cap)
        ye = _experts(xe, block_e, block_meta, n_used, exp_gate, exp_up, exp_down, l)
        dest3 = _block_rows(dest, ROW_BLOCK)
        if last:
            blk = lambda i: (i // lat_bpb) * bpb + 1 + i % lat_bpb
            xa = _combine(xa, ye, route, dest3, g2, row2(final_g), blk, ent, bsz * lat_bpb, True)
        else:
            xa = _combine(xa, ye, route, dest3, g2, row2(final_g), lambda i: i, ent, n // ROW_BLOCK, False)
            xs = (xa,)

    return xa.reshape(bsz, seq, d)
```
